```python
import math
import jax
import jax.numpy as jnp
from jax import lax
import numpy as np

D_MODEL = 2048
BATCH = 4
SEQ = 2048
DEPTH = 2
DEC_BATCH = 128
DEC_SEQ = 4
PAST_LEN = 2048
PAGE_SIZE = 128

HEAD_DIM = 128
N_MIX_HEADS = D_MODEL // HEAD_DIM
N_CROSS_HEADS = 4
N_SELF_HEADS = N_MIX_HEADS - N_CROSS_HEADS
NSA_KV_HEADS = 4
NSA_GROUP = N_SELF_HEADS // NSA_KV_HEADS
CMP_BLOCK = 64
SLC_BLOCK = 64
N_SEL = 16
WINDOW = 512
CMP_HIDDEN = 256
FOX_HEADS = N_SELF_HEADS
MEM_LEN = 256
N_BUCKETS = 32
MAX_DISTANCE = 128
FFN_DIM = ((8 * D_MODEL) // 3 + 127) // 128 * 128
RMS_EPS = 1e-6
FORGET_BIAS = 3.0
N_NSA_LAYERS = (DEPTH + 1) // 2
N_FOX_LAYERS = DEPTH // 2
Q_BLOCK = 128
SLC_Q_BLOCK = 32
FORCE_SCORE = 1e4
NEG_INF = -1e30
ATTN_SCALE = HEAD_DIM ** -0.5
SELF_W = N_SELF_HEADS * HEAD_DIM
NSA_KV_W = NSA_KV_HEADS * HEAD_DIM
CROSS_W = N_CROSS_HEADS * HEAD_DIM
MIX_W = SELF_W + CROSS_W
NSA_SIZES = (SELF_W,) + (NSA_KV_W,) * 6 + (3 * N_SELF_HEADS, CROSS_W)
FOX_SIZES = (SELF_W,) * 3 + (FOX_HEADS, CROSS_W)
NSA_IN = sum(NSA_SIZES)
FOX_IN = sum(FOX_SIZES)

kernel_name = 'nsa_fox_macaron_hybrid_step'


def rmsnorm(x, g):
    xf = x.astype(jnp.float32)
    y = xf * lax.rsqrt(jnp.mean(xf * xf, axis=-1, keepdims=True) + RMS_EPS)
    return (y * g.astype(jnp.float32)).astype(x.dtype)


def swiglu(x, w_gu, w_down):
    a, b = jnp.split(x @ w_gu, 2, axis=-1)
    return (jax.nn.silu(a) * b) @ w_down


def _split(x, sizes):
    offs, acc = [], 0
    for s in sizes[:-1]:
        acc += s
        offs.append(acc)
    return jnp.split(x, offs, axis=-1)


def t5_bucket(dist):
    n = jnp.maximum(dist, 0)
    max_exact = N_BUCKETS // 2
    nf = jnp.maximum(n, 1).astype(jnp.float32)
    large = max_exact + (jnp.log(nf / max_exact) / math.log(MAX_DISTANCE / max_exact)
                         * (N_BUCKETS - max_exact)).astype(jnp.int32)
    return jnp.where(n < max_exact, n, jnp.minimum(large, N_BUCKETS - 1))


def t5_bias(rel_bias, dist):
    b = rel_bias[t5_bucket(dist)].astype(jnp.float32)
    nq, nk = dist.shape
    return b.reshape(nq, nk, NSA_KV_HEADS, NSA_GROUP).transpose(2, 3, 0, 1)


def attn_probs(q, k, bias, mask):
    s = jnp.einsum('bqgrd,bkgd->bgrqk', q, k).astype(jnp.float32) * ATTN_SCALE + bias
    s = jnp.where(mask, s, NEG_INF)
    return jnp.where(mask, jax.nn.softmax(s, axis=-1), 0.0)


def attend(q, k, v, bias, mask):
    p = attn_probs(q, k, bias, mask)
    return jnp.einsum('bgrqk,bkgd->bqgrd', p.astype(v.dtype), v)


def memory_kv(mem, g, w):
    b, m, _ = mem.shape
    kv = (rmsnorm(mem, g) @ w).reshape(b, m, 2, N_CROSS_HEADS, HEAD_DIM)
    return kv[:, :, 0], kv[:, :, 1]


def mix_out(o_self, qx, mk, mv, w_out):
    b, t = o_self.shape[:2]
    o_x = attend(qx[:, :, :, None], mk, mv, 0.0, True).reshape(b, t, CROSS_W)
    return jnp.concatenate([o_self.reshape(b, t, SELF_W), o_x], axis=-1) @ w_out


def _split_nsa(proj):
    b, t, _ = proj.shape
    parts = _split(proj, NSA_SIZES)
    q = parts[0].reshape(b, t, N_SELF_HEADS, HEAD_DIM)
    kvs = [p.reshape(b, t, NSA_KV_HEADS, HEAD_DIM) for p in parts[1:7]]
    return q, kvs, parts[7], parts[8].reshape(b, t, N_CROSS_HEADS, HEAD_DIM)


def nsa_compress(rows, w1, w2, pe):
    b, t, g, d = rows.shape
    nb = t // CMP_BLOCK
    blk = rows.reshape(b, nb, CMP_BLOCK, g, d) + pe[None, None, :, None, :]
    flat = blk.transpose(0, 1, 3, 2, 4).reshape(b, nb, g, CMP_BLOCK * d)
    return jax.nn.silu(flat @ w1) @ w2


def nsa_cmp_branch(q, q_pos, ck, cv, rel_bias):
    b, nq = q.shape[:2]
    nb = ck.shape[1]
    qg = q.reshape(b, nq, NSA_KV_HEADS, NSA_GROUP, HEAD_DIM)
    blk_end = jnp.arange(nb) * CMP_BLOCK + (CMP_BLOCK - 1)
    dist = q_pos[:, None] - blk_end[None, :]
    p = attn_probs(qg, ck, t5_bias(rel_bias, dist), dist >= 0)
    o = jnp.einsum('bgrqk,bkgd->bqgrd', p.astype(cv.dtype), cv).reshape(b, nq, N_SELF_HEADS, HEAD_DIM)
    return o, p.sum(axis=2)


def nsa_select(imp, q_pos):
    nb = imp.shape[-1]
    cur = (q_pos // SLC_BLOCK)[:, None]
    j = jnp.arange(nb)[None, :]
    forced = (j == 0) | (j == cur) | (j == cur - 1)
    score = jnp.where(j <= cur, jnp.where(forced, FORCE_SCORE, imp), -1.0)
    _, idx = lax.top_k(score, min(N_SEL, nb))
    return idx, idx <= cur[None, None]


def nsa_slc_core(q, q_pos, kb, vb, idx, valid, rel_bias):
    nq = q.shape[0]
    qg = q.reshape(nq, NSA_KV_HEADS, NSA_GROUP, HEAD_DIM)
    s = jnp.einsum('qgrd,gqnld->grqnl', qg, kb).astype(jnp.float32) * ATTN_SCALE
    kpos = idx[..., None] * SLC_BLOCK + jnp.arange(SLC_BLOCK)
    dist = q_pos[None, :, None, None] - kpos
    mask = (valid[..., None] & (dist >= 0))[:, None]
    tab = rel_bias.reshape(N_BUCKETS, NSA_KV_HEADS, NSA_GROUP).transpose(1, 0, 2)
    bias = tab[jnp.arange(NSA_KV_HEADS)[:, None, None, None], t5_bucket(dist)]
    s = jnp.where(mask, s + bias.transpose(0, 4, 1, 2, 3).astype(jnp.float32), NEG_INF)
    p = jnp.where(mask, jax.nn.softmax(s, axis=(-2, -1)), 0.0)
    o = jnp.einsum('grqnl,gqnld->qgrd', p.astype(vb.dtype), vb)
    return o.reshape(nq, N_SELF_HEADS, HEAD_DIM)


def nsa_slc_prompt(q, ks, vs, idx, valid, rel_bias):
    b, s_len = q.shape[:2]
    nb = s_len // SLC_BLOCK
    n = idx.shape[-1]
    kbl = ks.reshape(b, nb, SLC_BLOCK, NSA_KV_HEADS, HEAD_DIM)
    vbl = vs.reshape(b, nb, SLC_BLOCK, NSA_KV_HEADS, HEAD_DIM)
    nch = s_len // SLC_Q_BLOCK
    bi = jnp.arange(b)[:, None, None, None, None]
    gi = jnp.arange(NSA_KV_HEADS)[None, :, None, None, None]
    li = jnp.arange(SLC_BLOCK)
    core = jax.vmap(nsa_slc_core, in_axes=(0, None, 0, 0, 0, 0, None))

    def chunk(args):
        qc, qp, ic, vc = args
        ji = ic[..., None]
        return core(qc, qp, kbl[bi, ji, li, gi], vbl[bi, ji, li, gi], ic, vc, rel_bias)

    xs = (q.reshape(b, nch, SLC_Q_BLOCK, N_SELF_HEADS, HEAD_DIM).swapaxes(0, 1),
          jnp.arange(s_len).reshape(nch, SLC_Q_BLOCK),
          idx.reshape(b, NSA_KV_HEADS, nch, SLC_Q_BLOCK, n).transpose(2, 0, 1, 3, 4),
          valid.reshape(b, NSA_KV_HEADS, nch, SLC_Q_BLOCK, n).transpose(2, 0, 1, 3, 4))
    out = lax.map(chunk, xs)
    return out.swapaxes(0, 1).reshape(b, s_len, N_SELF_HEADS, HEAD_DIM)


def nsa_win_prompt(q, kw, vw, rel_bias):
    b, s_len = q.shape[:2]
    nqb = s_len // Q_BLOCK
    band = WINDOW + Q_BLOCK
    pad = ((0, 0), (WINDOW, 0), (0, 0), (0, 0))
    band_idx = jnp.arange(nqb)[:, None] * Q_BLOCK + jnp.arange(band)[None, :]
    kb = jnp.pad(kw, pad)[:, band_idx]
    vb = jnp.pad(vw, pad)[:, band_idx]
    qb = q.reshape(b, nqb, Q_BLOCK, NSA_KV_HEADS, NSA_GROUP, HEAD_DIM)
    qpos = jnp.arange(s_len).reshape(nqb, Q_BLOCK)

    def blk(qi, ki, vi, qp, kp):
        dist = qp[:, None] - kp[None, :]
        mask = (dist >= 0) & (dist < WINDOW) & (kp >= 0)[None, :]
        return attend(qi, ki, vi, t5_bias(rel_bias, dist), mask)

    o = jax.vmap(blk, in_axes=(1, 1, 1, 0, 0), out_axes=1)(qb, kb, vb, qpos, band_idx - WINDOW)
    return o.reshape(b, s_len, N_SELF_HEADS, HEAD_DIM)


def nsa_win_sample(q, q_pos, kw, vw, buf_k, buf_v, rel_bias):
    b, nq = q.shape[:2]
    wb = buf_k.shape[1]
    keys = jnp.concatenate([buf_k, kw], axis=1)
    vals = jnp.concatenate([buf_v, vw], axis=1)
    kp = jnp.concatenate([PAST_LEN - wb + jnp.arange(wb), q_pos])
    dist = q_pos[:, None] - kp[None, :]
    mask = (dist >= 0) & (dist < WINDOW)
    qg = q.reshape(b, nq, NSA_KV_HEADS, NSA_GROUP, HEAD_DIM)
    o = attend(qg, keys, vals, t5_bias(rel_bias, dist), mask).reshape(b, nq, N_SELF_HEADS, HEAD_DIM)
    keep = min(WINDOW, wb + nq)
    return o, keys[:, -keep:], vals[:, -keep:]


def nsa_gate(gl, b_gate, o_c, o_s, o_w):
    b, t = gl.shape[:2]
    g = jax.nn.sigmoid((gl + b_gate).astype(jnp.float32)).reshape(b, t, N_SELF_HEADS, 3).astype(o_c.dtype)
    return g[..., 0:1] * o_c + g[..., 1:2] * o_s + g[..., 2:3] * o_w


def nsa_prompt_mixer(h, w_in, b_gate, w_cmp1, w_cmp2, cmp_pe, rel_bias):
    s_len = h.shape[1]
    q, (kc, vc, ks, vs, kw, vw), gl, qx = _split_nsa(h @ w_in)
    q_pos = jnp.arange(s_len)
    ck = nsa_compress(kc, w_cmp1[0], w_cmp2[0], cmp_pe[0])
    cv = nsa_compress(vc, w_cmp1[1], w_cmp2[1], cmp_pe[1])
    o_c, imp = nsa_cmp_branch(q, q_pos, ck, cv, rel_bias)
    idx, valid = nsa_select(imp, q_pos)
    o_s = nsa_slc_prompt(q, ks, vs, idx, valid, rel_bias)
    o_w = nsa_win_prompt(q, kw, vw, rel_bias)
    keep = min(WINDOW, s_len)
    return nsa_gate(gl, b_gate, o_c, o_s, o_w), qx, (kc, vc, ks, vs, kw[:, -keep:], vw[:, -keep:])


def nsa_sample_mixer(h, q_pos, pool_ck, pool_cv, pool_sk, pool_sv, buf_k, buf_v, page_table,
                     w_in, b_gate, w_cmp1, w_cmp2, cmp_pe, rel_bias):
    q, (kc, vc, ks, vs, kw, vw), gl, qx = _split_nsa(h @ w_in)
    nq = q.shape[1]
    t_len = PAST_LEN + nq
    nb = -(-t_len // CMP_BLOCK)
    n_past_blk = PAST_LEN // SLC_BLOCK
    bpp = PAGE_SIZE // SLC_BLOCK
    n_new_blk = -(-nq // SLC_BLOCK)
    blk_sk = pool_sk.reshape(-1, SLC_BLOCK, NSA_KV_HEADS, HEAD_DIM)
    blk_sv = pool_sv.reshape(-1, SLC_BLOCK, NSA_KV_HEADS, HEAD_DIM)
    gi = jnp.arange(NSA_KV_HEADS)[:, None, None, None]
    li = jnp.arange(SLC_BLOCK)
    pad_new = ((0, n_new_blk * SLC_BLOCK - nq), (0, 0), (0, 0))

    def one(args):
        q1, kc1, vc1, ks1, vs1, pt1 = args

        def rows(pool, new):
            past = pool[pt1].reshape(PAST_LEN, NSA_KV_HEADS, HEAD_DIM)
            full = jnp.concatenate([past, new], axis=0)
            return jnp.pad(full, ((0, nb * CMP_BLOCK - t_len), (0, 0), (0, 0)))[None]

        ck = nsa_compress(rows(pool_ck, kc1), w_cmp1[0], w_cmp2[0], cmp_pe[0])
        cv = nsa_compress(rows(pool_cv, vc1), w_cmp1[1], w_cmp2[1], cmp_pe[1])
        o_c, imp = nsa_cmp_branch(q1[None], q_pos, ck, cv, rel_bias)
        idx, valid = nsa_select(imp, q_pos)
        idx, valid = idx[0], valid[0]
        jp = jnp.minimum(idx, n_past_blk - 1)
        phys = (pt1[jp // bpp] * bpp + jp % bpp)[..., None]
        jn = jnp.clip(idx - n_past_blk, 0, n_new_blk - 1)[..., None]
        is_past = (idx < n_past_blk)[..., None, None]
        new_k = jnp.pad(ks1, pad_new).reshape(n_new_blk, SLC_BLOCK, NSA_KV_HEADS, HEAD_DIM)
        new_v = jnp.pad(vs1, pad_new).reshape(n_new_blk, SLC_BLOCK, NSA_KV_HEADS, HEAD_DIM)
        kb = jnp.where(is_past, blk_sk[phys, li, gi], new_k[jn, li, gi])
        vb = jnp.where(is_past, blk_sv[phys, li, gi], new_v[jn, li, gi])
        return o_c[0], nsa_slc_core(q1, q_pos, kb, vb, idx, valid, rel_bias)

    o_c, o_s = lax.map(one, (q, kc, vc, ks, vs, page_table))
    o_w, nbk, nbv = nsa_win_sample(q, q_pos, kw, vw, buf_k, buf_v, rel_bias)
    return nsa_gate(gl, b_gate, o_c, o_s, o_w), qx, (kc, vc, ks, vs, nbk, nbv)


def _split_fox(proj):
    b, t, _ = proj.shape
    q, k, v, fl, qx = _split(proj, FOX_SIZES)
    hd = (b, t, FOX_HEADS, HEAD_DIM)
    return q.reshape(hd), k.reshape(hd), v.reshape(hd), fl, qx.reshape(b, t, N_CROSS_HEADS, HEAD_DIM)


def fox_bias(cq, ck):
    return cq.transpose(0, 2, 1)[:, :, None, :, None] - ck.transpose(0, 2, 1)[:, :, None, None, :]


def fox_prompt_mixer(h, w_in, b_forget):
    q, k, v, fl, qx = _split_fox(h @ w_in)
    logf = jax.nn.log_sigmoid((fl + b_forget).astype(jnp.float32))
    b, s_len = q.shape[:2]
    c = jnp.cumsum(logf, axis=1)
    nqb = s_len // Q_BLOCK
    kpos = jnp.arange(s_len)

    def blk(args):
        qb, cq, qp = args
        mask = kpos[None, :] <= qp[:, None]
        return attend(qb[:, :, :, None], k, v, fox_bias(cq, c), mask)[:, :, :, 0]

    xs = (q.reshape(b, nqb, Q_BLOCK, FOX_HEADS, HEAD_DIM).swapaxes(0, 1),
          c.reshape(b, nqb, Q_BLOCK, FOX_HEADS).swapaxes(0, 1),
          kpos.reshape(nqb, Q_BLOCK))
    o = lax.map(blk, xs).swapaxes(0, 1).reshape(b, s_len, FOX_HEADS, HEAD_DIM)
    return o, qx, (k, v, logf)


def fox_sample_mixer(h, q_pos, pool_k, pool_v, pool_logf, page_table, w_in, b_forget):
    q, k, v, fl, qx = _split_fox(h @ w_in)
    logf = jax.nn.log_sigmoid((fl + b_forget).astype(jnp.float32))
    nq = q.shape[1]
    kpos = jnp.arange(PAST_LEN + nq)
    mask = kpos[None, :] <= q_pos[:, None]

    def one(args):
        q1, k1, v1, lf1, pt1 = args
        kk = jnp.concatenate([pool_k[pt1].reshape(PAST_LEN, FOX_HEADS, HEAD_DIM), k1], axis=0)[None]
        vv = jnp.concatenate([pool_v[pt1].reshape(PAST_LEN, FOX_HEADS, HEAD_DIM), v1], axis=0)[None]
        lf = jnp.concatenate([pool_logf[pt1].reshape(PAST_LEN, FOX_HEADS).astype(jnp.float32), lf1], axis=0)[None]
        c = jnp.cumsum(lf, axis=1)
        return attend(q1[None, :, :, None], kk, vv, fox_bias(c[:, PAST_LEN:], c), mask)[0, :, :, 0]

    o = lax.map(one, (q, k, v, logf, page_table))
    return o, qx, (k, v, logf)


def setup_inputs(seed: int = 0) -> dict:
    key = jax.random.key(seed)
    ks = jax.random.split(key, 32)

    def nrm(i, shape, scale):
        return jax.random.normal(ks[i], shape, jnp.float32) * scale

    n_pages = PAST_LEN // PAGE_SIZE
    n_pool = (DEC_BATCH * n_pages * 5) // 4
    w_buf = min(WINDOW, PAST_LEN)
    kv_nsa = (N_NSA_LAYERS, n_pool, PAGE_SIZE, NSA_KV_HEADS, HEAD_DIM)
    kv_fox = (N_FOX_LAYERS, n_pool, PAGE_SIZE, FOX_HEADS, HEAD_DIM)
    win = (N_NSA_LAYERS, DEC_BATCH, w_buf, NSA_KV_HEADS, HEAD_DIM)
    memc = (DEPTH, DEC_BATCH, MEM_LEN, N_CROSS_HEADS, HEAD_DIM)
    page_table = jax.random.permutation(ks[0], n_pool)[: DEC_BATCH * n_pages].reshape(
        DEC_BATCH, n_pages).astype(jnp.int32)
    return {
        'x_prompt': nrm(1, (BATCH, SEQ, D_MODEL), 1.0),
        'x_sample': nrm(2, (DEC_BATCH, DEC_SEQ, D_MODEL), 1.0),
        'mem_prompt': nrm(3, (BATCH, MEM_LEN, D_MODEL), 1.0),
        'cache_nsa_cmp_k': nrm(4, kv_nsa, 1.0),
        'cache_nsa_cmp_v': nrm(5, kv_nsa, 1.0),
        'cache_nsa_slc_k': nrm(6, kv_nsa, 1.0),
        'cache_nsa_slc_v': nrm(7, kv_nsa, 1.0),
        'cache_nsa_win_k': nrm(8, win, 1.0),
        'cache_nsa_win_v': nrm(9, win, 1.0),
        'cache_fox_k': nrm(10, kv_fox, 1.0),
        'cache_fox_v': nrm(11, kv_fox, 1.0),
        'cache_fox_logf': jax.nn.log_sigmoid(FORGET_BIAS + nrm(12, kv_fox[:4], 1.0)),
        'cache_mem_k': nrm(13, memc, 1.0),
        'cache_mem_v': nrm(14, memc, 1.0),
        'page_table': page_table,
        'rel_bias': nrm(15, (N_BUCKETS, N_SELF_HEADS), 0.5),
        'norm_g': 1.0 + nrm(16, (DEPTH, 3, D_MODEL), 0.02),
        'mem_norm_g': 1.0 + nrm(17, (DEPTH, D_MODEL), 0.02),
        'w_mem_kv': nrm(18, (DEPTH, D_MODEL, 2 * CROSS_W), D_MODEL ** -0.5),
        'w_ff_gu': nrm(19, (DEPTH, 2, D_MODEL, 2 * FFN_DIM), D_MODEL ** -0.5),
        'w_ff_down': nrm(20, (DEPTH, 2, FFN_DIM, D_MODEL), FFN_DIM ** -0.5),
        'w_in_nsa': nrm(21, (N_NSA_LAYERS, D_MODEL, NSA_IN), D_MODEL ** -0.5),
        'b_gate_nsa': nrm(22, (N_NSA_LAYERS, 3 * N_SELF_HEADS), 0.1),
        'w_cmp1': nrm(23, (N_NSA_LAYERS, 2, CMP_BLOCK * HEAD_DIM, CMP_HIDDEN), (CMP_BLOCK * HEAD_DIM) ** -0.5),
        'w_cmp2': nrm(24, (N_NSA_LAYERS, 2, CMP_HIDDEN, HEAD_DIM), CMP_HIDDEN ** -0.5),
        'cmp_pe': nrm(25, (N_NSA_LAYERS, 2, CMP_BLOCK, HEAD_DIM), 0.5),
        'w_out_nsa': nrm(26, (N_NSA_LAYERS, MIX_W, D_MODEL), MIX_W ** -0.5),
        'w_in_fox': nrm(27, (N_FOX_LAYERS, D_MODEL, FOX_IN), D_MODEL ** -0.5),
        'b_forget': FORGET_BIAS + nrm(28, (N_FOX_LAYERS, FOX_HEADS), 0.5),
        'w_out_fox': nrm(29, (N_FOX_LAYERS, MIX_W, D_MODEL), MIX_W ** -0.5),
        'final_norm_g': 1.0 + nrm(30, (D_MODEL,), 0.02),
    }


def reference(x_prompt, x_sample, mem_prompt, cache_nsa_cmp_k, cache_nsa_cmp_v, cache_nsa_slc_k,
              cache_nsa_slc_v, cache_nsa_win_k, cache_nsa_win_v, cache_fox_k, cache_fox_v, cache_fox_logf,
              cache_mem_k, cache_mem_v, page_table, rel_bias, norm_g, mem_norm_g, w_mem_kv, w_ff_gu,
              w_ff_down, w_in_nsa, b_gate_nsa, w_cmp1, w_cmp2, cmp_pe, w_out_nsa, w_in_fox, b_forget,
              w_out_fox, final_norm_g):
    xp, xs = x_prompt, x_sample
    q_pos_s = PAST_LEN + jnp.arange(x_sample.shape[1])
    nsa_p, nsa_s, fox_p, fox_s, mem_k_p, mem_v_p = [], [], [], [], [], []
    for i in range(DEPTH):
        xp = xp + 0.5 * swiglu(rmsnorm(xp, norm_g[i, 0]), w_ff_gu[i, 0], w_ff_down[i, 0])
        xs = xs + 0.5 * swiglu(rmsnorm(xs, norm_g[i, 0]), w_ff_gu[i, 0], w_ff_down[i, 0])
        hp = rmsnorm(xp, norm_g[i, 1])
        hs = rmsnorm(xs, norm_g[i, 1])
        mk_p, mv_p = memory_kv(mem_prompt, mem_norm_g[i], w_mem_kv[i])
        mem_k_p.append(mk_p)
        mem_v_p.append(mv_p)
        a = i // 2
        if i % 2 == 0:
            o_p, qx_p, st_p = nsa_prompt_mixer(hp, w_in_nsa[a], b_gate_nsa[a], w_cmp1[a], w_cmp2[a],
                                               cmp_pe[a], rel_bias)
            o_s, qx_s, st_s = nsa_sample_mixer(hs, q_pos_s, cache_nsa_cmp_k[a], cache_nsa_cmp_v[a],
                                               cache_nsa_slc_k[a], cache_nsa_slc_v[a], cache_nsa_win_k[a],
                                               cache_nsa_win_v[a], page_table, w_in_nsa[a], b_gate_nsa[a],
                                               w_cmp1[a], w_cmp2[a], cmp_pe[a], rel_bias)
            nsa_p.append(st_p)
            nsa_s.append(st_s)
            w_out = w_out_nsa[a]
        else:
            o_p, qx_p, st_p = fox_prompt_mixer(hp, w_in_fox[a], b_forget[a])
            o_s, qx_s, st_s = fox_sample_mixer(hs, q_pos_s, cache_fox_k[a], cache_fox_v[a],
                                               cache_fox_logf[a], page_table, w_in_fox[a], b_forget[a])
            fox_p.append(st_p)
            fox_s.append(st_s)
            w_out = w_out_fox[a]
        xp = xp + mix_out(o_p, qx_p, mk_p, mv_p, w_out)
        xs = xs + mix_out(o_s, qx_s, cache_mem_k[i], cache_mem_v[i], w_out)
        xp = xp + 0.5 * swiglu(rmsnorm(xp, norm_g[i, 2]), w_ff_gu[i, 1], w_ff_down[i, 1])
        xs = xs + 0.5 * swiglu(rmsnorm(xs, norm_g[i, 2]), w_ff_gu[i, 1], w_ff_down[i, 1])
    y_prompt = rmsnorm(xp, final_norm_g)
    y_sample = rmsnorm(xs, final_norm_g)

    def stack(states, j):
        return jnp.stack([st[j] for st in states])

    p_cmp_k, p_cmp_v, p_slc_k = stack(nsa_p, 0), stack(nsa_p, 1), stack(nsa_p, 2)
    p_slc_v, p_win_k, p_win_v = stack(nsa_p, 3), stack(nsa_p, 4), stack(nsa_p, 5)
    s_cmp_k, s_cmp_v, s_slc_k = stack(nsa_s, 0), stack(nsa_s, 1), stack(nsa_s, 2)
    s_slc_v, s_win_k, s_win_v = stack(nsa_s, 3), stack(nsa_s, 4), stack(nsa_s, 5)
    p_fox_k, p_fox_v, p_fox_logf = stack(fox_p, 0), stack(fox_p, 1), stack(fox_p, 2)
    s_fox_k, s_fox_v, s_fox_logf = stack(fox_s, 0), stack(fox_s, 1), stack(fox_s, 2)
    p_mem_k = jnp.stack(mem_k_p)
    p_mem_v = jnp.stack(mem_v_p)
    return (y_prompt, y_sample,
            p_cmp_k, p_cmp_v, p_slc_k, p_slc_v, p_win_k, p_win_v,
            p_fox_k, p_fox_v, p_fox_logf, p_mem_k, p_mem_v,
            s_cmp_k, s_cmp_v, s_slc_k, s_slc_v, s_win_k, s_win_v,
            s_fox_k, s_fox_v, s_fox_logf)
```

```python
import functools
import math

import jax
import jax.numpy as jnp
from jax import lax
from jax.experimental import pallas as pl
from jax.experimental.pallas import tpu as pltpu

D_MODEL = 2048
BATCH = 4
SEQ = 2048
DEPTH = 2
DEC_BATCH = 128
DEC_SEQ = 4
PAST_LEN = 2048
PAGE_SIZE = 128
HEAD_DIM = 128
N_MIX_HEADS = D_MODEL // HEAD_DIM
N_CROSS_HEADS = 4
N_SELF_HEADS = N_MIX_HEADS - N_CROSS_HEADS
NSA_KV_HEADS = 4
NSA_GROUP = N_SELF_HEADS // NSA_KV_HEADS
CMP_BLOCK = 64
SLC_BLOCK = 64
N_SEL = 16
WINDOW = 512
CMP_HIDDEN = 256
FOX_HEADS = N_SELF_HEADS
MEM_LEN = 256
N_BUCKETS = 32
MAX_DISTANCE = 128
FFN_DIM = ((8 * D_MODEL) // 3 + 127) // 128 * 128
RMS_EPS = 1e-6
Q_BLOCK = 128
SLC_Q_BLOCK = 32
FORCE_SCORE = 1e4
NEG_INF = -1e30
ATTN_SCALE = HEAD_DIM ** -0.5
SELF_W = N_SELF_HEADS * HEAD_DIM
NSA_KV_W = NSA_KV_HEADS * HEAD_DIM
CROSS_W = N_CROSS_HEADS * HEAD_DIM
MIX_W = SELF_W + CROSS_W
NSA_SIZES = (SELF_W,) + (NSA_KV_W,) * 6 + (3 * N_SELF_HEADS, CROSS_W)
FOX_SIZES = (SELF_W,) * 3 + (FOX_HEADS, CROSS_W)

LANE = 128
VMEM_LIMIT = 56 * 1024 * 1024
N_TOK = BATCH * SEQ + DEC_BATCH * DEC_SEQ
TOK_TILE = 544

F32 = jnp.float32
BF16 = jnp.bfloat16


def _rms(x, g):
    return x * lax.rsqrt(jnp.mean(x * x, axis=-1, keepdims=True) + RMS_EPS) * g


def _ffn_kernel(x_ref, g_ref, fg_ref, wg0, wg1, wu0, wu1, wd0, wd1, o_ref, h_ref, *, n_steps, final_norm):
    f = pl.program_id(1)

    @pl.when(f == 0)
    def _():
        x = x_ref[...]
        h_ref[...] = _rms(x, g_ref[...]).astype(BF16)
        o_ref[...] = x

    h = h_ref[...]
    w_gate = jnp.concatenate([wg0[...].astype(BF16), wg1[...].astype(BF16)], axis=1)
    w_up = jnp.concatenate([wu0[...].astype(BF16), wu1[...].astype(BF16)], axis=1)
    a = jnp.dot(h, w_gate, preferred_element_type=F32)
    b = jnp.dot(h, w_up, preferred_element_type=F32)
    act = a * jax.nn.sigmoid(a) * b
    lane = lax.broadcasted_iota(jnp.int32, act.shape, 1)
    act = jnp.where(jnp.logical_and(f == n_steps - 1, lane >= LANE), 0.0, act)
    w_down = jnp.concatenate([wd0[...].astype(BF16), wd1[...].astype(BF16)], axis=0)
    o_ref[...] += 0.5 * jnp.dot(act.astype(BF16), w_down, preferred_element_type=F32)

    if final_norm:
        @pl.when(f == n_steps - 1)
        def _():
            o_ref[...] = _rms(o_ref[...], fg_ref[...])


def ffn(x, g, w_gu, w_down, final_g=None):
    m, d = x.shape
    n_f = FFN_DIM // LANE
    n_steps = (n_f + 1) // 2
    tm = TOK_TILE
    final_norm = final_g is not None
    fg = final_g if final_norm else g

    def col(off, k):
        return lambda i, f: (0, off + jnp.minimum(2 * f + k, n_f - 1))

    def row(k):
        return lambda i, f: (jnp.minimum(2 * f + k, n_f - 1), 0)

    wspec = lambda im: pl.BlockSpec((d, LANE), im)
    return pl.pallas_call(
        functools.partial(_ffn_kernel, n_steps=n_steps, final_norm=final_norm),
        grid=(m // tm, n_steps),
        in_specs=[
            pl.BlockSpec((tm, d), lambda i, f: (i, 0)),
            pl.BlockSpec((1, d), lambda i, f: (0, 0)),
            pl.BlockSpec((1, d), lambda i, f: (0, 0)),
            wspec(col(0, 0)), wspec(col(0, 1)), wspec(col(n_f, 0)), wspec(col(n_f, 1)),
            pl.BlockSpec((LANE, d), row(0)), pl.BlockSpec((LANE, d), row(1)),
        ],
        out_specs=pl.BlockSpec((tm, d), lambda i, f: (i, 0)),
        out_shape=jax.ShapeDtypeStruct((m, d), F32),
        scratch_shapes=[pltpu.VMEM((tm, d), BF16)],
        compiler_params=pltpu.CompilerParams(
            dimension_semantics=("parallel", "arbitrary"), vmem_limit_bytes=VMEM_LIMIT),
        name="ffn",
    )(x, g.reshape(1, d), fg.reshape(1, d), w_gu, w_gu, w_gu, w_gu, w_down, w_down)


def _linear_kernel(*refs, norm, residual):
    if norm:
        x_ref, g_ref, w_ref = refs[:3]
        rest = refs[3:]
    else:
        x_ref, w_ref = refs[:2]
        rest = refs[2:]
    if residual:
        r_ref, o_ref, h_ref = rest
    else:
        o_ref, h_ref = rest
    j = pl.program_id(1)

    @pl.when(j == 0)
    def _():
        x = x_ref[...].astype(F32)
        if norm:
            x = _rms(x, g_ref[...])
        h_ref[...] = x.astype(BF16)

    y = jnp.dot(h_ref[...], w_ref[...].astype(BF16), preferred_element_type=F32)
    if residual:
        y = y + r_ref[...]
    o_ref[...] = y


def linear(x, w, g=None, res=None, tm=TOK_TILE, tn=512):
    m, k = x.shape
    n = w.shape[1]
    norm, residual = g is not None, res is not None
    in_specs = [pl.BlockSpec((tm, k), lambda i, j: (i, 0))]
    args = [x]
    if norm:
        in_specs.append(pl.BlockSpec((1, k), lambda i, j: (0, 0)))
        args.append(g.reshape(1, k))
    in_specs.append(pl.BlockSpec((k, tn), lambda i, j: (0, j)))
    args.append(w)
    if residual:
        in_specs.append(pl.BlockSpec((tm, tn), lambda i, j: (i, j)))
        args.append(res)
    return pl.pallas_call(
        functools.partial(_linear_kernel, norm=norm, residual=residual),
        grid=(m // tm, n // tn),
        in_specs=in_specs,
        out_specs=pl.BlockSpec((tm, tn), lambda i, j: (i, j)),
        out_shape=jax.ShapeDtypeStruct((m, n), F32),
        scratch_shapes=[pltpu.VMEM((tm, k), BF16)],
        compiler_params=pltpu.CompilerParams(
            dimension_semantics=("parallel", "arbitrary"), vmem_limit_bytes=VMEM_LIMIT),
        name="linear",
    )(*args)


def _split(x, sizes):
    offs, acc = [], 0
    for s in sizes[:-1]:
        acc += s
        offs.append(acc)
    return jnp.split(x, offs, axis=-1)


def t5_bucket(dist):
    n = jnp.maximum(dist, 0)
    max_exact = N_BUCKETS // 2
    nf = jnp.maximum(n, 1).astype(jnp.float32)
    large = max_exact + (jnp.log(nf / max_exact) / math.log(MAX_DISTANCE / max_exact)
                         * (N_BUCKETS - max_exact)).astype(jnp.int32)
    return jnp.where(n < max_exact, n, jnp.minimum(large, N_BUCKETS - 1))


def t5_bias(rel_bias, dist):
    b = rel_bias[t5_bucket(dist)].astype(jnp.float32)
    nq, nk = dist.shape
    return b.reshape(nq, nk, NSA_KV_HEADS, NSA_GROUP).transpose(2, 3, 0, 1)


def attn_probs(q, k, bias, mask):
    s = jnp.einsum('bqgrd,bkgd->bgrqk', q, k).astype(jnp.float32) * ATTN_SCALE + bias
    s = jnp.where(mask, s, NEG_INF)
    return jnp.where(mask, jax.nn.softmax(s, axis=-1), 0.0)


def attend(q, k, v, bias, mask):
    p = attn_probs(q, k, bias, mask)
    return jnp.einsum('bgrqk,bkgd->bqgrd', p.astype(v.dtype), v)


def cross_attend(o_self, qx, mk, mv):
    b, t = o_self.shape[:2]
    o_x = attend(qx[:, :, :, None], mk, mv, 0.0, True).reshape(b, t, CROSS_W)
    return jnp.concatenate([o_self.reshape(b, t, SELF_W), o_x], axis=-1)


def _split_nsa(proj):
    b, t, _ = proj.shape
    parts = _split(proj, NSA_SIZES)
    q = parts[0].reshape(b, t, N_SELF_HEADS, HEAD_DIM)
    kvs = [p.reshape(b, t, NSA_KV_HEADS, HEAD_DIM) for p in parts[1:7]]
    return q, kvs, parts[7], parts[8].reshape(b, t, N_CROSS_HEADS, HEAD_DIM)


def nsa_compress(rows, w1, w2, pe):
    b, t, g, d = rows.shape
    nb = t // CMP_BLOCK
    blk = rows.reshape(b, nb, CMP_BLOCK, g, d) + pe[None, None, :, None, :]
    flat = blk.transpose(0, 1, 3, 2, 4).reshape(b, nb, g, CMP_BLOCK * d)
    return jax.nn.silu(flat @ w1) @ w2


def nsa_cmp_branch(q, q_pos, ck, cv, rel_bias):
    b, nq = q.shape[:2]
    nb = ck.shape[1]
    qg = q.reshape(b, nq, NSA_KV_HEADS, NSA_GROUP, HEAD_DIM)
    blk_end = jnp.arange(nb) * CMP_BLOCK + (CMP_BLOCK - 1)
    dist = q_pos[:, None] - blk_end[None, :]
    p = attn_probs(qg, ck, t5_bias(rel_bias, dist), dist >= 0)
    o = jnp.einsum('bgrqk,bkgd->bqgrd', p.astype(cv.dtype), cv).reshape(b, nq, N_SELF_HEADS, HEAD_DIM)
    return o, p.sum(axis=2)


def nsa_select(imp, q_pos):
    nb = imp.shape[-1]
    cur = (q_pos // SLC_BLOCK)[:, None]
    j = jnp.arange(nb)[None, :]
    forced = (j == 0) | (j == cur) | (j == cur - 1)
    score = jnp.where(j <= cur, jnp.where(forced, FORCE_SCORE, imp), -1.0)
    _, idx = lax.top_k(score, min(N_SEL, nb))
    return idx, idx <= cur[None, None]


def nsa_slc_core(q, q_pos, kb, vb, idx, valid, rel_bias):
    nq = q.shape[0]
    qg = q.reshape(nq, NSA_KV_HEADS, NSA_GROUP, HEAD_DIM)
    s = jnp.einsum('qgrd,gqnld->grqnl', qg, kb).astype(jnp.float32) * ATTN_SCALE
    kpos = idx[..., None] * SLC_BLOCK + jnp.arange(SLC_BLOCK)
    dist = q_pos[None, :, None, None] - kpos
    mask = (valid[..., None] & (dist >= 0))[:, None]
    tab = rel_bias.reshape(N_BUCKETS, NSA_KV_HEADS, NSA_GROUP).transpose(1, 0, 2)
    bias = tab[jnp.arange(NSA_KV_HEADS)[:, None, None, None], t5_bucket(dist)]
    s = jnp.where(mask, s + bias.transpose(0, 4, 1, 2, 3).astype(jnp.float32), NEG_INF)
    p = jnp.where(mask, jax.nn.softmax(s, axis=(-2, -1)), 0.0)
    o = jnp.einsum('grqnl,gqnld->qgrd', p.astype(vb.dtype), vb)
    return o.reshape(nq, N_SELF_HEADS, HEAD_DIM)


def nsa_slc_prompt(q, ks, vs, idx, valid, rel_bias):
    b, s_len = q.shape[:2]
    nb = s_len // SLC_BLOCK
    n = idx.shape[-1]
    kbl = ks.reshape(b, nb, SLC_BLOCK, NSA_KV_HEADS, HEAD_DIM)
    vbl = vs.reshape(b, nb, SLC_BLOCK, NSA_KV_HEADS, HEAD_DIM)
    nch = s_len // SLC_Q_BLOCK
    bi = jnp.arange(b)[:, None, None, None, None]
    gi = jnp.arange(NSA_KV_HEADS)[None, :, None, None, None]
    li = jnp.arange(SLC_BLOCK)
    core = jax.vmap(nsa_slc_core, in_axes=(0, None, 0, 0, 0, 0, None))

    def chunk(args):
        qc, qp, ic, vc = args
        ji = ic[..., None]
        return core(qc, qp, kbl[bi, ji, li, gi], vbl[bi, ji, li, gi], ic, vc, rel_bias)

    xs = (q.reshape(b, nch, SLC_Q_BLOCK, N_SELF_HEADS, HEAD_DIM).swapaxes(0, 1),
          jnp.arange(s_len).reshape(nch, SLC_Q_BLOCK),
          idx.reshape(b, NSA_KV_HEADS, nch, SLC_Q_BLOCK, n).transpose(2, 0, 1, 3, 4),
          valid.reshape(b, NSA_KV_HEADS, nch, SLC_Q_BLOCK, n).transpose(2, 0, 1, 3, 4))
    out = lax.map(chunk, xs)
    return out.swapaxes(0, 1).reshape(b, s_len, N_SELF_HEADS, HEAD_DIM)


def nsa_win_prompt(q, kw, vw, rel_bias):
    b, s_len = q.shape[:2]
    nqb = s_len // Q_BLOCK
    band = WINDOW + Q_BLOCK
    pad = ((0, 0), (WINDOW, 0), (0, 0), (0, 0))
    band_idx = jnp.arange(nqb)[:, None] * Q_BLOCK + jnp.arange(band)[None, :]
    kb = jnp.pad(kw, pad)[:, band_idx]
    vb = jnp.pad(vw, pad)[:, band_idx]
    qb = q.reshape(b, nqb, Q_BLOCK, NSA_KV_HEADS, NSA_GROUP, HEAD_DIM)
    qpos = jnp.arange(s_len).reshape(nqb, Q_BLOCK)

    def blk(qi, ki, vi, qp, kp):
        dist = qp[:, None] - kp[None, :]
        mask = (dist >= 0) & (dist < WINDOW) & (kp >= 0)[None, :]
        return attend(qi, ki, vi, t5_bias(rel_bias, dist), mask)

    o = jax.vmap(blk, in_axes=(1, 1, 1, 0, 0), out_axes=1)(qb, kb, vb, qpos, band_idx - WINDOW)
    return o.reshape(b, s_len, N_SELF_HEADS, HEAD_DIM)


def nsa_win_sample(q, q_pos, kw, vw, buf_k, buf_v, rel_bias):
    b, nq = q.shape[:2]
    wb = buf_k.shape[1]
    keys = jnp.concatenate([buf_k, kw], axis=1)
    vals = jnp.concatenate([buf_v, vw], axis=1)
    kp = jnp.concatenate([PAST_LEN - wb + jnp.arange(wb), q_pos])
    dist = q_pos[:, None] - kp[None, :]
    mask = (dist >= 0) & (dist < WINDOW)
    qg = q.reshape(b, nq, NSA_KV_HEADS, NSA_GROUP, HEAD_DIM)
    o = attend(qg, keys, vals, t5_bias(rel_bias, dist), mask).reshape(b, nq, N_SELF_HEADS, HEAD_DIM)
    keep = min(WINDOW, wb + nq)
    return o, keys[:, -keep:], vals[:, -keep:]


def nsa_gate(gl, b_gate, o_c, o_s, o_w):
    b, t = gl.shape[:2]
    g = jax.nn.sigmoid((gl + b_gate).astype(jnp.float32)).reshape(b, t, N_SELF_HEADS, 3).astype(o_c.dtype)
    return g[..., 0:1] * o_c + g[..., 1:2] * o_s + g[..., 2:3] * o_w


def nsa_prompt_mixer(proj, b_gate, w_cmp1, w_cmp2, cmp_pe, rel_bias):
    s_len = proj.shape[1]
    q, (kc, vc, ks, vs, kw, vw), gl, qx = _split_nsa(proj)
    q_pos = jnp.arange(s_len)
    ck = nsa_compress(kc, w_cmp1[0], w_cmp2[0], cmp_pe[0])
    cv = nsa_compress(vc, w_cmp1[1], w_cmp2[1], cmp_pe[1])
    o_c, imp = nsa_cmp_branch(q, q_pos, ck, cv, rel_bias)
    idx, valid = nsa_select(imp, q_pos)
    o_s = nsa_slc_prompt(q, ks, vs, idx, valid, rel_bias)
    o_w = nsa_win_prompt(q, kw, vw, rel_bias)
    keep = min(WINDOW, s_len)
    return nsa_gate(gl, b_gate, o_c, o_s, o_w), qx, (kc, vc, ks, vs, kw[:, -keep:], vw[:, -keep:])


def nsa_sample_mixer(proj, q_pos, pool_ck, pool_cv, pool_sk, pool_sv, buf_k, buf_v, page_table,
                     b_gate, w_cmp1, w_cmp2, cmp_pe, rel_bias):
    q, (kc, vc, ks, vs, kw, vw), gl, qx = _split_nsa(proj)
    nq = q.shape[1]
    t_len = PAST_LEN + nq
    nb = -(-t_len // CMP_BLOCK)
    n_past_blk = PAST_LEN // SLC_BLOCK
    bpp = PAGE_SIZE // SLC_BLOCK
    n_new_blk = -(-nq // SLC_BLOCK)
    blk_sk = pool_sk.reshape(-1, SLC_BLOCK, NSA_KV_HEADS, HEAD_DIM)
    blk_sv = pool_sv.reshape(-1, SLC_BLOCK, NSA_KV_HEADS, HEAD_DIM)
    gi = jnp.arange(NSA_KV_HEADS)[:, None, None, None]
    li = jnp.arange(SLC_BLOCK)
    pad_new = ((0, n_new_blk * SLC_BLOCK - nq), (0, 0), (0, 0))

    def one(args):
        q1, kc1, vc1, ks1, vs1, pt1 = args

        def rows(pool, new):
            past = pool[pt1].reshape(PAST_LEN, NSA_KV_HEADS, HEAD_DIM)
            full = jnp.concatenate([past, new], axis=0)
            return jnp.pad(full, ((0, nb * CMP_BLOCK - t_len), (0, 0), (0, 0)))[None]

        ck = nsa_compress(rows(pool_ck, kc1), w_cmp1[0], w_cmp2[0], cmp_pe[0])
        cv = nsa_compress(rows(pool_cv, vc1), w_cmp1[1], w_cmp2[1], cmp_pe[1])
        o_c, imp = nsa_cmp_branch(q1[None], q_pos, ck, cv, rel_bias)
        idx, valid = nsa_select(imp, q_pos)
        idx, valid = idx[0], valid[0]
        jp = jnp.minimum(idx, n_past_blk - 1)
        phys = (pt1[jp // bpp] * bpp + jp % bpp)[..., None]
        jn = jnp.clip(idx - n_past_blk, 0, n_new_blk - 1)[..., None]
        is_past = (idx < n_past_blk)[..., None, None]
        new_k = jnp.pad(ks1, pad_new).reshape(n_new_blk, SLC_BLOCK, NSA_KV_HEADS, HEAD_DIM)
        new_v = jnp.pad(vs1, pad_new).reshape(n_new_blk, SLC_BLOCK, NSA_KV_HEADS, HEAD_DIM)
        kb = jnp.where(is_past, blk_sk[phys, li, gi], new_k[jn, li, gi])
        vb = jnp.where(is_past, blk_sv[phys, li, gi], new_v[jn, li, gi])
        return o_c[0], nsa_slc_core(q1, q_pos, kb, vb, idx, valid, rel_bias)

    o_c, o_s = lax.map(one, (q, kc, vc, ks, vs, page_table))
    o_w, nbk, nbv = nsa_win_sample(q, q_pos, kw, vw, buf_k, buf_v, rel_bias)
    return nsa_gate(gl, b_gate, o_c, o_s, o_w), qx, (kc, vc, ks, vs, nbk, nbv)


def _split_fox(proj):
    b, t, _ = proj.shape
    q, k, v, fl, qx = _split(proj, FOX_SIZES)
    hd = (b, t, FOX_HEADS, HEAD_DIM)
    return q.reshape(hd), k.reshape(hd), v.reshape(hd), fl, qx.reshape(b, t, N_CROSS_HEADS, HEAD_DIM)


def fox_bias(cq, ck):
    return cq.transpose(0, 2, 1)[:, :, None, :, None] - ck.transpose(0, 2, 1)[:, :, None, None, :]


def fox_prompt_mixer(proj, b_forget):
    q, k, v, fl, qx = _split_fox(proj)
    logf = jax.nn.log_sigmoid((fl + b_forget).astype(jnp.float32))
    b, s_len = q.shape[:2]
    c = jnp.cumsum(logf, axis=1)
    nqb = s_len // Q_BLOCK
    kpos = jnp.arange(s_len)

    def blk(args):
        qb, cq, qp = args
        mask = kpos[None, :] <= qp[:, None]
        return attend(qb[:, :, :, None], k, v, fox_bias(cq, c), mask)[:, :, :, 0]

    xs = (q.reshape(b, nqb, Q_BLOCK, FOX_HEADS, HEAD_DIM).swapaxes(0, 1),
          c.reshape(b, nqb, Q_BLOCK, FOX_HEADS).swapaxes(0, 1),
          kpos.reshape(nqb, Q_BLOCK))
    o = lax.map(blk, xs).swapaxes(0, 1).reshape(b, s_len, FOX_HEADS, HEAD_DIM)
    return o, qx, (k, v, logf)


def fox_sample_mixer(proj, q_pos, pool_k, pool_v, pool_logf, page_table, b_forget):
    q, k, v, fl, qx = _split_fox(proj)
    logf = jax.nn.log_sigmoid((fl + b_forget).astype(jnp.float32))
    nq = q.shape[1]
    kpos = jnp.arange(PAST_LEN + nq)
    mask = kpos[None, :] <= q_pos[:, None]

    def one(args):
        q1, k1, v1, lf1, pt1 = args
        kk = jnp.concatenate([pool_k[pt1].reshape(PAST_LEN, FOX_HEADS, HEAD_DIM), k1], axis=0)[None]
        vv = jnp.concatenate([pool_v[pt1].reshape(PAST_LEN, FOX_HEADS, HEAD_DIM), v1], axis=0)[None]
        lf = jnp.concatenate([pool_logf[pt1].reshape(PAST_LEN, FOX_HEADS).astype(jnp.float32), lf1], axis=0)[None]
        c = jnp.cumsum(lf, axis=1)
        return attend(q1[None, :, :, None], kk, vv, fox_bias(c[:, PAST_LEN:], c), mask)[0, :, :, 0]

    o = lax.map(one, (q, k, v, logf, page_table))
    return o, qx, (k, v, logf)


def _pad_cols(w, mult):
    n = w.shape[1]
    return jnp.pad(w, ((0, 0), (0, (-n) % mult)))


def kernel(x_prompt, x_sample, mem_prompt, cache_nsa_cmp_k, cache_nsa_cmp_v, cache_nsa_slc_k, cache_nsa_slc_v, cache_nsa_win_k, cache_nsa_win_v, cache_fox_k, cache_fox_v, cache_fox_logf, cache_mem_k, cache_mem_v, page_table, rel_bias, norm_g, mem_norm_g, w_mem_kv, w_ff_gu, w_ff_down, w_in_nsa, b_gate_nsa, w_cmp1, w_cmp2, cmp_pe, w_out_nsa, w_in_fox, b_forget, w_out_fox, final_norm_g):
    n_p = BATCH * SEQ
    x = jnp.concatenate([x_prompt.reshape(n_p, D_MODEL), x_sample.reshape(-1, D_MODEL)], axis=0)
    mem = mem_prompt.reshape(BATCH * MEM_LEN, D_MODEL)
    q_pos_s = PAST_LEN + jnp.arange(DEC_SEQ)
    nsa_p, nsa_s, fox_p, fox_s, mem_k_p, mem_v_p = [], [], [], [], [], []
    for i in range(DEPTH):
        x = ffn(x, norm_g[i, 0], w_ff_gu[i, 0], w_ff_down[i, 0])
        kv = linear(mem, w_mem_kv[i], g=mem_norm_g[i], tm=512).reshape(BATCH, MEM_LEN, 2, N_CROSS_HEADS, HEAD_DIM)
        mk_p, mv_p = kv[:, :, 0], kv[:, :, 1]
        mem_k_p.append(mk_p)
        mem_v_p.append(mv_p)
        a = i // 2
        if i % 2 == 0:
            n_in = sum(NSA_SIZES)
            proj = linear(x, _pad_cols(w_in_nsa[a], 512), g=norm_g[i, 1])[:, :n_in]
            pp = proj[:n_p].reshape(BATCH, SEQ, n_in)
            ps = proj[n_p:].reshape(DEC_BATCH, DEC_SEQ, n_in)
            o_p, qx_p, st_p = nsa_prompt_mixer(pp, b_gate_nsa[a], w_cmp1[a], w_cmp2[a], cmp_pe[a], rel_bias)
            o_s, qx_s, st_s = nsa_sample_mixer(ps, q_pos_s, cache_nsa_cmp_k[a], cache_nsa_cmp_v[a],
                                               cache_nsa_slc_k[a], cache_nsa_slc_v[a], cache_nsa_win_k[a],
                                               cache_nsa_win_v[a], page_table, b_gate_nsa[a],
                                               w_cmp1[a], w_cmp2[a], cmp_pe[a], rel_bias)
            nsa_p.append(st_p)
            nsa_s.append(st_s)
            w_out = w_out_nsa[a]
        else:
            n_in = sum(FOX_SIZES)
            proj = linear(x, _pad_cols(w_in_fox[a], 512), g=norm_g[i, 1])[:, :n_in]
            pp = proj[:n_p].reshape(BATCH, SEQ, n_in)
            ps = proj[n_p:].reshape(DEC_BATCH, DEC_SEQ, n_in)
            o_p, qx_p, st_p = fox_prompt_mixer(pp, b_forget[a])
            o_s, qx_s, st_s = fox_sample_mixer(ps, q_pos_s, cache_fox_k[a], cache_fox_v[a],
                                               cache_fox_logf[a], page_table, b_forget[a])
            fox_p.append(st_p)
            fox_s.append(st_s)
            w_out = w_out_fox[a]
        mix_p = cross_attend(o_p, qx_p, mk_p, mv_p).reshape(n_p, MIX_W)
        mix_s = cross_attend(o_s, qx_s, cache_mem_k[i], cache_mem_v[i]).reshape(-1, MIX_W)
        x = linear(jnp.concatenate([mix_p, mix_s], axis=0), w_out, res=x)
        x = ffn(x, norm_g[i, 2], w_ff_gu[i, 1], w_ff_down[i, 1],
                final_g=final_norm_g if i == DEPTH - 1 else None)
    y_prompt = x[:n_p].reshape(BATCH, SEQ, D_MODEL)
    y_sample = x[n_p:].reshape(DEC_BATCH, DEC_SEQ, D_MODEL)

    def stack(states, j):
        return jnp.stack([st[j] for st in states])

    p_cmp_k, p_cmp_v, p_slc_k = stack(nsa_p, 0), stack(nsa_p, 1), stack(nsa_p, 2)
    p_slc_v, p_win_k, p_win_v = stack(nsa_p, 3), stack(nsa_p, 4), stack(nsa_p, 5)
    s_cmp_k, s_cmp_v, s_slc_k = stack(nsa_s, 0), stack(nsa_s, 1), stack(nsa_s, 2)
    s_slc_v, s_win_k, s_win_v = stack(nsa_s, 3), stack(nsa_s, 4), stack(nsa_s, 5)
    p_fox_k, p_fox_v, p_fox_logf = stack(fox_p, 0), stack(fox_p, 1), stack(fox_p, 2)
    s_fox_k, s_fox_v, s_fox_logf = stack(fox_s, 0), stack(fox_s, 1), stack(fox_s, 2)
    p_mem_k = jnp.stack(mem_k_p)
    p_mem_v = jnp.stack(mem_v_p)
    return (y_prompt, y_sample,
            p_cmp_k, p_cmp_v, p_slc_k, p_slc_v, p_win_k, p_win_v,
            p_fox_k, p_fox_v, p_fox_logf, p_mem_k, p_mem_v,
            s_cmp_k, s_cmp_v, s_slc_k, s_slc_v, s_win_k, s_win_v,
            s_fox_k, s_fox_v, s_fox_logf)
```

```python
import functools
import math

import jax
import jax.numpy as jnp
from jax import lax
from jax.experimental import pallas as pl
from jax.experimental.pallas import tpu as pltpu

D_MODEL = 2048
BATCH = 4
SEQ = 2048
DEPTH = 2
DEC_BATCH = 128
DEC_SEQ = 4
PAST_LEN = 2048
PAGE_SIZE = 128
HEAD_DIM = 128
N_CROSS_HEADS = 4
N_SELF_HEADS = 12
NSA_KV_HEADS = 4
NSA_GROUP = N_SELF_HEADS // NSA_KV_HEADS
CMP_BLOCK = 64
SLC_BLOCK = 64
N_SEL = 16
WINDOW = 512
CMP_HIDDEN = 256
FOX_HEADS = N_SELF_HEADS
MEM_LEN = 256
N_BUCKETS = 32
MAX_DISTANCE = 128
FFN_DIM = ((8 * D_MODEL) // 3 + 127) // 128 * 128
RMS_EPS = 1e-6
FORCE_SCORE = 1e4
NEG_INF = -1e30
ATTN_SCALE = HEAD_DIM ** -0.5
SELF_W = N_SELF_HEADS * HEAD_DIM
NSA_KV_W = NSA_KV_HEADS * HEAD_DIM
CROSS_W = N_CROSS_HEADS * HEAD_DIM
N_PAGES = PAST_LEN // PAGE_SIZE
N_PAST_BLK = PAST_LEN // SLC_BLOCK

LANE = 128
VMEM_LIMIT = 56 * 1024 * 1024
N_TOK = BATCH * SEQ + DEC_BATCH * DEC_SEQ
TOK_TILE = 544
Q_TILE = 128
COL_QX = 3 * SELF_W
COL_GATE = COL_QX + CROSS_W
PROJ_W = COL_GATE + CROSS_W

F32 = jnp.float32
BF16 = jnp.bfloat16
_NT = (((1,), (1,)), ((), ()))


def _cparams(sem):
    return pltpu.CompilerParams(dimension_semantics=sem, vmem_limit_bytes=VMEM_LIMIT)


def _rms(x, g):
    return x * lax.rsqrt(jnp.mean(x * x, axis=-1, keepdims=True) + RMS_EPS) * g


def _dot(a, b):
    return jnp.dot(a, b, preferred_element_type=F32)


def _dot_nt(a, b):
    return lax.dot_general(a, b, _NT, preferred_element_type=F32)


def _t5_bias(dist, rb_ref, h):
    n = jnp.maximum(dist, 0)
    max_exact = N_BUCKETS // 2
    nf = jnp.maximum(n, 1).astype(F32)
    large = max_exact + (jnp.log(nf / max_exact) / math.log(MAX_DISTANCE / max_exact)
                         * (N_BUCKETS - max_exact)).astype(jnp.int32)
    bucket = jnp.where(n <= max_exact, n, jnp.minimum(large, N_BUCKETS - 1))
    out = jnp.zeros(dist.shape, F32)
    for b in range(N_BUCKETS):
        out = jnp.where(bucket == b, rb_ref[b, h], out)
    return out


def _softmax_rows(s, mask):
    s = jnp.where(mask, s, NEG_INF)
    e = jnp.where(mask, jnp.exp(s - jnp.max(s, axis=-1, keepdims=True)), 0.0)
    den = jnp.sum(e, axis=-1, keepdims=True)
    return e / jnp.where(den > 0.0, den, 1.0)


def _select_blocks(imp, cur, col, n_blocks):
    forced = (col == 0) | (col == cur) | (col == cur - 1)
    score = jnp.where(col <= cur, jnp.where(forced, FORCE_SCORE, imp), -1.0)
    rank = jnp.zeros(score.shape, jnp.int32)
    for i in range(n_blocks):
        ci = score[:, i:i + 1]
        beats = (ci > score) | ((ci == score) & (col > i))
        rank = rank + beats.astype(jnp.int32)
    return (rank < N_SEL) & (col <= cur)


class _Flash:
    def __init__(self, m_ref, l_ref, acc_ref):
        self.m, self.l, self.acc = m_ref, l_ref, acc_ref

    def reset(self):
        self.m[...] = jnp.full(self.m.shape, NEG_INF, F32)
        self.l[...] = jnp.zeros(self.l.shape, F32)
        self.acc[...] = jnp.zeros(self.acc.shape, F32)

    def update(self, s, v, mask=None):
        if mask is not None:
            s = jnp.where(mask, s, NEG_INF)
        m_old = self.m[...]
        m_new = jnp.maximum(m_old, jnp.max(s, axis=-1, keepdims=True))
        alpha = jnp.exp(m_old - m_new)
        p = jnp.exp(s - m_new)
        if mask is not None:
            p = jnp.where(mask, p, 0.0)
        self.l[...] = alpha * self.l[...] + jnp.sum(p, axis=-1, keepdims=True)
        self.acc[...] = alpha * self.acc[...] + _dot(p.astype(BF16), v)
        self.m[...] = m_new

    def result(self):
        return self.acc[...] / self.l[...]


def _ffn_kernel(x_ref, g_ref, fg_ref, wg0, wg1, wu0, wu1, wd0, wd1, o_ref, h_ref, *, n_steps, final_norm):
    f = pl.program_id(1)

    @pl.when(f == 0)
    def _():
        x = x_ref[...]
        h_ref[...] = _rms(x, g_ref[...]).astype(BF16)
        o_ref[...] = x

    h = h_ref[...]
    w_gate = jnp.concatenate([wg0[...].astype(BF16), wg1[...].astype(BF16)], axis=1)
    w_up = jnp.concatenate([wu0[...].astype(BF16), wu1[...].astype(BF16)], axis=1)
    a = _dot(h, w_gate)
    b = _dot(h, w_up)
    act = a * jax.nn.sigmoid(a) * b
    lane = lax.broadcasted_iota(jnp.int32, act.shape, 1)
    act = jnp.where(jnp.logical_and(f == n_steps - 1, lane >= LANE), 0.0, act)
    w_down = jnp.concatenate([wd0[...].astype(BF16), wd1[...].astype(BF16)], axis=0)
    o_ref[...] += 0.5 * _dot(act.astype(BF16), w_down)

    if final_norm:
        @pl.when(f == n_steps - 1)
        def _():
            o_ref[...] = _rms(o_ref[...], fg_ref[...])


def ffn(x, g, w_gu, w_down, final_g=None):
    m, d = x.shape
    n_f = FFN_DIM // LANE
    n_steps = (n_f + 1) // 2
    tm = TOK_TILE
    final_norm = final_g is not None
    fg = final_g if final_norm else g

    def col(off, k):
        return lambda i, f: (0, off + jnp.minimum(2 * f + k, n_f - 1))

    def row(k):
        return lambda i, f: (jnp.minimum(2 * f + k, n_f - 1), 0)

    wspec = lambda im: pl.BlockSpec((d, LANE), im)
    return pl.pallas_call(
        functools.partial(_ffn_kernel, n_steps=n_steps, final_norm=final_norm),
        grid=(m // tm, n_steps),
        in_specs=[
            pl.BlockSpec((tm, d), lambda i, f: (i, 0)),
            pl.BlockSpec((1, d), lambda i, f: (0, 0)),
            pl.BlockSpec((1, d), lambda i, f: (0, 0)),
            wspec(col(0, 0)), wspec(col(0, 1)), wspec(col(n_f, 0)), wspec(col(n_f, 1)),
            pl.BlockSpec((LANE, d), row(0)), pl.BlockSpec((LANE, d), row(1)),
        ],
        out_specs=pl.BlockSpec((tm, d), lambda i, f: (i, 0)),
        out_shape=jax.ShapeDtypeStruct((m, d), F32),
        scratch_shapes=[pltpu.VMEM((tm, d), BF16)],
        compiler_params=_cparams(("parallel", "arbitrary")),
        name="ffn",
    )(x, g.reshape(1, d), fg.reshape(1, d), w_gu, w_gu, w_gu, w_gu, w_down, w_down)


def _linear_kernel(*refs, n_x, norm, n_main, residual):
    x_refs = refs[:n_x]
    refs = refs[n_x:]
    if norm:
        g_ref, refs = refs[0], refs[1:]
    w_ref, wt_ref = refs[:2]
    refs = refs[2:]
    if residual:
        r_ref, o_ref, h_ref = refs
    else:
        o_ref, h_ref = refs
    j = pl.program_id(1)

    @pl.when(j == 0)
    def _():
        off = 0
        for x_ref in x_refs:
            x = x_ref[...].astype(F32)
            if norm:
                x = _rms(x, g_ref[...])
            h_ref[:, off:off + x.shape[1]] = x.astype(BF16)
            off += x.shape[1]

    def emit(w):
        y = _dot(h_ref[...], w[...].astype(BF16))
        if residual:
            y = y + r_ref[...]
        o_ref[...] = y

    pl.when(j < n_main)(lambda: emit(w_ref))
    pl.when(j >= n_main)(lambda: emit(wt_ref))


def linear(xs, w, w_tail=None, n_main=None, g=None, res=None, tm=TOK_TILE, tn=512):
    xs = list(xs) if isinstance(xs, (list, tuple)) else [xs]
    m = xs[0].shape[0]
    k = sum(x.shape[1] for x in xs)
    if w_tail is None:
        w_tail, n_main = w, w.shape[1] // tn
    n_blocks = n_main + (w_tail.shape[1] // tn if w_tail is not w else 0)
    norm, residual = g is not None, res is not None
    in_specs = [pl.BlockSpec((tm, x.shape[1]), lambda i, j: (i, 0)) for x in xs]
    args = list(xs)
    if norm:
        in_specs.append(pl.BlockSpec((1, k), lambda i, j: (0, 0)))
        args.append(g.reshape(1, k))
    in_specs.append(pl.BlockSpec((k, tn), lambda i, j: (0, jnp.minimum(j, n_main - 1))))
    in_specs.append(pl.BlockSpec((k, tn), lambda i, j: (0, jnp.maximum(j - n_main, 0))))
    args += [w, w_tail]
    if residual:
        in_specs.append(pl.BlockSpec((tm, tn), lambda i, j: (i, j)))
        args.append(res)
    return pl.pallas_call(
        functools.partial(_linear_kernel, n_x=len(xs), norm=norm, n_main=n_main, residual=residual),
        grid=(m // tm, n_blocks),
        in_specs=in_specs,
        out_specs=pl.BlockSpec((tm, tn), lambda i, j: (i, j)),
        out_shape=jax.ShapeDtypeStruct((m, n_blocks * tn), F32),
        scratch_shapes=[pltpu.VMEM((tm, k), BF16)],
        compiler_params=_cparams(("parallel", "arbitrary")),
        name="linear",
    )(*args)


def _compress_kernel(x_ref, pe_ref, w1_ref, w2_ref, o_ref, acc_ref, *, n_valid):
    l = pl.program_id(1)

    @pl.when(l == 0)
    def _():
        acc_ref[...] = jnp.zeros(acc_ref.shape, F32)

    pe = pe_ref[pl.ds(l, 1), :]
    w1 = w1_ref[...].astype(BF16)
    x = x_ref[...]
    if n_valid < CMP_BLOCK:
        x = jnp.where(l < n_valid, x, 0.0)
    for g in range(NSA_KV_HEADS):
        xg = (x[:, g * HEAD_DIM:(g + 1) * HEAD_DIM] + pe).astype(BF16)
        acc_ref[g] += _dot(xg, w1)

    @pl.when(l == CMP_BLOCK - 1)
    def _():
        w2 = w2_ref[...].astype(BF16)
        for g in range(NSA_KV_HEADS):
            a = acc_ref[g]
            hid = a * jax.nn.sigmoid(a)
            o_ref[:, g * HEAD_DIM:(g + 1) * HEAD_DIM] = _dot(hid.astype(BF16), w2)


def compress(x2d, col_block, n_valid, w1, w2, pe, tr, rows=None, row_block0=0):
    rows = x2d.shape[0] if rows is None else rows
    return pl.pallas_call(
        functools.partial(_compress_kernel, n_valid=n_valid),
        grid=(rows // tr, CMP_BLOCK),
        in_specs=[
            pl.BlockSpec((tr, NSA_KV_W),
                         lambda i, l: (row_block0 + i, col_block(jnp.minimum(l, n_valid - 1)))),
            pl.BlockSpec((CMP_BLOCK, HEAD_DIM), lambda i, l: (0, 0)),
            pl.BlockSpec((HEAD_DIM, CMP_HIDDEN), lambda i, l: (l, 0)),
            pl.BlockSpec((CMP_HIDDEN, HEAD_DIM), lambda i, l: (0, 0)),
        ],
        out_specs=pl.BlockSpec((tr, NSA_KV_W), lambda i, l: (i, 0)),
        out_shape=jax.ShapeDtypeStruct((rows, NSA_KV_W), F32),
        scratch_shapes=[pltpu.VMEM((NSA_KV_HEADS, tr, CMP_HIDDEN), F32)],
        compiler_params=_cparams(("parallel", "arbitrary")),
        name="compress",
    )(x2d, pe, w1, w2)


def _nsa_prompt_kernel(rb_ref, q_ref, ks_ref, vs_ref, kw_ref, vw_ref, ck_ref, cv_ref, gl_ref, bg_ref,
                       qx_ref, mk_ref, mv_ref, o_ref, ox_ref, bt_ref, selx_ref, m_ref, l_ref, acc_ref):
    g = pl.program_id(1)
    qt = pl.program_id(2)
    q0 = qt * Q_TILE
    nr = NSA_GROUP * Q_TILE
    ii = lax.broadcasted_iota(jnp.int32, (Q_TILE, LANE), 0)
    jj = lax.broadcasted_iota(jnp.int32, (Q_TILE, LANE), 1)

    @pl.when(qt == 0)
    def _():
        for r in range(NSA_GROUP):
            bt_ref[r, 0] = _t5_bias(ii - jj, rb_ref, NSA_GROUP * g + r)
            bt_ref[r, 1] = _t5_bias(LANE + ii - jj, rb_ref, NSA_GROUP * g + r)

    def heads(f):
        return jnp.concatenate([f(r) for r in range(NSA_GROUP)], axis=0)

    qs = heads(lambda r: q_ref[:, r * HEAD_DIM:(r + 1) * HEAD_DIM]).astype(BF16)
    far_bias = heads(lambda r: jnp.full((Q_TILE, 1), rb_ref[N_BUCKETS - 1, NSA_GROUP * g + r], F32))
    diag_bias = heads(lambda r: bt_ref[r, 0])
    near_bias = heads(lambda r: bt_ref[r, 1])
    causal = jnp.concatenate([jj <= ii] * NSA_GROUP, axis=0)

    nb = ck_ref.shape[0]
    row = lax.broadcasted_iota(jnp.int32, (Q_TILE, nb), 0)
    col = lax.broadcasted_iota(jnp.int32, (Q_TILE, nb), 1)
    qpos = q0 + row
    dist = qpos - (col * CMP_BLOCK + (CMP_BLOCK - 1))
    sc = _dot_nt(qs, ck_ref[...].astype(BF16)) * ATTN_SCALE
    probs = []
    for r in range(NSA_GROUP):
        s_r = sc[r * Q_TILE:(r + 1) * Q_TILE] + _t5_bias(dist, rb_ref, NSA_GROUP * g + r)
        probs.append(_softmax_rows(s_r, dist >= 0))
    o_c = _dot(jnp.concatenate(probs, axis=0).astype(BF16), cv_ref[...].astype(BF16))
    imp = probs[0] + probs[1] + probs[2]
    sel = _select_blocks(imp, qpos // SLC_BLOCK, col, nb).astype(BF16)
    for t in range(selx_ref.shape[0]):
        kk = lax.broadcasted_iota(jnp.int32, (nb, LANE), 1) + t * LANE
        bj = lax.broadcasted_iota(jnp.int32, (nb, LANE), 0)
        expand = jnp.where(kk // SLC_BLOCK == bj, 1.0, 0.0).astype(BF16)
        selx_ref[t] = _dot(sel, expand)

    fl = _Flash(m_ref, l_ref, acc_ref)

    def tile(k_ref, v_ref, k0, bias, mask):
        k = k_ref[pl.ds(k0, LANE), :].astype(BF16)
        v = v_ref[pl.ds(k0, LANE), :].astype(BF16)
        fl.update(_dot_nt(qs, k) * ATTN_SCALE + bias, v, mask)

    def sel_mask(t):
        return jnp.concatenate([selx_ref[t] > 0.5] * NSA_GROUP, axis=0)

    fl.reset()

    def far(t, carry):
        tile(ks_ref, vs_ref, pl.multiple_of(t * LANE, LANE), far_bias, sel_mask(t))
        return carry

    lax.fori_loop(0, jnp.maximum(qt - 1, 0), far, 0)

    @pl.when(qt >= 1)
    def _():
        tile(ks_ref, vs_ref, pl.multiple_of(q0 - LANE, LANE), near_bias, sel_mask(qt - 1))

    tile(ks_ref, vs_ref, pl.multiple_of(q0, LANE), diag_bias, sel_mask(qt) & causal)
    o_s = fl.result()

    fl.reset()
    n_win = WINDOW // LANE

    @pl.when(qt >= n_win)
    def _():
        upper = jnp.concatenate([jj > ii] * NSA_GROUP, axis=0)
        tile(kw_ref, vw_ref, pl.multiple_of(q0 - WINDOW, LANE), far_bias, upper)

    for back in range(n_win - 1, 1, -1):
        @pl.when(qt >= back)
        def _(back=back):
            tile(kw_ref, vw_ref, pl.multiple_of(q0 - back * LANE, LANE), far_bias, None)

    @pl.when(qt >= 1)
    def _():
        tile(kw_ref, vw_ref, pl.multiple_of(q0 - LANE, LANE), near_bias, None)

    tile(kw_ref, vw_ref, pl.multiple_of(q0, LANE), diag_bias, causal)
    o_w = fl.result()

    gate = jax.nn.sigmoid(gl_ref[...] + bg_ref[...])
    for r in range(NSA_GROUP):
        rows = slice(r * Q_TILE, (r + 1) * Q_TILE)
        o = (gate[:, 3 * r:3 * r + 1] * o_c[rows] + gate[:, 3 * r + 1:3 * r + 2] * o_s[rows]
             + gate[:, 3 * r + 2:3 * r + 3] * o_w[rows])
        o_ref[:, r * HEAD_DIM:(r + 1) * HEAD_DIM] = o.astype(o_ref.dtype)

    sx = _dot_nt(qx_ref[...].astype(BF16), mk_ref[...].astype(BF16)) * ATTN_SCALE
    px = _softmax_rows(sx, sx == sx)
    ox_ref[...] = _dot(px.astype(BF16), mv_ref[...].astype(BF16)).astype(ox_ref.dtype)


def nsa_prompt(proj, ck, cv, kvm, rel_bias, b_gate, batch):
    nq = SEQ // Q_TILE
    nb = SEQ // CMP_BLOCK
    hb = lambda c: c // HEAD_DIM
    qrow = lambda b, g, t: b * nq + t
    grid_spec = pltpu.PrefetchScalarGridSpec(
        num_scalar_prefetch=0,
        grid=(batch, NSA_KV_HEADS, nq),
        in_specs=[
            pl.BlockSpec(memory_space=pltpu.SMEM),
            pl.BlockSpec((Q_TILE, NSA_GROUP * HEAD_DIM), lambda b, g, t: (qrow(b, g, t), g)),
            pl.BlockSpec((SEQ, HEAD_DIM), lambda b, g, t: (b, hb(SELF_W + 2 * NSA_KV_W) + g)),
            pl.BlockSpec((SEQ, HEAD_DIM), lambda b, g, t: (b, hb(SELF_W + 3 * NSA_KV_W) + g)),
            pl.BlockSpec((SEQ, HEAD_DIM), lambda b, g, t: (b, hb(SELF_W + 4 * NSA_KV_W) + g)),
            pl.BlockSpec((SEQ, HEAD_DIM), lambda b, g, t: (b, hb(SELF_W + 5 * NSA_KV_W) + g)),
            pl.BlockSpec((nb, HEAD_DIM), lambda b, g, t: (b, g)),
            pl.BlockSpec((nb, HEAD_DIM), lambda b, g, t: (b, g)),
            pl.BlockSpec((Q_TILE, LANE), lambda b, g, t: (qrow(b, g, t), hb(COL_GATE) + g)),
            pl.BlockSpec((1, LANE), lambda b, g, t: (0, g)),
            pl.BlockSpec((Q_TILE, HEAD_DIM), lambda b, g, t: (qrow(b, g, t), hb(COL_QX) + g)),
            pl.BlockSpec((MEM_LEN, HEAD_DIM), lambda b, g, t: (b, g)),
            pl.BlockSpec((MEM_LEN, HEAD_DIM), lambda b, g, t: (b, N_CROSS_HEADS + g)),
        ],
        out_specs=[
            pl.BlockSpec((Q_TILE, NSA_GROUP * HEAD_DIM), lambda b, g, t: (qrow(b, g, t), g)),
            pl.BlockSpec((Q_TILE, HEAD_DIM), lambda b, g, t: (qrow(b, g, t), g)),
        ],
        scratch_shapes=[
            pltpu.VMEM((NSA_GROUP, 2, Q_TILE, LANE), F32),
            pltpu.VMEM((SEQ // LANE, Q_TILE, LANE), F32),
            pltpu.VMEM((NSA_GROUP * Q_TILE, 1), F32),
            pltpu.VMEM((NSA_GROUP * Q_TILE, 1), F32),
            pltpu.VMEM((NSA_GROUP * Q_TILE, HEAD_DIM), F32),
        ],
    )
    return pl.pallas_call(
        _nsa_prompt_kernel,
        grid_spec=grid_spec,
        out_shape=[jax.ShapeDtypeStruct((batch * SEQ, SELF_W), BF16),
                   jax.ShapeDtypeStruct((batch * SEQ, CROSS_W), BF16)],
        compiler_params=_cparams(("arbitrary", "arbitrary", "arbitrary")),
        name="nsa_prompt",
    )(rel_bias, proj, proj, proj, proj, proj, ck, cv, proj, b_gate, proj, kvm, kvm)


FOX_TILE = 256
HEAD_PAD = 16


def _log_sigmoid(x):
    return jnp.minimum(x, 0.0) - jnp.log1p(jnp.exp(-jnp.abs(x)))


def _fox_prep_kernel(fl_ref, bf_ref, lf_ref, ct_ref):
    n_chunks = fl_ref.shape[0] // LANE
    ii = lax.broadcasted_iota(jnp.int32, (LANE, LANE), 0)
    jj = lax.broadcasted_iota(jnp.int32, (LANE, LANE), 1)
    tri = jnp.where(jj <= ii, 1.0, 0.0).astype(F32)
    carry = jnp.zeros((1, LANE), F32)
    per_tile = FOX_TILE // LANE
    for c in range(n_chunks):
        lf = _log_sigmoid(fl_ref[c * LANE:(c + 1) * LANE, :] + bf_ref[...])
        lf_ref[c * LANE:(c + 1) * LANE, :] = lf
        cs = jnp.dot(tri, lf, precision=lax.Precision.HIGHEST, preferred_element_type=F32) + carry
        carry = cs[LANE - 1:LANE, :]
        ct_ref[:, c // per_tile, (c % per_tile) * LANE:(c % per_tile + 1) * LANE] = cs.T[:HEAD_PAD]


def fox_prep(proj, b_forget, batch, seq_len=SEQ, row_block0=0):
    bf = jnp.pad(b_forget.reshape(1, FOX_HEADS), ((0, 0), (0, LANE - FOX_HEADS)))
    n_tiles = max(seq_len // FOX_TILE, 1)
    return pl.pallas_call(
        _fox_prep_kernel,
        grid=(batch,),
        in_specs=[pl.BlockSpec((seq_len, LANE), lambda b: (row_block0 + b, COL_GATE // LANE)),
                  pl.BlockSpec((1, LANE), lambda b: (0, 0))],
        out_specs=[pl.BlockSpec((seq_len, LANE), lambda b: (b, 0)),
                   pl.BlockSpec((None, HEAD_PAD, n_tiles, FOX_TILE), lambda b: (b, 0, 0, 0))],
        out_shape=[jax.ShapeDtypeStruct((batch * seq_len, LANE), F32),
                   jax.ShapeDtypeStruct((batch, HEAD_PAD, n_tiles, FOX_TILE), F32)],
        compiler_params=_cparams(("arbitrary",)),
        name="fox_prep",
    )(proj, bf)


def _fox_prompt_kernel(q_ref, k_ref, v_ref, c_ref, o_ref, m_ref, l_ref, acc_ref):
    qt = pl.program_id(2)
    q = q_ref[...].astype(BF16)
    fl = _Flash(m_ref, l_ref, acc_ref)
    fl.reset()

    def tile(t, mask):
        k0 = pl.multiple_of(t * FOX_TILE, FOX_TILE)
        k = k_ref[pl.ds(k0, FOX_TILE), :].astype(BF16)
        v = v_ref[pl.ds(k0, FOX_TILE), :].astype(BF16)
        fl.update(_dot_nt(q, k) * ATTN_SCALE - c_ref[pl.ds(t, 1), :], v, mask)

    def far(t, carry):
        tile(t, None)
        return carry

    lax.fori_loop(0, qt, far, 0)
    ii = lax.broadcasted_iota(jnp.int32, (FOX_TILE, FOX_TILE), 0)
    jj = lax.broadcasted_iota(jnp.int32, (FOX_TILE, FOX_TILE), 1)
    tile(qt, jj <= ii)
    o_ref[...] = fl.result().astype(o_ref.dtype)


def fox_prompt(proj, ct, batch):
    nq = SEQ // FOX_TILE
    hb = lambda c: c // HEAD_DIM
    return pl.pallas_call(
        _fox_prompt_kernel,
        grid=(batch, FOX_HEADS, nq),
        in_specs=[
            pl.BlockSpec((FOX_TILE, HEAD_DIM), lambda b, h, t: (b * nq + t, h)),
            pl.BlockSpec((SEQ, HEAD_DIM), lambda b, h, t: (b, hb(SELF_W) + h)),
            pl.BlockSpec((SEQ, HEAD_DIM), lambda b, h, t: (b, hb(2 * SELF_W) + h)),
            pl.BlockSpec((None, None, nq, FOX_TILE), lambda b, h, t: (b, h, 0, 0)),
        ],
        out_specs=pl.BlockSpec((FOX_TILE, HEAD_DIM), lambda b, h, t: (b * nq + t, h)),
        out_shape=jax.ShapeDtypeStruct((batch * SEQ, SELF_W), BF16),
        scratch_shapes=[pltpu.VMEM((FOX_TILE, 1), F32), pltpu.VMEM((FOX_TILE, 1), F32),
                        pltpu.VMEM((FOX_TILE, HEAD_DIM), F32)],
        compiler_params=_cparams(("arbitrary", "arbitrary", "arbitrary")),
        name="fox_prompt",
    )(proj, proj, proj, ct)


def _cross_prompt_kernel(q_ref, mk_ref, mv_ref, o_ref):
    for h in range(N_CROSS_HEADS):
        hs = slice(h * HEAD_DIM, (h + 1) * HEAD_DIM)
        s = _dot_nt(q_ref[:, hs].astype(BF16), mk_ref[:, hs].astype(BF16)) * ATTN_SCALE
        p = _softmax_rows(s, s == s)
        o_ref[:, hs] = _dot(p.astype(BF16), mv_ref[:, hs].astype(BF16)).astype(o_ref.dtype)


def cross_prompt(proj, kvm, batch, tq=512):
    nq = SEQ // tq
    return pl.pallas_call(
        _cross_prompt_kernel,
        grid=(batch, nq),
        in_specs=[pl.BlockSpec((tq, CROSS_W), lambda b, t: (b * nq + t, COL_QX // CROSS_W)),
                  pl.BlockSpec((MEM_LEN, CROSS_W), lambda b, t: (b, 0)),
                  pl.BlockSpec((MEM_LEN, CROSS_W), lambda b, t: (b, 1))],
        out_specs=pl.BlockSpec((tq, CROSS_W), lambda b, t: (b * nq + t, 0)),
        out_shape=jax.ShapeDtypeStruct((batch * SEQ, CROSS_W), BF16),
        compiler_params=_cparams(("arbitrary", "arbitrary")),
        name="cross_prompt",
    )(proj, kvm, kvm)


ROWS = 16


def _pad_rows(x):
    return jnp.pad(x, [(0, 0)] * (x.ndim - 2) + [(0, ROWS - x.shape[-2]), (0, 0)])


def _group_rows(q):
    b = q.shape[0]
    q = q.reshape(b, DEC_SEQ, NSA_KV_HEADS, NSA_GROUP, HEAD_DIM).transpose(0, 2, 3, 1, 4)
    return _pad_rows(q.reshape(b, NSA_KV_HEADS, NSA_GROUP * DEC_SEQ, HEAD_DIM))


def _ungroup_rows(o):
    b = o.shape[0]
    o = o[:, :, :NSA_GROUP * DEC_SEQ].reshape(b, NSA_KV_HEADS, NSA_GROUP, DEC_SEQ, HEAD_DIM)
    return o.transpose(0, 3, 1, 2, 4).reshape(b * DEC_SEQ, SELF_W)


def _gate_rows(gl):
    b = gl.shape[0]
    gl = gl.reshape(b, DEC_SEQ, NSA_KV_HEADS, LANE).transpose(0, 2, 1, 3)
    return _pad_rows(jnp.tile(gl, (1, 1, NSA_GROUP, 1)))


def _head_rows(q, n_heads):
    b = q.shape[0]
    return _pad_rows(q.reshape(b, DEC_SEQ, n_heads, HEAD_DIM).transpose(0, 2, 1, 3))


def _unhead_rows(o):
    b, h = o.shape[:2]
    return o[:, :, :DEC_SEQ].transpose(0, 2, 1, 3).reshape(b * DEC_SEQ, h * HEAD_DIM)


def _row_ids():
    row = lax.broadcasted_iota(jnp.int32, (ROWS, LANE), 0)
    return row, row % DEC_SEQ, row // DEC_SEQ


def _per_head(r_idx, f):
    return jnp.where(r_idx == 0, f(0), jnp.where(r_idx == 1, f(1), f(2)))


def _safe_div(acc, l):
    return acc / jnp.where(l > 0.0, l, 1.0)


def _merge_new_keys(fl, q32, kn, vn, bias, masks):
    s_new = [jnp.where(masks[j], jnp.sum(q32 * kn[j:j + 1, :], axis=-1, keepdims=True) * ATTN_SCALE
                       + bias[j], NEG_INF) for j in range(DEC_SEQ)]
    m_old = fl.m[...]
    m_new = m_old
    for s in s_new:
        m_new = jnp.maximum(m_new, s)
    alpha = jnp.exp(m_old - m_new)
    l = alpha * fl.l[...]
    acc = alpha * fl.acc[...]
    for j in range(DEC_SEQ):
        p = jnp.where(masks[j], jnp.exp(s_new[j] - m_new), 0.0)
        l = l + p
        acc = acc + p * vn[j:j + 1, :]
    return _safe_div(acc, l)


def _fox_sample_kernel(pt_ref, q_ref, kn_ref, vn_ref, k_ref, v_ref, c_ref, lfn_ref, o_ref,
                       m_ref, l_ref, acc_ref, run_ref):
    p = pl.program_id(1)

    @pl.when(p == 0)
    def _():
        m_ref[...] = jnp.full(m_ref.shape, NEG_INF, F32)
        l_ref[...] = jnp.zeros(l_ref.shape, F32)
        acc_ref[...] = jnp.zeros(acc_ref.shape, F32)
        run_ref[...] = jnp.zeros(run_ref.shape, F32)

    ck = run_ref[...] + c_ref[...]
    for h in range(FOX_HEADS):
        hs = slice(h * HEAD_DIM, (h + 1) * HEAD_DIM)
        fl = _Flash(m_ref.at[h], l_ref.at[h], acc_ref.at[h])
        s = _dot_nt(q_ref[h].astype(BF16), k_ref[:, hs].astype(BF16)) * ATTN_SCALE - ck[h:h + 1, :]
        fl.update(s, v_ref[:, hs].astype(BF16))
    run_ref[...] = ck[:, PAGE_SIZE - 1:PAGE_SIZE]

    @pl.when(p == N_PAGES - 1)
    def _():
        row = lax.broadcasted_iota(jnp.int32, (ROWS, 1), 0)
        total = run_ref[...]
        lfn = lfn_ref[...]
        c_new, c = [], total
        for j in range(DEC_SEQ):
            c = c + lfn[:, j:j + 1]
            c_new.append(c)
        for h in range(FOX_HEADS):
            hs = slice(h * HEAD_DIM, (h + 1) * HEAD_DIM)
            fl = _Flash(m_ref.at[h], l_ref.at[h], acc_ref.at[h])
            o_ref[h] = _merge_new_keys(fl, q_ref[h], kn_ref[:, hs], vn_ref[:, hs],
                                       [-c_new[j][h:h + 1, :] for j in range(DEC_SEQ)],
                                       [row >= j for j in range(DEC_SEQ)])


def fox_sample(q16, ps3, pool_k, pool_v, ct_pool, lfn, page_table):
    nb = q16.shape[0]
    grid_spec = pltpu.PrefetchScalarGridSpec(
        num_scalar_prefetch=1,
        grid=(nb, N_PAGES),
        in_specs=[
            pl.BlockSpec((None, FOX_HEADS, ROWS, HEAD_DIM), lambda b, p, pt: (b, 0, 0, 0)),
            pl.BlockSpec((None, DEC_SEQ, SELF_W), lambda b, p, pt: (b, 0, 1)),
            pl.BlockSpec((None, DEC_SEQ, SELF_W), lambda b, p, pt: (b, 0, 2)),
            pl.BlockSpec((None, PAGE_SIZE, SELF_W), lambda b, p, pt: (pt[b, p], 0, 0)),
            pl.BlockSpec((None, PAGE_SIZE, SELF_W), lambda b, p, pt: (pt[b, p], 0, 0)),
            pl.BlockSpec((None, HEAD_PAD, PAGE_SIZE), lambda b, p, pt: (pt[b, p], 0, 0)),
            pl.BlockSpec((None, HEAD_PAD, DEC_SEQ), lambda b, p, pt: (b, 0, 0)),
        ],
        out_specs=pl.BlockSpec((None, FOX_HEADS, ROWS, HEAD_DIM), lambda b, p, pt: (b, 0, 0, 0)),
        scratch_shapes=[pltpu.VMEM((FOX_HEADS, ROWS, 1), F32), pltpu.VMEM((FOX_HEADS, ROWS, 1), F32),
                        pltpu.VMEM((FOX_HEADS, ROWS, HEAD_DIM), F32), pltpu.VMEM((HEAD_PAD, 1), F32)],
    )
    return pl.pallas_call(
        _fox_sample_kernel,
        grid_spec=grid_spec,
        out_shape=jax.ShapeDtypeStruct((nb, FOX_HEADS, ROWS, HEAD_DIM), F32),
        compiler_params=_cparams(("arbitrary", "arbitrary")),
        name="fox_sample",
    )(page_table, q16, ps3, ps3, pool_k, pool_v, ct_pool, lfn)


N_SEL_TILES = N_PAGES + 1


def _nsa_sample_cmp_kernel(pt_ref, rb_ref, q_ref, *refs):
    ck_pages, cv_pages = refs[:N_PAGES], refs[N_PAGES:2 * N_PAGES]
    ckn_ref, cvn_ref, oc_ref, sx_ref, ck_s, cv_s = refs[2 * N_PAGES:]
    b = pl.program_id(0)
    blk_per_page = PAGE_SIZE // CMP_BLOCK

    @pl.when(b == 0)
    def _():
        ck_s[...] = jnp.zeros(ck_s.shape, F32)
        cv_s[...] = jnp.zeros(cv_s.shape, F32)

    for p in range(N_PAGES):
        ck_s[p * blk_per_page:(p + 1) * blk_per_page, :] = ck_pages[p][...]
        cv_s[p * blk_per_page:(p + 1) * blk_per_page, :] = cv_pages[p][...]
    ck_s[N_PAST_BLK:N_PAST_BLK + 1, :] = ckn_ref[...]
    cv_s[N_PAST_BLK:N_PAST_BLK + 1, :] = cvn_ref[...]

    row, tok, r_idx = _row_ids()
    lane = lax.broadcasted_iota(jnp.int32, (ROWS, LANE), 1)
    qpos = PAST_LEN + tok
    dist = qpos - (lane * CMP_BLOCK + (CMP_BLOCK - 1))
    valid = (dist >= 0) & (lane <= N_PAST_BLK)
    for g in range(NSA_KV_HEADS):
        gs = slice(g * HEAD_DIM, (g + 1) * HEAD_DIM)
        s = _dot_nt(q_ref[g].astype(BF16), ck_s[:, gs].astype(BF16)) * ATTN_SCALE
        s = s + _per_head(r_idx, lambda r: _t5_bias(dist, rb_ref, NSA_GROUP * g + r))
        p = _softmax_rows(s, valid)
        oc_ref[g] = _dot(p.astype(BF16), cv_s[:, gs].astype(BF16))
        p0 = jnp.where(row < NSA_GROUP * DEC_SEQ, p, 0.0)
        imp = p0 + pltpu.roll(p0, ROWS - DEC_SEQ, 0) + pltpu.roll(p0, ROWS - 2 * DEC_SEQ, 0)
        imp = jnp.where(row < DEC_SEQ, imp, 0.0)
        imp = imp + pltpu.roll(imp, DEC_SEQ, 0) + pltpu.roll(imp, 2 * DEC_SEQ, 0)
        sel = _select_blocks(imp, qpos // SLC_BLOCK, lane, N_PAST_BLK + 1).astype(BF16)
        for t in range(N_SEL_TILES):
            kk = lax.broadcasted_iota(jnp.int32, (LANE, LANE), 1) + t * LANE
            bj = lax.broadcasted_iota(jnp.int32, (LANE, LANE), 0)
            expand = jnp.where(kk // SLC_BLOCK == bj, 1.0, 0.0).astype(BF16)
            sx_ref[g, :, t * LANE:(t + 1) * LANE] = _dot(sel, expand)


def nsa_sample_cmp(qg16, ck_pool, cv_pool, ck_new, cv_new, rel_bias, page_table):
    nb = qg16.shape[0]
    page = lambda p: pl.BlockSpec((None, PAGE_SIZE // CMP_BLOCK, NSA_KV_W), lambda b, pt: (pt[b, p], 0, 0))
    grid_spec = pltpu.PrefetchScalarGridSpec(
        num_scalar_prefetch=1,
        grid=(nb,),
        in_specs=[pl.BlockSpec(memory_space=pltpu.SMEM),
                  pl.BlockSpec((None, NSA_KV_HEADS, ROWS, HEAD_DIM), lambda b, pt: (b, 0, 0, 0))]
        + [page(p) for p in range(N_PAGES)] + [page(p) for p in range(N_PAGES)]
        + [pl.BlockSpec((None, 1, NSA_KV_W), lambda b, pt: (b, 0, 0))] * 2,
        out_specs=[pl.BlockSpec((None, NSA_KV_HEADS, ROWS, HEAD_DIM), lambda b, pt: (b, 0, 0, 0)),
                   pl.BlockSpec((None, NSA_KV_HEADS, ROWS, N_SEL_TILES * LANE), lambda b, pt: (b, 0, 0, 0))],
        scratch_shapes=[pltpu.VMEM((LANE, NSA_KV_W), F32), pltpu.VMEM((LANE, NSA_KV_W), F32)],
    )
    return pl.pallas_call(
        _nsa_sample_cmp_kernel,
        grid_spec=grid_spec,
        out_shape=[jax.ShapeDtypeStruct((nb, NSA_KV_HEADS, ROWS, HEAD_DIM), F32),
                   jax.ShapeDtypeStruct((nb, NSA_KV_HEADS, ROWS, N_SEL_TILES * LANE), F32)],
        compiler_params=_cparams(("arbitrary",)),
        name="nsa_sample_cmp",
    )(page_table, rel_bias, qg16, *([ck_pool] * N_PAGES), *([cv_pool] * N_PAGES), ck_new, cv_new)


def _nsa_sample_kernel(pt_ref, rb_ref, q_ref, k_ref, v_ref, sx_ref, sxn_ref, kn_ref, vn_ref,
                       wk_ref, wv_ref, wkn_ref, wvn_ref, oc_ref, gl_ref, bg_ref, o_ref,
                       m_ref, l_ref, acc_ref, ow_ref, b15_ref, bn_ref):
    p = pl.program_id(1)
    row, tok, r_idx = _row_ids()
    lane = lax.broadcasted_iota(jnp.int32, (ROWS, LANE), 1)
    tok1, r1 = tok[:, 0:1], r_idx[:, 0:1]
    new_vis = [tok1 >= j for j in range(DEC_SEQ)]

    def far_bias(g):
        return _per_head(r1, lambda r: rb_ref[N_BUCKETS - 1, NSA_GROUP * g + r])

    @pl.when(p == 0)
    def _():
        m_ref[...] = jnp.full(m_ref.shape, NEG_INF, F32)
        l_ref[...] = jnp.zeros(l_ref.shape, F32)
        acc_ref[...] = jnp.zeros(acc_ref.shape, F32)
        for g in range(NSA_KV_HEADS):
            gs = slice(g * HEAD_DIM, (g + 1) * HEAD_DIM)
            b15_ref[g] = _per_head(r_idx, lambda r: _t5_bias(LANE + tok - lane, rb_ref, NSA_GROUP * g + r))
            bn_ref[g] = _per_head(r_idx, lambda r: _t5_bias(tok - lane, rb_ref, NSA_GROUP * g + r))
            q32 = q_ref[g]
            s = _dot_nt(q32.astype(BF16), wk_ref[:, gs].astype(BF16)) * ATTN_SCALE
            bias = jnp.concatenate([jnp.broadcast_to(far_bias(g), (ROWS, WINDOW - LANE)), b15_ref[g]], axis=1)
            jw = lax.broadcasted_iota(jnp.int32, (ROWS, WINDOW), 1)
            in_win = jw > lax.broadcasted_iota(jnp.int32, (ROWS, WINDOW), 0) % DEC_SEQ
            s = jnp.where(in_win, s + bias, NEG_INF)
            s_new = [jnp.where(new_vis[j], jnp.sum(q32 * wkn_ref[j:j + 1, gs], axis=-1, keepdims=True)
                               * ATTN_SCALE + bn_ref[g][:, j:j + 1], NEG_INF) for j in range(DEC_SEQ)]
            mx = jnp.max(s, axis=-1, keepdims=True)
            for sn in s_new:
                mx = jnp.maximum(mx, sn)
            e = jnp.where(in_win, jnp.exp(s - mx), 0.0)
            den = jnp.sum(e, axis=-1, keepdims=True)
            acc = _dot(e.astype(BF16), wv_ref[:, gs].astype(BF16))
            for j in range(DEC_SEQ):
                pj = jnp.where(new_vis[j], jnp.exp(s_new[j] - mx), 0.0)
                den = den + pj
                acc = acc + pj * wvn_ref[j:j + 1, gs]
            ow_ref[g] = _safe_div(acc, den)

    for g in range(NSA_KV_HEADS):
        gs = slice(g * HEAD_DIM, (g + 1) * HEAD_DIM)
        fl = _Flash(m_ref.at[g], l_ref.at[g], acc_ref.at[g])
        s = _dot_nt(q_ref[g].astype(BF16), k_ref[:, gs].astype(BF16)) * ATTN_SCALE
        s = s + jnp.where(p == N_PAGES - 1, b15_ref[g], far_bias(g))
        fl.update(s, v_ref[:, gs].astype(BF16), sx_ref[g] > 0.5)

    @pl.when(p == N_PAGES - 1)
    def _():
        gate = jax.nn.sigmoid(gl_ref[...] + bg_ref[...].reshape(NSA_KV_HEADS, 1, LANE))
        for g in range(NSA_KV_HEADS):
            gs = slice(g * HEAD_DIM, (g + 1) * HEAD_DIM)
            fl = _Flash(m_ref.at[g], l_ref.at[g], acc_ref.at[g])
            picked = sxn_ref[g][:, 0:1] > 0.5
            o_s = _merge_new_keys(fl, q_ref[g], kn_ref[:, gs], vn_ref[:, gs],
                                  [bn_ref[g][:, j:j + 1] for j in range(DEC_SEQ)],
                                  [new_vis[j] & picked for j in range(DEC_SEQ)])
            gt = [_per_head(r1, lambda r: gate[g][:, 3 * r + c:3 * r + c + 1]) for c in range(3)]
            o_ref[g] = gt[0] * oc_ref[g] + gt[1] * o_s + gt[2] * ow_ref[g]


def nsa_sample(qg16, ps3, pool_k, pool_v, selx, buf_k, buf_v, o_c, gl16, b_gate, rel_bias, page_table):
    nb = qg16.shape[0]
    cb = lambda c: c // NSA_KV_W
    qspec = pl.BlockSpec((None, NSA_KV_HEADS, ROWS, HEAD_DIM), lambda b, p, pt: (b, 0, 0, 0))
    new = lambda c: pl.BlockSpec((None, DEC_SEQ, NSA_KV_W), lambda b, p, pt: (b, 0, cb(c)))
    grid_spec = pltpu.PrefetchScalarGridSpec(
        num_scalar_prefetch=1,
        grid=(nb, N_PAGES),
        in_specs=[
            pl.BlockSpec(memory_space=pltpu.SMEM),
            qspec,
            pl.BlockSpec((None, PAGE_SIZE, NSA_KV_W), lambda b, p, pt: (pt[b, p], 0, 0)),
            pl.BlockSpec((None, PAGE_SIZE, NSA_KV_W), lambda b, p, pt: (pt[b, p], 0, 0)),
            pl.BlockSpec((None, NSA_KV_HEADS, ROWS, LANE), lambda b, p, pt: (b, 0, 0, p)),
            pl.BlockSpec((None, NSA_KV_HEADS, ROWS, LANE), lambda b, p, pt: (b, 0, 0, N_PAGES)),
            new(SELF_W + 2 * NSA_KV_W), new(SELF_W + 3 * NSA_KV_W),
            pl.BlockSpec((None, WINDOW, NSA_KV_W), lambda b, p, pt: (b, 0, 0)),
            pl.BlockSpec((None, WINDOW, NSA_KV_W), lambda b, p, pt: (b, 0, 0)),
            new(SELF_W + 4 * NSA_KV_W), new(SELF_W + 5 * NSA_KV_W),
            qspec, qspec,
            pl.BlockSpec((1, CROSS_W), lambda b, p, pt: (0, 0)),
        ],
        out_specs=qspec,
        scratch_shapes=[pltpu.VMEM((NSA_KV_HEADS, ROWS, 1), F32), pltpu.VMEM((NSA_KV_HEADS, ROWS, 1), F32),
                        pltpu.VMEM((NSA_KV_HEADS, ROWS, HEAD_DIM), F32),
                        pltpu.VMEM((NSA_KV_HEADS, ROWS, HEAD_DIM), F32),
                        pltpu.VMEM((NSA_KV_HEADS, ROWS, LANE), F32),
                        pltpu.VMEM((NSA_KV_HEADS, ROWS, LANE), F32)],
    )
    return pl.pallas_call(
        _nsa_sample_kernel,
        grid_spec=grid_spec,
        out_shape=jax.ShapeDtypeStruct((nb, NSA_KV_HEADS, ROWS, HEAD_DIM), F32),
        compiler_params=_cparams(("arbitrary", "arbitrary")),
        name="nsa_sample",
    )(page_table, rel_bias, qg16, pool_k, pool_v, selx, selx, ps3, ps3, buf_k, buf_v, ps3, ps3,
      o_c, gl16, b_gate)


def _cross_sample_kernel(q_ref, mk_ref, mv_ref, o_ref):
    for h in range(N_CROSS_HEADS):
        hs = slice(h * HEAD_DIM, (h + 1) * HEAD_DIM)
        s = _dot_nt(q_ref[h].astype(BF16), mk_ref[:, hs].astype(BF16)) * ATTN_SCALE
        p = _softmax_rows(s, s == s)
        o_ref[h] = _dot(p.astype(BF16), mv_ref[:, hs].astype(BF16))


def cross_sample(qx16, mem_k, mem_v):
    nb = qx16.shape[0]
    qspec = pl.BlockSpec((None, N_CROSS_HEADS, ROWS, HEAD_DIM), lambda b: (b, 0, 0, 0))
    mspec = pl.BlockSpec((None, MEM_LEN, CROSS_W), lambda b: (b, 0, 0))
    return pl.pallas_call(
        _cross_sample_kernel,
        grid=(nb,),
        in_specs=[qspec, mspec, mspec],
        out_specs=qspec,
        out_shape=jax.ShapeDtypeStruct((nb, N_CROSS_HEADS, ROWS, HEAD_DIM), F32),
        compiler_params=_cparams(("arbitrary",)),
        name="cross_sample",
    )(qx16, mem_k, mem_v)


def _arrange_in_weights(w, n_gate, per_group):
    gate = w[:, COL_QX:COL_QX + n_gate]
    if per_group:
        gate = jnp.pad(gate.reshape(-1, NSA_KV_HEADS, n_gate // NSA_KV_HEADS),
                       ((0, 0), (0, 0), (0, LANE - n_gate // NSA_KV_HEADS))).reshape(-1, CROSS_W)
    else:
        gate = jnp.pad(gate, ((0, 0), (0, CROSS_W - n_gate)))
    return jnp.concatenate([w[:, COL_QX + n_gate:], gate], axis=1)


def kernel(x_prompt, x_sample, mem_prompt, cache_nsa_cmp_k, cache_nsa_cmp_v, cache_nsa_slc_k, cache_nsa_slc_v, cache_nsa_win_k, cache_nsa_win_v, cache_fox_k, cache_fox_v, cache_fox_logf, cache_mem_k, cache_mem_v, page_table, rel_bias, norm_g, mem_norm_g, w_mem_kv, w_ff_gu, w_ff_down, w_in_nsa, b_gate_nsa, w_cmp1, w_cmp2, cmp_pe, w_out_nsa, w_in_fox, b_forget, w_out_fox, final_norm_g):
    n_p = BATCH * SEQ
    x = jnp.concatenate([x_prompt.reshape(n_p, D_MODEL), x_sample.reshape(-1, D_MODEL)], axis=0)
    mem = mem_prompt.reshape(BATCH * MEM_LEN, D_MODEL)
    n_pool = cache_nsa_cmp_k.shape[1]
    blocks_per_tok = PROJ_W // NSA_KV_W
    kv_heads = (BATCH, SEQ, NSA_KV_HEADS, HEAD_DIM)
    out = {}
    for i in range(DEPTH):
        x = ffn(x, norm_g[i, 0], w_ff_gu[i, 0], w_ff_down[i, 0])
        kvm = linear(mem, w_mem_kv[i], g=mem_norm_g[i], tm=512)
        out[f"mem_k{i}"] = kvm[:, :CROSS_W].reshape(BATCH, MEM_LEN, N_CROSS_HEADS, HEAD_DIM)
        out[f"mem_v{i}"] = kvm[:, CROSS_W:].reshape(BATCH, MEM_LEN, N_CROSS_HEADS, HEAD_DIM)
        a = i // 2
        if i % 2 == 0:
            w_tail = _arrange_in_weights(w_in_nsa[a], 3 * N_SELF_HEADS, True)
            proj = linear(x, w_in_nsa[a], w_tail, COL_QX // 512, g=norm_g[i, 1])
            ps3 = proj[n_p:].reshape(DEC_BATCH, DEC_SEQ, PROJ_W)
            bg = jnp.pad(b_gate_nsa[a].reshape(NSA_KV_HEADS, 3 * NSA_GROUP),
                         ((0, 0), (0, LANE - 3 * NSA_GROUP))).reshape(1, CROSS_W)
            cmp_w = [(w_cmp1[a, c], w_cmp2[a, c], cmp_pe[a, c]) for c in range(2)]
            x_blk = proj.reshape(N_TOK // CMP_BLOCK, CMP_BLOCK * PROJ_W)
            x_new = proj.reshape(N_TOK // DEC_SEQ, DEC_SEQ * PROJ_W)
            pools = [cache_nsa_cmp_k[a], cache_nsa_cmp_v[a]]
            c_prompt, c_pool, c_new = [], [], []
            for c in range(2):
                col = lambda l, c=c: blocks_per_tok * l + SELF_W // NSA_KV_W + c
                c_prompt.append(compress(x_blk, col, CMP_BLOCK, *cmp_w[c], tr=BATCH * SEQ // CMP_BLOCK,
                                         rows=BATCH * SEQ // CMP_BLOCK))
                c_pool.append(compress(pools[c].reshape(n_pool * PAGE_SIZE // CMP_BLOCK, CMP_BLOCK * NSA_KV_W),
                                       lambda l: l, CMP_BLOCK, *cmp_w[c], tr=1024)
                              .reshape(n_pool, PAGE_SIZE // CMP_BLOCK, NSA_KV_W))
                c_new.append(compress(x_new, col, DEC_SEQ, *cmp_w[c], tr=DEC_BATCH, rows=DEC_BATCH,
                                      row_block0=n_p // DEC_SEQ // DEC_BATCH).reshape(DEC_BATCH, 1, NSA_KV_W))
            o_p, ox_p = nsa_prompt(proj, c_prompt[0], c_prompt[1], kvm, rel_bias, bg, BATCH)
            qg16 = _group_rows(ps3[:, :, :SELF_W])
            o_c, selx = nsa_sample_cmp(qg16, c_pool[0], c_pool[1], c_new[0], c_new[1], rel_bias, page_table)
            o_s = _ungroup_rows(nsa_sample(
                qg16, ps3, cache_nsa_slc_k[a].reshape(n_pool, PAGE_SIZE, NSA_KV_W),
                cache_nsa_slc_v[a].reshape(n_pool, PAGE_SIZE, NSA_KV_W), selx,
                cache_nsa_win_k[a].reshape(DEC_BATCH, WINDOW, NSA_KV_W),
                cache_nsa_win_v[a].reshape(DEC_BATCH, WINDOW, NSA_KV_W),
                o_c, _gate_rows(ps3[:, :, COL_GATE:]), bg, rel_bias, page_table))
            for j, name in enumerate(("cmp_k", "cmp_v", "slc_k", "slc_v", "win_k", "win_v")):
                cols = slice(SELF_W + j * NSA_KV_W, SELF_W + (j + 1) * NSA_KV_W)
                st_p = proj[:n_p, cols].reshape(kv_heads)
                st_s = ps3[:, :, cols].reshape(DEC_BATCH, DEC_SEQ, NSA_KV_HEADS, HEAD_DIM)
                if name.startswith("win"):
                    buf = (cache_nsa_win_k if name == "win_k" else cache_nsa_win_v)[a]
                    st_p = st_p[:, -WINDOW:]
                    st_s = jnp.concatenate([buf[:, DEC_SEQ:], st_s], axis=1)
                out.setdefault("p_" + name, []).append(st_p)
                out.setdefault("s_" + name, []).append(st_s)
            w_out = w_out_nsa[a]
        else:
            w_tail = _arrange_in_weights(w_in_fox[a], FOX_HEADS, False)
            proj = linear(x, w_in_fox[a], w_tail, COL_QX // 512, g=norm_g[i, 1])
            ps3 = proj[n_p:].reshape(DEC_BATCH, DEC_SEQ, PROJ_W)
            lf_p, ct = fox_prep(proj, b_forget[a], BATCH)
            n_s = DEC_BATCH * DEC_SEQ
            lf_s, _ = fox_prep(proj, b_forget[a], 1, seq_len=n_s, row_block0=n_p // n_s)
            o_p = fox_prompt(proj, ct, BATCH)
            ox_p = cross_prompt(proj, kvm, BATCH)
            ct_pool = jnp.pad(jnp.cumsum(cache_fox_logf[a].astype(F32), axis=1).transpose(0, 2, 1),
                              ((0, 0), (0, HEAD_PAD - FOX_HEADS), (0, 0)))
            lfn = lf_s[:, :HEAD_PAD].reshape(DEC_BATCH, DEC_SEQ, HEAD_PAD).transpose(0, 2, 1)
            o_s = _unhead_rows(fox_sample(
                _head_rows(ps3[:, :, :SELF_W], FOX_HEADS), ps3,
                cache_fox_k[a].reshape(n_pool, PAGE_SIZE, SELF_W),
                cache_fox_v[a].reshape(n_pool, PAGE_SIZE, SELF_W), ct_pool, lfn, page_table))
            heads = (FOX_HEADS, HEAD_DIM)
            for j, name in enumerate(("fox_k", "fox_v")):
                cols = slice((j + 1) * SELF_W, (j + 2) * SELF_W)
                out.setdefault("p_" + name, []).append(proj[:n_p, cols].reshape(BATCH, SEQ, *heads))
                out.setdefault("s_" + name, []).append(ps3[:, :, cols].reshape(DEC_BATCH, DEC_SEQ, *heads))
            out.setdefault("p_fox_logf", []).append(lf_p[:, :FOX_HEADS].reshape(BATCH, SEQ, FOX_HEADS))
            out.setdefault("s_fox_logf", []).append(lf_s[:, :FOX_HEADS].reshape(DEC_BATCH, DEC_SEQ, FOX_HEADS))
            w_out = w_out_fox[a]
        ox_s = _unhead_rows(cross_sample(
            _head_rows(ps3[:, :, COL_QX:COL_GATE], N_CROSS_HEADS),
            cache_mem_k[i].reshape(DEC_BATCH, MEM_LEN, CROSS_W),
            cache_mem_v[i].reshape(DEC_BATCH, MEM_LEN, CROSS_W)))
        mix_self = jnp.concatenate([o_p, o_s.astype(BF16)], axis=0)
        mix_cross = jnp.concatenate([ox_p, ox_s.astype(BF16)], axis=0)
        x = linear([mix_self, mix_cross], w_out, res=x)
        x = ffn(x, norm_g[i, 2], w_ff_gu[i, 1], w_ff_down[i, 1],
                final_g=final_norm_g if i == DEPTH - 1 else None)
    y_prompt = x[:n_p].reshape(BATCH, SEQ, D_MODEL)
    y_sample = x[n_p:].reshape(DEC_BATCH, DEC_SEQ, D_MODEL)
    st = lambda name: jnp.stack(out[name])
    p_mem_k = jnp.stack([out[f"mem_k{i}"] for i in range(DEPTH)])
    p_mem_v = jnp.stack([out[f"mem_v{i}"] for i in range(DEPTH)])
    return (y_prompt, y_sample,
            st("p_cmp_k"), st("p_cmp_v"), st("p_slc_k"), st("p_slc_v"), st("p_win_k"), st("p_win_v"),
            st("p_fox_k"), st("p_fox_v"), st("p_fox_logf"), p_mem_k, p_mem_v,
            st("s_cmp_k"), st("s_cmp_v"), st("s_slc_k"), st("s_slc_v"), st("s_win_k"), st("s_win_v"),
            st("s_fox_k"), st("s_fox_v"), st("s_fox_logf"))
```

```python
import functools
import math

import jax
import jax.numpy as jnp
from jax import lax
from jax.experimental import pallas as pl
from jax.experimental.pallas import tpu as pltpu

D_MODEL = 2048
BATCH = 4
SEQ = 2048
DEPTH = 2
DEC_BATCH = 128
DEC_SEQ = 4
PAST_LEN = 2048
PAGE_SIZE = 128
HEAD_DIM = 128
N_CROSS_HEADS = 4
N_SELF_HEADS = 12
NSA_KV_HEADS = 4
NSA_GROUP = N_SELF_HEADS // NSA_KV_HEADS
CMP_BLOCK = 64
SLC_BLOCK = 64
N_SEL = 16
WINDOW = 512
CMP_HIDDEN = 256
FOX_HEADS = N_SELF_HEADS
MEM_LEN = 256
N_BUCKETS = 32
MAX_DISTANCE = 128
FFN_DIM = ((8 * D_MODEL) // 3 + 127) // 128 * 128
RMS_EPS = 1e-6
FORCE_SCORE = 1e4
NEG_INF = -1e30
ATTN_SCALE = HEAD_DIM ** -0.5
SELF_W = N_SELF_HEADS * HEAD_DIM
NSA_KV_W = NSA_KV_HEADS * HEAD_DIM
CROSS_W = N_CROSS_HEADS * HEAD_DIM
N_PAGES = PAST_LEN // PAGE_SIZE
N_PAST_BLK = PAST_LEN // SLC_BLOCK

LANE = 128
VMEM_LIMIT = 56 * 1024 * 1024
N_TOK = BATCH * SEQ + DEC_BATCH * DEC_SEQ
TOK_TILE = 544
Q_TILE = 128
COL_QX = 3 * SELF_W
COL_GATE = COL_QX + CROSS_W
PROJ_W = COL_GATE + CROSS_W

F32 = jnp.float32
BF16 = jnp.bfloat16
_NT = (((1,), (1,)), ((), ()))


def _cparams(sem):
    return pltpu.CompilerParams(dimension_semantics=sem, vmem_limit_bytes=VMEM_LIMIT)


def _rms(x, g):
    return x * lax.rsqrt(jnp.mean(x * x, axis=-1, keepdims=True) + RMS_EPS) * g


def _dot(a, b):
    return jnp.dot(a, b, preferred_element_type=F32)


def _dot_nt(a, b):
    return lax.dot_general(a, b, _NT, preferred_element_type=F32)


def _t5_bias(dist, rb_ref, h):
    n = jnp.maximum(dist, 0)
    max_exact = N_BUCKETS // 2
    nf = jnp.maximum(n, 1).astype(F32)
    large = max_exact + (jnp.log(nf / max_exact) / math.log(MAX_DISTANCE / max_exact)
                         * (N_BUCKETS - max_exact)).astype(jnp.int32)
    bucket = jnp.where(n <= max_exact, n, jnp.minimum(large, N_BUCKETS - 1))
    out = jnp.zeros(dist.shape, F32)
    for b in range(N_BUCKETS):
        out = jnp.where(bucket == b, rb_ref[b, h], out)
    return out


def _softmax_rows(s, mask):
    s = jnp.where(mask, s, NEG_INF)
    e = jnp.where(mask, jnp.exp(s - jnp.max(s, axis=-1, keepdims=True)), 0.0)
    den = jnp.sum(e, axis=-1, keepdims=True)
    return e / jnp.where(den > 0.0, den, 1.0)


def _select_blocks(imp, cur, col, n_blocks):
    forced = (col == 0) | (col == cur) | (col == cur - 1)
    score = jnp.where(col <= cur, jnp.where(forced, FORCE_SCORE, imp), -1.0)
    rank = jnp.zeros(score.shape, jnp.int32)
    for i in range(n_blocks):
        ci = score[:, i:i + 1]
        beats = (ci > score) | ((ci == score) & (col > i))
        rank = rank + beats.astype(jnp.int32)
    return (rank < N_SEL) & (col <= cur)


class _Flash:
    def __init__(self, m_ref, l_ref, acc_ref):
        self.m, self.l, self.acc = m_ref, l_ref, acc_ref

    def reset(self):
        self.m[...] = jnp.full(self.m.shape, NEG_INF, F32)
        self.l[...] = jnp.zeros(self.l.shape, F32)
        self.acc[...] = jnp.zeros(self.acc.shape, F32)

    def update(self, s, v, mask=None):
        if mask is not None:
            s = jnp.where(mask, s, NEG_INF)
        m_old = self.m[...]
        m_new = jnp.maximum(m_old, jnp.max(s, axis=-1, keepdims=True))
        alpha = jnp.exp(m_old - m_new)
        p = jnp.exp(s - m_new)
        if mask is not None:
            p = jnp.where(mask, p, 0.0)
        self.l[...] = alpha * self.l[...] + jnp.sum(p, axis=-1, keepdims=True)
        self.acc[...] = alpha * self.acc[...] + _dot(p.astype(BF16), v)
        self.m[...] = m_new

    def result(self):
        return self.acc[...] / self.l[...]


FFN_TILES = 4


def _ffn_kernel(x_ref, g_ref, fg_ref, *refs, n_tiles, n_steps, final_norm):
    t = FFN_TILES
    wg, wu, wd = refs[:t], refs[t:2 * t], refs[2 * t:3 * t]
    o_ref, h_ref = refs[3 * t:]
    f = pl.program_id(1)

    @pl.when(f == 0)
    def _():
        x = x_ref[...]
        h_ref[...] = _rms(x, g_ref[...]).astype(BF16)
        o_ref[...] = x

    h = h_ref[...]
    a = _dot(h, jnp.concatenate([r[...].astype(BF16) for r in wg], axis=1))
    b = _dot(h, jnp.concatenate([r[...].astype(BF16) for r in wu], axis=1))
    act = a * jax.nn.sigmoid(a) * b
    tile = t * f + lax.broadcasted_iota(jnp.int32, act.shape, 1) // LANE
    act = jnp.where(tile < n_tiles, act, 0.0)
    w_down = jnp.concatenate([r[...].astype(BF16) for r in wd], axis=0)
    o_ref[...] += 0.5 * _dot(act.astype(BF16), w_down)

    if final_norm:
        @pl.when(f == n_steps - 1)
        def _():
            o_ref[...] = _rms(o_ref[...], fg_ref[...])


def ffn(x, g, w_gu, w_down, which, final_g=None):
    m, d = x.shape
    n_f = FFN_DIM // LANE
    n_steps = pl.cdiv(n_f, FFN_TILES)
    tm = TOK_TILE
    final_norm = final_g is not None
    fg = final_g if final_norm else g

    def col(off, k):
        return lambda i, f: (*which, 0, off + jnp.minimum(FFN_TILES * f + k, n_f - 1))

    def row(k):
        return lambda i, f: (*which, jnp.minimum(FFN_TILES * f + k, n_f - 1), 0)

    ks = range(FFN_TILES)
    return pl.pallas_call(
        functools.partial(_ffn_kernel, n_tiles=n_f, n_steps=n_steps, final_norm=final_norm),
        grid=(m // tm, n_steps),
        in_specs=[
            pl.BlockSpec((tm, d), lambda i, f: (i, 0)),
            pl.BlockSpec((1, d), lambda i, f: (0, 0)),
            pl.BlockSpec((1, d), lambda i, f: (0, 0)),
        ] + [pl.BlockSpec((None, None, d, LANE), col(0, k)) for k in ks]
        + [pl.BlockSpec((None, None, d, LANE), col(n_f, k)) for k in ks]
        + [pl.BlockSpec((None, None, LANE, d), row(k)) for k in ks],
        out_specs=pl.BlockSpec((tm, d), lambda i, f: (i, 0)),
        out_shape=jax.ShapeDtypeStruct((m, d), F32),
        scratch_shapes=[pltpu.VMEM((tm, d), BF16)],
        compiler_params=_cparams(("parallel", "arbitrary")),
        name="ffn",
    )(x, g.reshape(1, d), fg.reshape(1, d), *([w_gu] * (2 * FFN_TILES)), *([w_down] * FFN_TILES))


def _linear_kernel(*refs, n_x, norm, n_main, residual):
    x_refs = refs[:n_x]
    refs = refs[n_x:]
    if norm:
        g_ref, refs = refs[0], refs[1:]
    w_ref, wt_ref = refs[:2]
    refs = refs[2:]
    if residual:
        r_ref, o_ref, h_ref = refs
    else:
        o_ref, h_ref = refs
    j = pl.program_id(1)

    @pl.when(j == 0)
    def _():
        off = 0
        for x_ref in x_refs:
            x = x_ref[...].astype(F32)
            if norm:
                x = _rms(x, g_ref[...])
            h_ref[:, off:off + x.shape[1]] = x.astype(BF16)
            off += x.shape[1]

    def emit(w):
        y = _dot(h_ref[...], w[...].astype(BF16))
        if residual:
            y = y + r_ref[...]
        o_ref[...] = y

    pl.when(j < n_main)(lambda: emit(w_ref))
    pl.when(j >= n_main)(lambda: emit(wt_ref))


def linear(xs, w, w_tail=None, n_main=None, g=None, res=None, tm=TOK_TILE, tn=512):
    xs = list(xs) if isinstance(xs, (list, tuple)) else [xs]
    m = xs[0].shape[0]
    k = sum(x.shape[1] for x in xs)
    if w_tail is None:
        w_tail, n_main = w, w.shape[1] // tn
    n_blocks = n_main + (w_tail.shape[1] // tn if w_tail is not w else 0)
    norm, residual = g is not None, res is not None
    in_specs = [pl.BlockSpec((tm, x.shape[1]), lambda i, j: (i, 0)) for x in xs]
    args = list(xs)
    if norm:
        in_specs.append(pl.BlockSpec((1, k), lambda i, j: (0, 0)))
        args.append(g.reshape(1, k))
    in_specs.append(pl.BlockSpec((k, tn), lambda i, j: (0, jnp.minimum(j, n_main - 1))))
    in_specs.append(pl.BlockSpec((k, tn), lambda i, j: (0, jnp.maximum(j - n_main, 0))))
    args += [w, w_tail]
    if residual:
        in_specs.append(pl.BlockSpec((tm, tn), lambda i, j: (i, j)))
        args.append(res)
    return pl.pallas_call(
        functools.partial(_linear_kernel, n_x=len(xs), norm=norm, n_main=n_main, residual=residual),
        grid=(m // tm, n_blocks),
        in_specs=in_specs,
        out_specs=pl.BlockSpec((tm, tn), lambda i, j: (i, j)),
        out_shape=jax.ShapeDtypeStruct((m, n_blocks * tn), F32),
        scratch_shapes=[pltpu.VMEM((tm, k), BF16)],
        compiler_params=_cparams(("parallel", "arbitrary")),
        name="linear",
    )(*args)


def _compress_kernel(x_ref, pe_ref, w1_ref, w2_ref, o_ref, acc_ref, *, n_valid):
    l = pl.program_id(1)

    @pl.when(l == 0)
    def _():
        acc_ref[...] = jnp.zeros(acc_ref.shape, F32)

    pe = pe_ref[pl.ds(l, 1), :]
    w1 = w1_ref[...].astype(BF16)
    for g in range(NSA_KV_HEADS):
        xg = x_ref[:, g * HEAD_DIM:(g + 1) * HEAD_DIM]
        if n_valid < CMP_BLOCK:
            xg = jnp.where(l < n_valid, xg, 0.0)
        acc_ref[g] += _dot((xg + pe).astype(BF16), w1)

    @pl.when(l == CMP_BLOCK - 1)
    def _():
        w2 = w2_ref[...].astype(BF16)
        for g in range(NSA_KV_HEADS):
            a = acc_ref[g]
            hid = a * jax.nn.sigmoid(a)
            o_ref[:, g * HEAD_DIM:(g + 1) * HEAD_DIM] = _dot(hid.astype(BF16), w2)


def compress(x2d, col_block, n_valid, w1, w2, pe, tr, rows=None, row_block0=0):
    rows = x2d.shape[0] if rows is None else rows
    x_spec = pl.BlockSpec((tr, NSA_KV_W),
                          lambda i, l: (row_block0 + i, col_block(jnp.minimum(l, n_valid - 1))))
    return pl.pallas_call(
        functools.partial(_compress_kernel, n_valid=n_valid),
        grid=(rows // tr, CMP_BLOCK),
        in_specs=[
            x_spec,
            pl.BlockSpec((CMP_BLOCK, HEAD_DIM), lambda i, l: (0, 0)),
            pl.BlockSpec((HEAD_DIM, CMP_HIDDEN), lambda i, l: (l, 0)),
            pl.BlockSpec((CMP_HIDDEN, HEAD_DIM), lambda i, l: (0, 0)),
        ],
        out_specs=pl.BlockSpec((tr, NSA_KV_W), lambda i, l: (i, 0)),
        out_shape=jax.ShapeDtypeStruct((rows, NSA_KV_W), F32),
        scratch_shapes=[pltpu.VMEM((NSA_KV_HEADS, tr, CMP_HIDDEN), F32)],
        compiler_params=_cparams(("parallel", "arbitrary")),
        name="compress",
    )(x2d, pe, w1, w2)


def _compress_pool_kernel(x_ref, pe_ref, w1a_ref, w1b_ref, w2_ref, o_ref, acc_ref):
    lp = pl.program_id(1)
    tr, sub, _ = x_ref.shape
    half = sub // 2

    @pl.when(lp == 0)
    def _():
        acc_ref[...] = jnp.zeros(acc_ref.shape, F32)

    pe2 = pe_ref[pl.ds(pl.multiple_of(2 * lp, 2), 2), :]
    odd8 = lax.broadcasted_iota(jnp.int32, (sub, HEAD_DIM), 0) >= half
    pe8 = jnp.where(odd8, pe2[1:2, :], pe2[0:1, :])
    x = (x_ref[...] + pe8[None]).reshape(tr * sub, HEAD_DIM).astype(BF16)
    odd = lax.broadcasted_iota(jnp.int32, (tr * sub, HEAD_DIM), 0) % sub >= half
    zero = jnp.zeros_like(x)
    x2 = jnp.concatenate([jnp.where(odd, zero, x), jnp.where(odd, x, zero)], axis=1)
    w = jnp.concatenate([w1a_ref[...].astype(BF16), w1b_ref[...].astype(BF16)], axis=0)
    acc_ref[...] += _dot(x2, w)

    @pl.when(lp == CMP_BLOCK // 2 - 1)
    def _():
        a = acc_ref[...]
        a = a + pltpu.roll(a, half, 0)
        hid = a * jax.nn.sigmoid(a)
        o_ref[...] = _dot(hid.astype(BF16), w2_ref[...].astype(BF16)).reshape(tr, sub, HEAD_DIM)


def compress_pool(pool, w1, w2, pe, tr=512):
    n_blk = pool.shape[0] * PAGE_SIZE // CMP_BLOCK
    sub = 2 * NSA_KV_HEADS
    x = pool.reshape(n_blk, CMP_BLOCK * NSA_KV_HEADS, HEAD_DIM)
    out = pl.pallas_call(
        _compress_pool_kernel,
        grid=(n_blk // tr, CMP_BLOCK // 2),
        in_specs=[
            pl.BlockSpec((tr, sub, HEAD_DIM), lambda i, l: (i, l, 0)),
            pl.BlockSpec((CMP_BLOCK, HEAD_DIM), lambda i, l: (0, 0)),
            pl.BlockSpec((HEAD_DIM, CMP_HIDDEN), lambda i, l: (2 * l, 0)),
            pl.BlockSpec((HEAD_DIM, CMP_HIDDEN), lambda i, l: (2 * l + 1, 0)),
            pl.BlockSpec((CMP_HIDDEN, HEAD_DIM), lambda i, l: (0, 0)),
        ],
        out_specs=pl.BlockSpec((tr, sub, HEAD_DIM), lambda i, l: (i, 0, 0)),
        out_shape=jax.ShapeDtypeStruct((n_blk, sub, HEAD_DIM), F32),
        scratch_shapes=[pltpu.VMEM((tr * sub, CMP_HIDDEN), F32)],
        compiler_params=_cparams(("parallel", "arbitrary")),
        name="compress_pool",
    )(x, pe, w1, w1, w2)
    return out[:, NSA_KV_HEADS:, :].reshape(n_blk, NSA_KV_W)


def _nsa_prompt_kernel(rb_ref, q_ref, ks_ref, vs_ref, kw_ref, vw_ref, ck_ref, cv_ref, gl_ref, bg_ref,
                       qx_ref, mk_ref, mv_ref, o_ref, ox_ref, bt_ref, cb_ref, selx_ref, m_ref, l_ref, acc_ref):
    g = pl.program_id(1)
    qt = pl.program_id(2)
    q0 = qt * Q_TILE
    ii = lax.broadcasted_iota(jnp.int32, (Q_TILE, LANE), 0)
    jj = lax.broadcasted_iota(jnp.int32, (Q_TILE, LANE), 1)

    @pl.when(qt == 0)
    def _():
        d_new = (lax.broadcasted_iota(jnp.int32, (Q_TILE, 1), 0) + 1) % CMP_BLOCK
        for r in range(NSA_GROUP):
            h = NSA_GROUP * g + r
            bt_ref[r, 0] = _t5_bias(ii - jj, rb_ref, h)
            bt_ref[r, 1] = _t5_bias(LANE + ii - jj, rb_ref, h)
            cb_ref[r, 0] = _t5_bias(d_new, rb_ref, h)
            cb_ref[r, 1] = _t5_bias(d_new + CMP_BLOCK, rb_ref, h)

    def heads(f):
        return jnp.concatenate([f(r) for r in range(NSA_GROUP)], axis=0)

    qs = heads(lambda r: q_ref[:, r * HEAD_DIM:(r + 1) * HEAD_DIM]).astype(BF16)
    far_bias = heads(lambda r: jnp.full((Q_TILE, 1), rb_ref[N_BUCKETS - 1, NSA_GROUP * g + r], F32))
    diag_bias = heads(lambda r: bt_ref[r, 0])
    near_bias = heads(lambda r: bt_ref[r, 1])
    causal = jnp.concatenate([jj <= ii] * NSA_GROUP, axis=0)

    nb = ck_ref.shape[0]
    row = lax.broadcasted_iota(jnp.int32, (Q_TILE, nb), 0)
    col = lax.broadcasted_iota(jnp.int32, (Q_TILE, nb), 1)
    qpos = q0 + row
    dist = qpos - (col * CMP_BLOCK + (CMP_BLOCK - 1))
    sc = _dot_nt(qs, ck_ref[...].astype(BF16)) * ATTN_SCALE
    newest = (qpos - (CMP_BLOCK - 1)) // CMP_BLOCK
    probs = []
    for r in range(NSA_GROUP):
        bias = jnp.where(col == newest, cb_ref[r, 0],
                         jnp.where(col == newest - 1, cb_ref[r, 1], rb_ref[N_BUCKETS - 1, NSA_GROUP * g + r]))
        probs.append(_softmax_rows(sc[r * Q_TILE:(r + 1) * Q_TILE] + bias, dist >= 0))
    o_c = _dot(jnp.concatenate(probs, axis=0).astype(BF16), cv_ref[...].astype(BF16))
    imp = probs[0] + probs[1] + probs[2]
    sel = _select_blocks(imp, qpos // SLC_BLOCK, col, nb).astype(BF16)
    for t in range(selx_ref.shape[0]):
        kk = lax.broadcasted_iota(jnp.int32, (nb, LANE), 1) + t * LANE
        bj = lax.broadcasted_iota(jnp.int32, (nb, LANE), 0)
        expand = jnp.where(kk // SLC_BLOCK == bj, 1.0, 0.0).astype(BF16)
        selx_ref[t] = _dot(sel, expand)

    every = jnp.concatenate([jj >= 0] * NSA_GROUP, axis=0)

    def key_tile(ref, t):
        return ref[pl.ds(pl.multiple_of(jnp.maximum(t, 0) * LANE, LANE), LANE), :].astype(BF16)

    def scores(k_ref, t, bias):
        return _dot_nt(qs, key_tile(k_ref, t)) * ATTN_SCALE + bias

    def sel_mask(t):
        return jnp.concatenate([selx_ref[jnp.maximum(t, 0)] > 0.5] * NSA_GROUP, axis=0)

    def pair_update(s0, s1, m0, m1, v_ref, t0, t1):
        def pv(p):
            return _dot(p[:, :LANE], key_tile(v_ref, t0)) + _dot(p[:, LANE:], key_tile(v_ref, t1))
        _flash_update(m_ref, l_ref, acc_ref, jnp.concatenate([s0, s1], axis=1), pv,
                      jnp.concatenate([m0, m1], axis=1))

    m_ref[...] = jnp.full(m_ref.shape, NEG_INF, F32)
    l_ref[...] = jnp.zeros(l_ref.shape, F32)
    acc_ref[...] = jnp.zeros(acc_ref.shape, F32)
    n_far = jnp.maximum(qt - 1, 0)

    def far(it, carry):
        t0 = 2 * it
        t1 = jnp.minimum(t0 + 1, n_far - 1)
        pair_update(scores(ks_ref, t0, far_bias), scores(ks_ref, t1, far_bias),
                    sel_mask(t0), sel_mask(t1) & (t0 + 1 < n_far), vs_ref, t0, t1)
        return carry

    lax.fori_loop(0, (n_far + 1) // 2, far, 0)
    pair_update(scores(ks_ref, qt - 1, near_bias), scores(ks_ref, qt, diag_bias),
                sel_mask(qt - 1) & (qt >= 1), sel_mask(qt) & causal, vs_ref, qt - 1, qt)
    o_s = acc_ref[...] / l_ref[...]

    n_win = WINDOW // LANE
    upper = jnp.concatenate([jj > ii] * NSA_GROUP, axis=0)
    band_s, band_m = [], []
    for back in range(n_win, -1, -1):
        bias = diag_bias if back == 0 else near_bias if back == 1 else far_bias
        base = causal if back == 0 else upper if back == n_win else every
        band_s.append(scores(kw_ref, qt - back, bias))
        band_m.append(base & (qt >= back))
    pw = _softmax_rows(jnp.concatenate(band_s, axis=1), jnp.concatenate(band_m, axis=1)).astype(BF16)
    o_w = sum(_dot(pw[:, i * LANE:(i + 1) * LANE], key_tile(vw_ref, qt - back))
              for i, back in enumerate(range(n_win, -1, -1)))

    gate = jax.nn.sigmoid(gl_ref[...] + bg_ref[...])
    for r in range(NSA_GROUP):
        rows = slice(r * Q_TILE, (r + 1) * Q_TILE)
        o = (gate[:, 3 * r:3 * r + 1] * o_c[rows] + gate[:, 3 * r + 1:3 * r + 2] * o_s[rows]
             + gate[:, 3 * r + 2:3 * r + 3] * o_w[rows])
        o_ref[:, r * HEAD_DIM:(r + 1) * HEAD_DIM] = o.astype(o_ref.dtype)

    sx = _dot_nt(qx_ref[...].astype(BF16), mk_ref[...].astype(BF16)) * ATTN_SCALE
    px = _softmax_rows(sx, sx == sx)
    ox_ref[...] = _dot(px.astype(BF16), mv_ref[...].astype(BF16)).astype(ox_ref.dtype)


def nsa_prompt(proj, ck, cv, kvm, rel_bias, b_gate, batch):
    nq = SEQ // Q_TILE
    nb = SEQ // CMP_BLOCK
    hb = lambda c: c // HEAD_DIM
    qrow = lambda b, g, t: b * nq + t
    grid_spec = pltpu.PrefetchScalarGridSpec(
        num_scalar_prefetch=0,
        grid=(batch, NSA_KV_HEADS, nq),
        in_specs=[
            pl.BlockSpec(memory_space=pltpu.SMEM),
            pl.BlockSpec((Q_TILE, NSA_GROUP * HEAD_DIM), lambda b, g, t: (qrow(b, g, t), g)),
            pl.BlockSpec((SEQ, HEAD_DIM), lambda b, g, t: (b, hb(SELF_W + 2 * NSA_KV_W) + g)),
            pl.BlockSpec((SEQ, HEAD_DIM), lambda b, g, t: (b, hb(SELF_W + 3 * NSA_KV_W) + g)),
            pl.BlockSpec((SEQ, HEAD_DIM), lambda b, g, t: (b, hb(SELF_W + 4 * NSA_KV_W) + g)),
            pl.BlockSpec((SEQ, HEAD_DIM), lambda b, g, t: (b, hb(SELF_W + 5 * NSA_KV_W) + g)),
            pl.BlockSpec((nb, HEAD_DIM), lambda b, g, t: (b, g)),
            pl.BlockSpec((nb, HEAD_DIM), lambda b, g, t: (b, g)),
            pl.BlockSpec((Q_TILE, LANE), lambda b, g, t: (qrow(b, g, t), hb(COL_GATE) + g)),
            pl.BlockSpec((1, LANE), lambda b, g, t: (0, g)),
            pl.BlockSpec((Q_TILE, HEAD_DIM), lambda b, g, t: (qrow(b, g, t), hb(COL_QX) + g)),
            pl.BlockSpec((MEM_LEN, HEAD_DIM), lambda b, g, t: (b, g)),
            pl.BlockSpec((MEM_LEN, HEAD_DIM), lambda b, g, t: (b, N_CROSS_HEADS + g)),
        ],
        out_specs=[
            pl.BlockSpec((Q_TILE, NSA_GROUP * HEAD_DIM), lambda b, g, t: (qrow(b, g, t), g)),
            pl.BlockSpec((Q_TILE, HEAD_DIM), lambda b, g, t: (qrow(b, g, t), g)),
        ],
        scratch_shapes=[
            pltpu.VMEM((NSA_GROUP, 2, Q_TILE, LANE), F32),
            pltpu.VMEM((NSA_GROUP, 2, Q_TILE, 1), F32),
            pltpu.VMEM((SEQ // LANE, Q_TILE, LANE), F32),
            pltpu.VMEM((NSA_GROUP * Q_TILE, 1), F32),
            pltpu.VMEM((NSA_GROUP * Q_TILE, 1), F32),
            pltpu.VMEM((NSA_GROUP * Q_TILE, HEAD_DIM), F32),
        ],
    )
    return pl.pallas_call(
        _nsa_prompt_kernel,
        grid_spec=grid_spec,
        out_shape=[jax.ShapeDtypeStruct((batch * SEQ, SELF_W), BF16),
                   jax.ShapeDtypeStruct((batch * SEQ, CROSS_W), BF16)],
        compiler_params=_cparams(("arbitrary", "arbitrary", "arbitrary")),
        name="nsa_prompt",
    )(rel_bias, proj, proj, proj, proj, proj, ck, cv, proj, b_gate, proj, kvm, kvm)


FOX_TILE = 256
HEAD_PAD = 16


def _log_sigmoid(x):
    return jnp.minimum(x, 0.0) - jnp.log1p(jnp.exp(-jnp.abs(x)))


def _fox_prep_kernel(fl_ref, bf_ref, lf_ref, ct_ref):
    n_chunks = fl_ref.shape[0] // LANE
    ii = lax.broadcasted_iota(jnp.int32, (LANE, LANE), 0)
    jj = lax.broadcasted_iota(jnp.int32, (LANE, LANE), 1)
    tri = jnp.where(jj <= ii, 1.0, 0.0).astype(F32)
    carry = jnp.zeros((1, LANE), F32)
    per_tile = FOX_TILE // LANE
    for c in range(n_chunks):
        lf = _log_sigmoid(fl_ref[c * LANE:(c + 1) * LANE, :] + bf_ref[...])
        lf_ref[c * LANE:(c + 1) * LANE, :] = lf
        cs = jnp.dot(tri, lf, precision=lax.Precision.HIGHEST, preferred_element_type=F32) + carry
        carry = cs[LANE - 1:LANE, :]
        ct_ref[:, c // per_tile, (c % per_tile) * LANE:(c % per_tile + 1) * LANE] = cs.T[:HEAD_PAD]


def fox_prep(proj, b_forget, batch, seq_len=SEQ, row_block0=0):
    bf = jnp.pad(b_forget.reshape(1, FOX_HEADS), ((0, 0), (0, LANE - FOX_HEADS)))
    n_tiles = max(seq_len // FOX_TILE, 1)
    return pl.pallas_call(
        _fox_prep_kernel,
        grid=(batch,),
        in_specs=[pl.BlockSpec((seq_len, LANE), lambda b: (row_block0 + b, COL_GATE // LANE)),
                  pl.BlockSpec((1, LANE), lambda b: (0, 0))],
        out_specs=[pl.BlockSpec((seq_len, LANE), lambda b: (b, 0)),
                   pl.BlockSpec((None, HEAD_PAD, n_tiles, FOX_TILE), lambda b: (b, 0, 0, 0))],
        out_shape=[jax.ShapeDtypeStruct((batch * seq_len, LANE), F32),
                   jax.ShapeDtypeStruct((batch, HEAD_PAD, n_tiles, FOX_TILE), F32)],
        compiler_params=_cparams(("arbitrary",)),
        name="fox_prep",
    )(proj, bf)


def _fox_prompt_kernel(q_ref, k_ref, v_ref, c_ref, o_ref, m_ref, l_ref, acc_ref):
    qt = pl.program_id(2)
    q = q_ref[...].astype(BF16)
    fl = _Flash(m_ref, l_ref, acc_ref)
    fl.reset()

    def tile(t, mask):
        k0 = pl.multiple_of(t * FOX_TILE, FOX_TILE)
        k = k_ref[pl.ds(k0, FOX_TILE), :].astype(BF16)
        v = v_ref[pl.ds(k0, FOX_TILE), :].astype(BF16)
        fl.update(_dot_nt(q, k) * ATTN_SCALE - c_ref[pl.ds(t, 1), :], v, mask)

    def far(t, carry):
        tile(t, None)
        return carry

    lax.fori_loop(0, qt, far, 0)
    ii = lax.broadcasted_iota(jnp.int32, (FOX_TILE, FOX_TILE), 0)
    jj = lax.broadcasted_iota(jnp.int32, (FOX_TILE, FOX_TILE), 1)
    tile(qt, jj <= ii)
    o_ref[...] = fl.result().astype(o_ref.dtype)


def fox_prompt(proj, ct, batch):
    nq = SEQ // FOX_TILE
    hb = lambda c: c // HEAD_DIM
    return pl.pallas_call(
        _fox_prompt_kernel,
        grid=(batch, FOX_HEADS, nq),
        in_specs=[
            pl.BlockSpec((FOX_TILE, HEAD_DIM), lambda b, h, t: (b * nq + t, h)),
            pl.BlockSpec((SEQ, HEAD_DIM), lambda b, h, t: (b, hb(SELF_W) + h)),
            pl.BlockSpec((SEQ, HEAD_DIM), lambda b, h, t: (b, hb(2 * SELF_W) + h)),
            pl.BlockSpec((None, None, nq, FOX_TILE), lambda b, h, t: (b, h, 0, 0)),
        ],
        out_specs=pl.BlockSpec((FOX_TILE, HEAD_DIM), lambda b, h, t: (b * nq + t, h)),
        out_shape=jax.ShapeDtypeStruct((batch * SEQ, SELF_W), BF16),
        scratch_shapes=[pltpu.VMEM((FOX_TILE, 1), F32), pltpu.VMEM((FOX_TILE, 1), F32),
                        pltpu.VMEM((FOX_TILE, HEAD_DIM), F32)],
        compiler_params=_cparams(("arbitrary", "arbitrary", "arbitrary")),
        name="fox_prompt",
    )(proj, proj, proj, ct)


def _cross_prompt_kernel(q_ref, mk_ref, mv_ref, o_ref):
    for h in range(N_CROSS_HEADS):
        hs = slice(h * HEAD_DIM, (h + 1) * HEAD_DIM)
        s = _dot_nt(q_ref[:, hs].astype(BF16), mk_ref[:, hs].astype(BF16)) * ATTN_SCALE
        p = _softmax_rows(s, s == s)
        o_ref[:, hs] = _dot(p.astype(BF16), mv_ref[:, hs].astype(BF16)).astype(o_ref.dtype)


def cross_prompt(proj, kvm, batch, tq=512):
    nq = SEQ // tq
    return pl.pallas_call(
        _cross_prompt_kernel,
        grid=(batch, nq),
        in_specs=[pl.BlockSpec((tq, CROSS_W), lambda b, t: (b * nq + t, COL_QX // CROSS_W)),
                  pl.BlockSpec((MEM_LEN, CROSS_W), lambda b, t: (b, 0)),
                  pl.BlockSpec((MEM_LEN, CROSS_W), lambda b, t: (b, 1))],
        out_specs=pl.BlockSpec((tq, CROSS_W), lambda b, t: (b * nq + t, 0)),
        out_shape=jax.ShapeDtypeStruct((batch * SEQ, CROSS_W), BF16),
        compiler_params=_cparams(("arbitrary", "arbitrary")),
        name="cross_prompt",
    )(proj, kvm, kvm)


ROWS = 16


def _pad_rows(x):
    return jnp.pad(x, [(0, 0)] * (x.ndim - 2) + [(0, ROWS - x.shape[-2]), (0, 0)])


def _group_rows(q):
    b = q.shape[0]
    q = q.reshape(b, DEC_SEQ, NSA_KV_HEADS, NSA_GROUP, HEAD_DIM).transpose(0, 2, 3, 1, 4)
    return _pad_rows(q.reshape(b, NSA_KV_HEADS, NSA_GROUP * DEC_SEQ, HEAD_DIM))


def _ungroup_rows(o):
    b = o.shape[0]
    o = o[:, :, :NSA_GROUP * DEC_SEQ].reshape(b, NSA_KV_HEADS, NSA_GROUP, DEC_SEQ, HEAD_DIM)
    return o.transpose(0, 3, 1, 2, 4).reshape(b * DEC_SEQ, SELF_W)


def _gate_rows(gl):
    b = gl.shape[0]
    gl = gl.reshape(b, DEC_SEQ, NSA_KV_HEADS, LANE).transpose(0, 2, 1, 3)
    return _pad_rows(jnp.tile(gl, (1, 1, NSA_GROUP, 1)))


def _head_rows(q, n_heads):
    b = q.shape[0]
    return _pad_rows(q.reshape(b, DEC_SEQ, n_heads, HEAD_DIM).transpose(0, 2, 1, 3))


def _unhead_rows(o):
    b, h = o.shape[:2]
    return o[:, :, :DEC_SEQ].transpose(0, 2, 1, 3).reshape(b * DEC_SEQ, h * HEAD_DIM)


def _row_ids():
    row = lax.broadcasted_iota(jnp.int32, (ROWS, LANE), 0)
    return row, row % DEC_SEQ, row // DEC_SEQ


def _per_head(r_idx, f):
    return jnp.where(r_idx == 0, f(0), jnp.where(r_idx == 1, f(1), f(2)))


def _safe_div(acc, l):
    return acc / jnp.where(l > 0.0, l, 1.0)


def _merge_new_keys(fl, q32, kn, vn, bias, masks):
    s_new = [jnp.where(masks[j], jnp.sum(q32 * kn[j:j + 1, :], axis=-1, keepdims=True) * ATTN_SCALE
                       + bias[j], NEG_INF) for j in range(DEC_SEQ)]
    m_old = fl.m[...]
    m_new = m_old
    for s in s_new:
        m_new = jnp.maximum(m_new, s)
    alpha = jnp.exp(m_old - m_new)
    l = alpha * fl.l[...]
    acc = alpha * fl.acc[...]
    for j in range(DEC_SEQ):
        p = jnp.where(masks[j], jnp.exp(s_new[j] - m_new), 0.0)
        l = l + p
        acc = acc + p * vn[j:j + 1, :]
    return _safe_div(acc, l)


PAGES_PER_STEP = 4
SUB = 8


def _flash_update(m_ref, l_ref, acc_ref, s, pv_fn, mask=None):
    if mask is not None:
        s = jnp.where(mask, s, NEG_INF)
    m_old = m_ref[...]
    m_new = jnp.maximum(m_old, jnp.max(s, axis=-1, keepdims=True))
    alpha = jnp.exp(m_old - m_new)
    p = jnp.exp(s - m_new)
    if mask is not None:
        p = jnp.where(mask, p, 0.0)
    l_ref[...] = alpha * l_ref[...] + jnp.sum(p, axis=-1, keepdims=True)
    acc_ref[...] = alpha * acc_ref[...] + pv_fn(p.astype(BF16))
    m_ref[...] = m_new


def _cumsum_lanes_kernel(x_ref, o_ref):
    ii = lax.broadcasted_iota(jnp.int32, (LANE, LANE), 0)
    jj = lax.broadcasted_iota(jnp.int32, (LANE, LANE), 1)
    upper = jnp.where(ii <= jj, 1.0, 0.0).astype(F32)
    o_ref[...] = jnp.dot(x_ref[...], upper, precision=lax.Precision.HIGHEST, preferred_element_type=F32)


def cumsum_lanes(x, tr=512):
    h, rows, _ = x.shape
    spec = pl.BlockSpec((None, tr, LANE), lambda i, j: (i, j, 0))
    return pl.pallas_call(
        _cumsum_lanes_kernel, grid=(h, rows // tr), in_specs=[spec], out_specs=spec,
        out_shape=jax.ShapeDtypeStruct(x.shape, F32),
        compiler_params=_cparams(("arbitrary", "arbitrary")), name="cumsum_lanes",
    )(x)


def _fox_sample_kernel(pt_ref, q_ref, kn_ref, vn_ref, *refs):
    n = PAGES_PER_STEP
    k_refs, v_refs, c_refs = refs[:n], refs[n:2 * n], refs[2 * n:3 * n]
    lfn_ref, o_ref, m_ref, l_ref, acc_ref, run_ref = refs[3 * n:]
    b = pl.program_id(0)
    step = pl.program_id(1)

    @pl.when(step == 0)
    def _():
        m_ref[...] = jnp.full(m_ref.shape, NEG_INF, F32)
        l_ref[...] = jnp.zeros(l_ref.shape, F32)
        acc_ref[...] = jnp.zeros(acc_ref.shape, F32)
        run_ref[...] = jnp.zeros(run_ref.shape, F32)

    sub = [pt_ref[b, step * n + j] % SUB for j in range(n)]
    s_heads = []
    for h in range(FOX_HEADS):
        q = q_ref[h].astype(BF16)
        run = run_ref[h:h + 1, :]
        parts = []
        for j in range(n):
            ck = run + c_refs[j][h, pl.ds(sub[j], 1), :]
            parts.append(_dot_nt(q, k_refs[j][h].astype(BF16)) * ATTN_SCALE - ck)
            run = jnp.broadcast_to(ck[:, PAGE_SIZE - 1:PAGE_SIZE], (1, LANE))
        run_ref[h:h + 1, :] = run
        s_heads.append(jnp.concatenate(parts, axis=1))
    s = jnp.concatenate(s_heads, axis=0)

    def pv(p):
        return jnp.concatenate(
            [sum(_dot(p[h * ROWS:(h + 1) * ROWS, j * PAGE_SIZE:(j + 1) * PAGE_SIZE], v_refs[j][h].astype(BF16))
                 for j in range(n)) for h in range(FOX_HEADS)], axis=0)

    _flash_update(m_ref, l_ref, acc_ref, s, pv)

    @pl.when(step == N_PAGES // n - 1)
    def _():
        row = lax.broadcasted_iota(jnp.int32, (ROWS, 1), 0)
        lfn = lfn_ref[...]
        for h in range(FOX_HEADS):
            hs = slice(h * HEAD_DIM, (h + 1) * HEAD_DIM)
            rs = slice(h * ROWS, (h + 1) * ROWS)
            c, c_new = run_ref[h:h + 1, 0:1], []
            for j in range(DEC_SEQ):
                c = c + lfn[h:h + 1, j:j + 1]
                c_new.append(-c)
            fl = _Flash(m_ref.at[rs], l_ref.at[rs], acc_ref.at[rs])
            o_ref[h] = _merge_new_keys(fl, q_ref[h], kn_ref[:, hs], vn_ref[:, hs], c_new,
                                       [row >= j for j in range(DEC_SEQ)])


def fox_sample(q16, ps3, pool_k, pool_v, ct_pool, lfn, page_table):
    nb = q16.shape[0]
    n = PAGES_PER_STEP
    page = lambda j: pl.BlockSpec((None, FOX_HEADS, PAGE_SIZE, HEAD_DIM),
                                  lambda b, s, pt: (pt[b, s * n + j], 0, 0, 0))
    cpage = lambda j: pl.BlockSpec((FOX_HEADS, SUB, PAGE_SIZE), lambda b, s, pt: (0, pt[b, s * n + j] // SUB, 0))
    rows = FOX_HEADS * ROWS
    grid_spec = pltpu.PrefetchScalarGridSpec(
        num_scalar_prefetch=1,
        grid=(nb, N_PAGES // n),
        in_specs=[
            pl.BlockSpec((None, FOX_HEADS, ROWS, HEAD_DIM), lambda b, s, pt: (b, 0, 0, 0)),
            pl.BlockSpec((None, DEC_SEQ, SELF_W), lambda b, s, pt: (b, 0, 1)),
            pl.BlockSpec((None, DEC_SEQ, SELF_W), lambda b, s, pt: (b, 0, 2)),
        ] + [page(j) for j in range(n)] + [page(j) for j in range(n)] + [cpage(j) for j in range(n)] + [
            pl.BlockSpec((None, HEAD_PAD, DEC_SEQ), lambda b, s, pt: (b, 0, 0)),
        ],
        out_specs=pl.BlockSpec((None, FOX_HEADS, ROWS, HEAD_DIM), lambda b, s, pt: (b, 0, 0, 0)),
        scratch_shapes=[pltpu.VMEM((rows, 1), F32), pltpu.VMEM((rows, 1), F32),
                        pltpu.VMEM((rows, HEAD_DIM), F32), pltpu.VMEM((HEAD_PAD, LANE), F32)],
    )
    return pl.pallas_call(
        _fox_sample_kernel,
        grid_spec=grid_spec,
        out_shape=jax.ShapeDtypeStruct((nb, FOX_HEADS, ROWS, HEAD_DIM), F32),
        compiler_params=_cparams(("arbitrary", "arbitrary")),
        name="fox_sample",
    )(page_table, q16, ps3, ps3, *([pool_k] * n), *([pool_v] * n), *([ct_pool] * n), lfn)


N_SEL_TILES = N_PAGES + 1


def _nsa_sample_cmp_kernel(pt_ref, rb_ref, q_ref, *refs):
    ck_pages, cv_pages = refs[:N_PAGES], refs[N_PAGES:2 * N_PAGES]
    ckn_ref, cvn_ref, oc_ref, sx_ref, ck_s, cv_s = refs[2 * N_PAGES:]
    b = pl.program_id(0)
    blk_per_page = PAGE_SIZE // CMP_BLOCK

    @pl.when(b == 0)
    def _():
        ck_s[...] = jnp.zeros(ck_s.shape, F32)
        cv_s[...] = jnp.zeros(cv_s.shape, F32)

    for p in range(N_PAGES):
        ck_s[p * blk_per_page:(p + 1) * blk_per_page, :] = ck_pages[p][...]
        cv_s[p * blk_per_page:(p + 1) * blk_per_page, :] = cv_pages[p][...]
    ck_s[N_PAST_BLK:N_PAST_BLK + 1, :] = ckn_ref[...]
    cv_s[N_PAST_BLK:N_PAST_BLK + 1, :] = cvn_ref[...]

    row, tok, r_idx = _row_ids()
    lane = lax.broadcasted_iota(jnp.int32, (ROWS, LANE), 1)
    qpos = PAST_LEN + tok
    dist = qpos - (lane * CMP_BLOCK + (CMP_BLOCK - 1))
    valid = (dist >= 0) & (lane <= N_PAST_BLK)
    for g in range(NSA_KV_HEADS):
        gs = slice(g * HEAD_DIM, (g + 1) * HEAD_DIM)
        s = _dot_nt(q_ref[g].astype(BF16), ck_s[:, gs].astype(BF16)) * ATTN_SCALE
        s = s + _per_head(r_idx, lambda r: _t5_bias(dist, rb_ref, NSA_GROUP * g + r))
        p = _softmax_rows(s, valid)
        oc_ref[g] = _dot(p.astype(BF16), cv_s[:, gs].astype(BF16))
        p0 = jnp.where(row < NSA_GROUP * DEC_SEQ, p, 0.0)
        imp = p0 + pltpu.roll(p0, ROWS - DEC_SEQ, 0) + pltpu.roll(p0, ROWS - 2 * DEC_SEQ, 0)
        imp = jnp.where(row < DEC_SEQ, imp, 0.0)
        imp = imp + pltpu.roll(imp, DEC_SEQ, 0) + pltpu.roll(imp, 2 * DEC_SEQ, 0)
        sel = _select_blocks(imp, qpos // SLC_BLOCK, lane, N_PAST_BLK + 1).astype(BF16)
        for t in range(N_SEL_TILES):
            kk = lax.broadcasted_iota(jnp.int32, (LANE, LANE), 1) + t * LANE
            bj = lax.broadcasted_iota(jnp.int32, (LANE, LANE), 0)
            expand = jnp.where(kk // SLC_BLOCK == bj, 1.0, 0.0).astype(BF16)
            sx_ref[g, :, t * LANE:(t + 1) * LANE] = _dot(sel, expand)


def nsa_sample_cmp(qg16, ck_pool, cv_pool, ck_new, cv_new, rel_bias, page_table):
    nb = qg16.shape[0]
    page = lambda p: pl.BlockSpec((None, PAGE_SIZE // CMP_BLOCK, NSA_KV_W), lambda b, pt: (pt[b, p], 0, 0))
    grid_spec = pltpu.PrefetchScalarGridSpec(
        num_scalar_prefetch=1,
        grid=(nb,),
        in_specs=[pl.BlockSpec(memory_space=pltpu.SMEM),
                  pl.BlockSpec((None, NSA_KV_HEADS, ROWS, HEAD_DIM), lambda b, pt: (b, 0, 0, 0))]
        + [page(p) for p in range(N_PAGES)] + [page(p) for p in range(N_PAGES)]
        + [pl.BlockSpec((None, 1, NSA_KV_W), lambda b, pt: (b, 0, 0))] * 2,
        out_specs=[pl.BlockSpec((None, NSA_KV_HEADS, ROWS, HEAD_DIM), lambda b, pt: (b, 0, 0, 0)),
                   pl.BlockSpec((None, NSA_KV_HEADS, ROWS, N_SEL_TILES * LANE), lambda b, pt: (b, 0, 0, 0))],
        scratch_shapes=[pltpu.VMEM((LANE, NSA_KV_W), F32), pltpu.VMEM((LANE, NSA_KV_W), F32)],
    )
    return pl.pallas_call(
        _nsa_sample_cmp_kernel,
        grid_spec=grid_spec,
        out_shape=[jax.ShapeDtypeStruct((nb, NSA_KV_HEADS, ROWS, HEAD_DIM), F32),
                   jax.ShapeDtypeStruct((nb, NSA_KV_HEADS, ROWS, N_SEL_TILES * LANE), F32)],
        compiler_params=_cparams(("arbitrary",)),
        name="nsa_sample_cmp",
    )(page_table, rel_bias, qg16, *([ck_pool] * N_PAGES), *([cv_pool] * N_PAGES), ck_new, cv_new)


def _nsa_sample_kernel(pt_ref, rb_ref, q_ref, *refs):
    n = PAGES_PER_STEP
    n_steps = N_PAGES // n
    k_refs, v_refs = refs[:n], refs[n:2 * n]
    (sx_ref, sxn_ref, kn_ref, vn_ref, wk_ref, wv_ref, wkn_ref, wvn_ref, oc_ref, gl_ref, bg_ref, o_ref,
     m_ref, l_ref, acc_ref, ow_ref, b15_ref, bn_ref) = refs[2 * n:]
    b = pl.program_id(0)
    p = pl.program_id(1)
    ng = NSA_KV_HEADS
    rows = ng * ROWS
    page_w = PAGE_SIZE * ng
    row, tok, r_idx = _row_ids()
    lane = lax.broadcasted_iota(jnp.int32, (ROWS, LANE), 1)
    tok1, r1 = tok[:, 0:1], r_idx[:, 0:1]
    new_vis = [tok1 >= j for j in range(DEC_SEQ)]
    q_all = q_ref[...].reshape(rows, HEAD_DIM).astype(BF16)
    grp_of_row = lax.broadcasted_iota(jnp.int32, (rows, 1), 0) // ROWS
    tok_of_row = lax.broadcasted_iota(jnp.int32, (rows, 1), 0) % DEC_SEQ

    def stack(f):
        return jnp.concatenate([f(g) for g in range(ng)], axis=0)

    far_bias = stack(lambda g: _per_head(r1, lambda r: rb_ref[N_BUCKETS - 1, NSA_GROUP * g + r]))

    def own_group(width):
        return lax.broadcasted_iota(jnp.int32, (rows, width), 1) % ng == grp_of_row

    @pl.when((b == 0) & (p == 0))
    def _():
        c_tok = lax.broadcasted_iota(jnp.int32, (ROWS, page_w), 1) // ng
        t_row = lax.broadcasted_iota(jnp.int32, (ROWS, page_w), 0) % DEC_SEQ
        r_row = lax.broadcasted_iota(jnp.int32, (ROWS, page_w), 0) // DEC_SEQ
        for g in range(ng):
            b15_ref[g * ROWS:(g + 1) * ROWS, :] = _per_head(
                r_row, lambda r: _t5_bias(LANE + t_row - c_tok, rb_ref, NSA_GROUP * g + r))
            bn_ref[g] = _per_head(r_idx, lambda r: _t5_bias(tok - lane, rb_ref, NSA_GROUP * g + r))

    def new_scores(kn):
        return [stack(lambda g: jnp.sum(q_ref[g] * kn[j:j + 1, g * HEAD_DIM:(g + 1) * HEAD_DIM], axis=-1,
                                        keepdims=True) * ATTN_SCALE + bn_ref[g][:, j:j + 1])
                for j in range(DEC_SEQ)]

    def new_values(vn, j):
        return stack(lambda g: jnp.broadcast_to(vn[j:j + 1, g * HEAD_DIM:(g + 1) * HEAD_DIM], (ROWS, HEAD_DIM)))

    @pl.when(p == 0)
    def _():
        m_ref[...] = jnp.full(m_ref.shape, NEG_INF, F32)
        l_ref[...] = jnp.zeros(l_ref.shape, F32)
        acc_ref[...] = jnp.zeros(acc_ref.shape, F32)
        width = WINDOW * ng
        s = _dot_nt(q_all, wk_ref[...].astype(BF16)) * ATTN_SCALE
        bias = jnp.concatenate([jnp.broadcast_to(far_bias, (rows, width - page_w)), b15_ref[...]], axis=1)
        c_tok = lax.broadcasted_iota(jnp.int32, (rows, width), 1) // ng
        in_win = own_group(width) & (c_tok > tok_of_row)
        s = jnp.where(in_win, s + bias, NEG_INF)
        vis = [tok_of_row >= j for j in range(DEC_SEQ)]
        s_new = [jnp.where(vis[j], sn, NEG_INF) for j, sn in enumerate(new_scores(wkn_ref))]
        mx = jnp.max(s, axis=-1, keepdims=True)
        for sn in s_new:
            mx = jnp.maximum(mx, sn)
        e = jnp.where(in_win, jnp.exp(s - mx), 0.0)
        den = jnp.sum(e, axis=-1, keepdims=True)
        acc = _dot(e.astype(BF16), wv_ref[...].astype(BF16))
        for j in range(DEC_SEQ):
            pj = jnp.where(vis[j], jnp.exp(s_new[j] - mx), 0.0)
            den = den + pj
            acc = acc + pj * new_values(wvn_ref, j)
        ow_ref[...] = _safe_div(acc, den)

    tt = lax.broadcasted_iota(jnp.int32, (PAGE_SIZE, page_w), 0)
    cc = lax.broadcasted_iota(jnp.int32, (PAGE_SIZE, page_w), 1)
    per_token = jnp.where(cc // ng == tt, 1.0, 0.0).astype(BF16)
    sel = sx_ref[...].reshape(rows, n * PAGE_SIZE).astype(BF16)
    own = own_group(page_w)
    s_parts, m_parts = [], []
    for j in range(n):
        bias = far_bias
        if j == n - 1:
            bias = jnp.where(p == n_steps - 1, b15_ref[...], bias)
        s_parts.append(_dot_nt(q_all, k_refs[j][...].astype(BF16)) * ATTN_SCALE + bias)
        m_parts.append((_dot(sel[:, j * PAGE_SIZE:(j + 1) * PAGE_SIZE], per_token) > 0.5) & own)

    def pv(pr):
        return sum(_dot(pr[:, j * page_w:(j + 1) * page_w], v_refs[j][...].astype(BF16)) for j in range(n))

    _flash_update(m_ref, l_ref, acc_ref, jnp.concatenate(s_parts, axis=1), pv, jnp.concatenate(m_parts, axis=1))

    @pl.when(p == n_steps - 1)
    def _():
        gate = jax.nn.sigmoid(gl_ref[...] + bg_ref[...].reshape(ng, 1, LANE))
        picked = stack(lambda g: sxn_ref[g][:, 0:1]) > 0.5
        vis = [(tok_of_row >= j) & picked for j in range(DEC_SEQ)]
        s_new = [jnp.where(vis[j], sn, NEG_INF) for j, sn in enumerate(new_scores(kn_ref))]
        m_old = m_ref[...]
        m_new = m_old
        for sn in s_new:
            m_new = jnp.maximum(m_new, sn)
        alpha = jnp.exp(m_old - m_new)
        den = alpha * l_ref[...]
        acc = alpha * acc_ref[...]
        for j in range(DEC_SEQ):
            pj = jnp.where(vis[j], jnp.exp(s_new[j] - m_new), 0.0)
            den = den + pj
            acc = acc + pj * new_values(vn_ref, j)
        o_s = _safe_div(acc, den)
        o_w = ow_ref[...]
        for g in range(ng):
            rs = slice(g * ROWS, (g + 1) * ROWS)
            gt = [_per_head(r1, lambda r: gate[g][:, 3 * r + c:3 * r + c + 1]) for c in range(3)]
            o_ref[g] = gt[0] * oc_ref[g] + gt[1] * o_s[rs] + gt[2] * o_w[rs]


def nsa_sample(qg16, ps3, pool_k, pool_v, selx, buf_k, buf_v, o_c, gl16, b_gate, rel_bias, page_table):
    nb = qg16.shape[0]
    n = PAGES_PER_STEP
    cb = lambda c: c // NSA_KV_W
    qspec = pl.BlockSpec((None, NSA_KV_HEADS, ROWS, HEAD_DIM), lambda b, p, pt: (b, 0, 0, 0))
    new = lambda c: pl.BlockSpec((None, DEC_SEQ, NSA_KV_W), lambda b, p, pt: (b, 0, cb(c)))
    page_w = PAGE_SIZE * NSA_KV_HEADS
    pool_k = pool_k.reshape(-1, page_w, HEAD_DIM)
    pool_v = pool_v.reshape(-1, page_w, HEAD_DIM)
    buf_k = buf_k.reshape(nb, WINDOW * NSA_KV_HEADS, HEAD_DIM)
    buf_v = buf_v.reshape(nb, WINDOW * NSA_KV_HEADS, HEAD_DIM)
    page = lambda j: pl.BlockSpec((None, page_w, HEAD_DIM), lambda b, p, pt: (pt[b, p * n + j], 0, 0))
    wspec = pl.BlockSpec((None, WINDOW * NSA_KV_HEADS, HEAD_DIM), lambda b, p, pt: (b, 0, 0))
    rows = NSA_KV_HEADS * ROWS
    grid_spec = pltpu.PrefetchScalarGridSpec(
        num_scalar_prefetch=1,
        grid=(nb, N_PAGES // n),
        in_specs=[pl.BlockSpec(memory_space=pltpu.SMEM), qspec]
        + [page(j) for j in range(n)] + [page(j) for j in range(n)] + [
            pl.BlockSpec((None, NSA_KV_HEADS, ROWS, n * LANE), lambda b, p, pt: (b, 0, 0, p)),
            pl.BlockSpec((None, NSA_KV_HEADS, ROWS, LANE), lambda b, p, pt: (b, 0, 0, N_PAGES)),
            new(SELF_W + 2 * NSA_KV_W), new(SELF_W + 3 * NSA_KV_W),
            wspec, wspec,
            new(SELF_W + 4 * NSA_KV_W), new(SELF_W + 5 * NSA_KV_W),
            qspec, qspec,
            pl.BlockSpec((1, CROSS_W), lambda b, p, pt: (0, 0)),
        ],
        out_specs=qspec,
        scratch_shapes=[pltpu.VMEM((rows, 1), F32), pltpu.VMEM((rows, 1), F32),
                        pltpu.VMEM((rows, HEAD_DIM), F32),
                        pltpu.VMEM((rows, HEAD_DIM), F32),
                        pltpu.VMEM((rows, page_w), F32),
                        pltpu.VMEM((NSA_KV_HEADS, ROWS, LANE), F32)],
    )
    return pl.pallas_call(
        _nsa_sample_kernel,
        grid_spec=grid_spec,
        out_shape=jax.ShapeDtypeStruct((nb, NSA_KV_HEADS, ROWS, HEAD_DIM), F32),
        compiler_params=_cparams(("arbitrary", "arbitrary")),
        name="nsa_sample",
    )(page_table, rel_bias, qg16, *([pool_k] * n), *([pool_v] * n), selx, selx, ps3, ps3, buf_k, buf_v,
      ps3, ps3, o_c, gl16, b_gate)


def _cross_sample_kernel(q_ref, mk_ref, mv_ref, o_ref):
    nh = N_CROSS_HEADS
    rows, width = nh * ROWS, MEM_LEN * nh
    q = q_ref[...].reshape(rows, HEAD_DIM).astype(BF16)
    s = _dot_nt(q, mk_ref[...].astype(BF16)) * ATTN_SCALE
    own = (lax.broadcasted_iota(jnp.int32, (rows, width), 1) % nh
           == lax.broadcasted_iota(jnp.int32, (rows, width), 0) // ROWS)
    p = _softmax_rows(s, own)
    o_ref[...] = _dot(p.astype(BF16), mv_ref[...].astype(BF16)).reshape(nh, ROWS, HEAD_DIM)


def cross_sample(qx16, mem_k, mem_v, layer):
    nb = qx16.shape[0]
    n_layers = mem_k.shape[0]
    mem_k = mem_k.reshape(n_layers, nb, MEM_LEN * N_CROSS_HEADS, HEAD_DIM)
    mem_v = mem_v.reshape(n_layers, nb, MEM_LEN * N_CROSS_HEADS, HEAD_DIM)
    qspec = pl.BlockSpec((None, N_CROSS_HEADS, ROWS, HEAD_DIM), lambda b: (b, 0, 0, 0))
    mspec = pl.BlockSpec((None, None, MEM_LEN * N_CROSS_HEADS, HEAD_DIM), lambda b: (layer, b, 0, 0))
    return pl.pallas_call(
        _cross_sample_kernel,
        grid=(nb,),
        in_specs=[qspec, mspec, mspec],
        out_specs=qspec,
        out_shape=jax.ShapeDtypeStruct((nb, N_CROSS_HEADS, ROWS, HEAD_DIM), F32),
        compiler_params=_cparams(("arbitrary",)),
        name="cross_sample",
    )(qx16, mem_k, mem_v)


def _arrange_in_weights(w, n_gate, per_group):
    gate = w[:, COL_QX:COL_QX + n_gate]
    if per_group:
        gate = jnp.pad(gate.reshape(-1, NSA_KV_HEADS, n_gate // NSA_KV_HEADS),
                       ((0, 0), (0, 0), (0, LANE - n_gate // NSA_KV_HEADS))).reshape(-1, CROSS_W)
    else:
        gate = jnp.pad(gate, ((0, 0), (0, CROSS_W - n_gate)))
    return jnp.concatenate([w[:, COL_QX + n_gate:], gate], axis=1)


def kernel(x_prompt, x_sample, mem_prompt, cache_nsa_cmp_k, cache_nsa_cmp_v, cache_nsa_slc_k, cache_nsa_slc_v, cache_nsa_win_k, cache_nsa_win_v, cache_fox_k, cache_fox_v, cache_fox_logf, cache_mem_k, cache_mem_v, page_table, rel_bias, norm_g, mem_norm_g, w_mem_kv, w_ff_gu, w_ff_down, w_in_nsa, b_gate_nsa, w_cmp1, w_cmp2, cmp_pe, w_out_nsa, w_in_fox, b_forget, w_out_fox, final_norm_g):
    n_p = BATCH * SEQ
    x = jnp.concatenate([x_prompt.reshape(n_p, D_MODEL), x_sample.reshape(-1, D_MODEL)], axis=0)
    mem = mem_prompt.reshape(BATCH * MEM_LEN, D_MODEL)
    n_pool = cache_nsa_cmp_k.shape[1]
    kv_heads = (BATCH, SEQ, NSA_KV_HEADS, HEAD_DIM)
    w_ff_gu = w_ff_gu.astype(BF16)
    w_ff_down = w_ff_down.astype(BF16)
    out = {}
    for i in range(DEPTH):
        x = ffn(x, norm_g[i, 0], w_ff_gu, w_ff_down, (i, 0))
        kvm = linear(mem, w_mem_kv[i], g=mem_norm_g[i], tm=512)
        out[f"mem_k{i}"] = kvm[:, :CROSS_W].reshape(BATCH, MEM_LEN, N_CROSS_HEADS, HEAD_DIM)
        out[f"mem_v{i}"] = kvm[:, CROSS_W:].reshape(BATCH, MEM_LEN, N_CROSS_HEADS, HEAD_DIM)
        a = i // 2
        if i % 2 == 0:
            w_tail = _arrange_in_weights(w_in_nsa[a], 3 * N_SELF_HEADS, True)
            proj = linear(x, w_in_nsa[a], w_tail, COL_QX // 512, g=norm_g[i, 1])
            ps3 = proj[n_p:].reshape(DEC_BATCH, DEC_SEQ, PROJ_W)
            bg = jnp.pad(b_gate_nsa[a].reshape(NSA_KV_HEADS, 3 * NSA_GROUP),
                         ((0, 0), (0, LANE - 3 * NSA_GROUP))).reshape(1, CROSS_W)
            cmp_w = [(w_cmp1[a, c], w_cmp2[a, c], cmp_pe[a, c]) for c in range(2)]
            kvc = proj[:, SELF_W:SELF_W + 2 * NSA_KV_W]
            x_blk = kvc.reshape(N_TOK // CMP_BLOCK, CMP_BLOCK * 2 * NSA_KV_W)
            x_new = kvc.reshape(N_TOK // DEC_SEQ, DEC_SEQ * 2 * NSA_KV_W)
            pools = [cache_nsa_cmp_k[a], cache_nsa_cmp_v[a]]
            c_prompt, c_pool, c_new = [], [], []
            for c in range(2):
                col = lambda l, c=c: 2 * l + c
                c_prompt.append(compress(x_blk, col, CMP_BLOCK, *cmp_w[c], tr=BATCH * SEQ // CMP_BLOCK,
                                         rows=BATCH * SEQ // CMP_BLOCK))
                c_pool.append(compress_pool(pools[c], *cmp_w[c])
                              .reshape(n_pool, PAGE_SIZE // CMP_BLOCK, NSA_KV_W))
                c_new.append(compress(x_new, col, DEC_SEQ, *cmp_w[c], tr=DEC_BATCH, rows=DEC_BATCH,
                                      row_block0=n_p // DEC_SEQ // DEC_BATCH).reshape(DEC_BATCH, 1, NSA_KV_W))
            o_p, ox_p = nsa_prompt(proj, c_prompt[0], c_prompt[1], kvm, rel_bias, bg, BATCH)
            qg16 = _group_rows(ps3[:, :, :SELF_W])
            o_c, selx = nsa_sample_cmp(qg16, c_pool[0], c_pool[1], c_new[0], c_new[1], rel_bias, page_table)
            o_s = _ungroup_rows(nsa_sample(
                qg16, ps3, cache_nsa_slc_k[a], cache_nsa_slc_v[a], selx,
                cache_nsa_win_k[a], cache_nsa_win_v[a],
                o_c, _gate_rows(ps3[:, :, COL_GATE:]), bg, rel_bias, page_table))
            for j, name in enumerate(("cmp_k", "cmp_v", "slc_k", "slc_v", "win_k", "win_v")):
                cols = slice(SELF_W + j * NSA_KV_W, SELF_W + (j + 1) * NSA_KV_W)
                st_p = proj[:n_p, cols].reshape(kv_heads)
                st_s = ps3[:, :, cols].reshape(DEC_BATCH, DEC_SEQ, NSA_KV_HEADS, HEAD_DIM)
                if name.startswith("win"):
                    buf = (cache_nsa_win_k if name == "win_k" else cache_nsa_win_v)[a]
                    st_p = st_p[:, -WINDOW:]
                    st_s = jnp.concatenate([buf[:, DEC_SEQ:], st_s], axis=1)
                out.setdefault("p_" + name, []).append(st_p)
                out.setdefault("s_" + name, []).append(st_s)
            w_out = w_out_nsa[a]
        else:
            w_tail = _arrange_in_weights(w_in_fox[a], FOX_HEADS, False)
            proj = linear(x, w_in_fox[a], w_tail, COL_QX // 512, g=norm_g[i, 1])
            ps3 = proj[n_p:].reshape(DEC_BATCH, DEC_SEQ, PROJ_W)
            lf_p, ct = fox_prep(proj, b_forget[a], BATCH)
            n_s = DEC_BATCH * DEC_SEQ
            lf_s, _ = fox_prep(proj, b_forget[a], 1, seq_len=n_s, row_block0=n_p // n_s)
            o_p = fox_prompt(proj, ct, BATCH)
            ox_p = cross_prompt(proj, kvm, BATCH)
            ct_pool = cumsum_lanes(jnp.transpose(cache_fox_logf[a], (2, 0, 1)))
            lfn = lf_s[:, :HEAD_PAD].reshape(DEC_BATCH, DEC_SEQ, HEAD_PAD).transpose(0, 2, 1)
            o_s = _unhead_rows(fox_sample(
                _head_rows(ps3[:, :, :SELF_W], FOX_HEADS), ps3,
                jnp.transpose(cache_fox_k[a], (0, 2, 1, 3)), jnp.transpose(cache_fox_v[a], (0, 2, 1, 3)),
                ct_pool, lfn, page_table))
            heads = (FOX_HEADS, HEAD_DIM)
            for j, name in enumerate(("fox_k", "fox_v")):
                cols = slice((j + 1) * SELF_W, (j + 2) * SELF_W)
                out.setdefault("p_" + name, []).append(proj[:n_p, cols].reshape(BATCH, SEQ, *heads))
                out.setdefault("s_" + name, []).append(ps3[:, :, cols].reshape(DEC_BATCH, DEC_SEQ, *heads))
            out.setdefault("p_fox_logf", []).append(lf_p[:, :FOX_HEADS].reshape(BATCH, SEQ, FOX_HEADS))
            out.setdefault("s_fox_logf", []).append(lf_s[:, :FOX_HEADS].reshape(DEC_BATCH, DEC_SEQ, FOX_HEADS))
            w_out = w_out_fox[a]
        ox_s = _unhead_rows(cross_sample(
            _head_rows(ps3[:, :, COL_QX:COL_GATE], N_CROSS_HEADS), cache_mem_k, cache_mem_v, i))
        mix_self = jnp.concatenate([o_p, o_s.astype(BF16)], axis=0)
        mix_cross = jnp.concatenate([ox_p, ox_s.astype(BF16)], axis=0)
        x = linear([mix_self, mix_cross], w_out, res=x)
        x = ffn(x, norm_g[i, 2], w_ff_gu, w_ff_down, (i, 1),
                final_g=final_norm_g if i == DEPTH - 1 else None)
    y_prompt = x[:n_p].reshape(BATCH, SEQ, D_MODEL)
    y_sample = x[n_p:].reshape(DEC_BATCH, DEC_SEQ, D_MODEL)
    st = lambda name: jnp.stack(out[name])
    p_mem_k = jnp.stack([out[f"mem_k{i}"] for i in range(DEPTH)])
    p_mem_v = jnp.stack([out[f"mem_v{i}"] for i in range(DEPTH)])
    return (y_prompt, y_sample,
            st("p_cmp_k"), st("p_cmp_v"), st("p_slc_k"), st("p_slc_v"), st("p_win_k"), st("p_win_v"),
            st("p_fox_k"), st("p_fox_v"), st("p_fox_logf"), p_mem_k, p_mem_v,
            st("s_cmp_k"), st("s_cmp_v"), st("s_slc_k"), st("s_slc_v"), st("s_win_k"), st("s_win_v"),
            st("s_fox_k"), st("s_fox_v"), st("s_fox_logf"))
```

```python
import functools
import math

import jax
import jax.numpy as jnp
from jax import lax
from jax.experimental import pallas as pl
from jax.experimental.pallas import tpu as pltpu

D_MODEL = 2048
BATCH = 4
SEQ = 2048
DEPTH = 2
DEC_BATCH = 128
DEC_SEQ = 4
PAST_LEN = 2048
PAGE_SIZE = 128
HEAD_DIM = 128
N_CROSS_HEADS = 4
N_SELF_HEADS = 12
NSA_KV_HEADS = 4
NSA_GROUP = N_SELF_HEADS // NSA_KV_HEADS
CMP_BLOCK = 64
SLC_BLOCK = 64
N_SEL = 16
WINDOW = 512
CMP_HIDDEN = 256
FOX_HEADS = N_SELF_HEADS
MEM_LEN = 256
N_BUCKETS = 32
MAX_DISTANCE = 128
FFN_DIM = ((8 * D_MODEL) // 3 + 127) // 128 * 128
RMS_EPS = 1e-6
FORCE_SCORE = 1e4
NEG_INF = -1e30
ATTN_SCALE = HEAD_DIM ** -0.5
SELF_W = N_SELF_HEADS * HEAD_DIM
NSA_KV_W = NSA_KV_HEADS * HEAD_DIM
CROSS_W = N_CROSS_HEADS * HEAD_DIM
N_PAGES = PAST_LEN // PAGE_SIZE
N_PAST_BLK = PAST_LEN // SLC_BLOCK

LANE = 128
VMEM_LIMIT = 56 * 1024 * 1024
N_TOK = BATCH * SEQ + DEC_BATCH * DEC_SEQ
TOK_TILE = 544
BIG_TOK_TILE = 2 * TOK_TILE
Q_TILE = 128
COL_QX = 3 * SELF_W
COL_GATE = COL_QX + CROSS_W
PROJ_W = COL_GATE + CROSS_W

F32 = jnp.float32
BF16 = jnp.bfloat16
_NT = (((1,), (1,)), ((), ()))


def _cparams(sem):
    return pltpu.CompilerParams(dimension_semantics=sem, vmem_limit_bytes=VMEM_LIMIT)


def _rms(x, g):
    return x * lax.rsqrt(jnp.mean(x * x, axis=-1, keepdims=True) + RMS_EPS) * g


def _dot(a, b):
    return jnp.dot(a, b, preferred_element_type=F32)


def _dot_nt(a, b):
    return lax.dot_general(a, b, _NT, preferred_element_type=F32)


def _t5_bias(dist, rb_ref, h):
    n = jnp.maximum(dist, 0)
    max_exact = N_BUCKETS // 2
    nf = jnp.maximum(n, 1).astype(F32)
    large = max_exact + (jnp.log(nf / max_exact) / math.log(MAX_DISTANCE / max_exact)
                         * (N_BUCKETS - max_exact)).astype(jnp.int32)
    bucket = jnp.where(n <= max_exact, n, jnp.minimum(large, N_BUCKETS - 1))
    out = jnp.zeros(dist.shape, F32)
    for b in range(N_BUCKETS):
        out = jnp.where(bucket == b, rb_ref[b, h], out)
    return out


def _softmax_rows(s, mask):
    s = jnp.where(mask, s, NEG_INF)
    e = jnp.where(mask, jnp.exp(s - jnp.max(s, axis=-1, keepdims=True)), 0.0)
    den = jnp.sum(e, axis=-1, keepdims=True)
    return e / jnp.where(den > 0.0, den, 1.0)


def _select_blocks(imp, cur, col, n_blocks):
    forced = (col == 0) | (col == cur) | (col == cur - 1)
    score = jnp.where(col <= cur, jnp.where(forced, FORCE_SCORE, imp), -1.0)
    rank = jnp.zeros(score.shape, jnp.int32)
    for i in range(n_blocks):
        ci = score[:, i:i + 1]
        beats = (ci > score) | ((ci == score) & (col > i))
        rank = rank + beats.astype(jnp.int32)
    return (rank < N_SEL) & (col <= cur)


class _Flash:
    def __init__(self, m_ref, l_ref, acc_ref):
        self.m, self.l, self.acc = m_ref, l_ref, acc_ref

    def reset(self):
        self.m[...] = jnp.full(self.m.shape, NEG_INF, F32)
        self.l[...] = jnp.zeros(self.l.shape, F32)
        self.acc[...] = jnp.zeros(self.acc.shape, F32)

    def update(self, s, v, mask=None):
        if mask is not None:
            s = jnp.where(mask, s, NEG_INF)
        m_old = self.m[...]
        m_new = jnp.maximum(m_old, jnp.max(s, axis=-1, keepdims=True))
        alpha = jnp.exp(m_old - m_new)
        p = jnp.exp(s - m_new)
        if mask is not None:
            p = jnp.where(mask, p, 0.0)
        self.l[...] = alpha * self.l[...] + jnp.sum(p, axis=-1, keepdims=True)
        self.acc[...] = alpha * self.acc[...] + _dot(p.astype(BF16), v)
        self.m[...] = m_new

    def result(self):
        return self.acc[...] / self.l[...]


FFN_TILES = 4


def _ffn_kernel(x_ref, g_ref, fg_ref, *refs, n_tiles, n_steps, final_norm):
    t = FFN_TILES
    wg, wu, wd = refs[:t], refs[t:2 * t], refs[2 * t:3 * t]
    o_ref, h_ref = refs[3 * t:]
    f = pl.program_id(1)

    @pl.when(f == 0)
    def _():
        x = x_ref[...]
        h_ref[...] = _rms(x, g_ref[...]).astype(BF16)
        o_ref[...] = x

    h = h_ref[...]
    a = _dot(h, jnp.concatenate([r[...].astype(BF16) for r in wg], axis=1))
    b = _dot(h, jnp.concatenate([r[...].astype(BF16) for r in wu], axis=1))
    act = a * jax.nn.sigmoid(a) * b
    tile = t * f + lax.broadcasted_iota(jnp.int32, act.shape, 1) // LANE
    act = jnp.where(tile < n_tiles, act, 0.0)
    w_down = jnp.concatenate([r[...].astype(BF16) for r in wd], axis=0)
    o_ref[...] += 0.5 * _dot(act.astype(BF16), w_down)

    if final_norm:
        @pl.when(f == n_steps - 1)
        def _():
            o_ref[...] = _rms(o_ref[...], fg_ref[...])


def ffn(x, g, w_gu, w_down, which, final_g=None):
    m, d = x.shape
    n_f = FFN_DIM // LANE
    n_steps = pl.cdiv(n_f, FFN_TILES)
    tm = TOK_TILE
    final_norm = final_g is not None
    fg = final_g if final_norm else g

    def col(off, k):
        return lambda i, f: (*which, 0, off + jnp.minimum(FFN_TILES * f + k, n_f - 1))

    def row(k):
        return lambda i, f: (*which, jnp.minimum(FFN_TILES * f + k, n_f - 1), 0)

    ks = range(FFN_TILES)
    return pl.pallas_call(
        functools.partial(_ffn_kernel, n_tiles=n_f, n_steps=n_steps, final_norm=final_norm),
        grid=(m // tm, n_steps),
        in_specs=[
            pl.BlockSpec((tm, d), lambda i, f: (i, 0)),
            pl.BlockSpec((1, d), lambda i, f: (0, 0)),
            pl.BlockSpec((1, d), lambda i, f: (0, 0)),
        ] + [pl.BlockSpec((None, None, d, LANE), col(0, k)) for k in ks]
        + [pl.BlockSpec((None, None, d, LANE), col(n_f, k)) for k in ks]
        + [pl.BlockSpec((None, None, LANE, d), row(k)) for k in ks],
        out_specs=pl.BlockSpec((tm, d), lambda i, f: (i, 0)),
        out_shape=jax.ShapeDtypeStruct((m, d), F32),
        scratch_shapes=[pltpu.VMEM((tm, d), BF16)],
        compiler_params=_cparams(("parallel", "arbitrary")),
        name="ffn",
    )(x, g.reshape(1, d), fg.reshape(1, d), *([w_gu] * (2 * FFN_TILES)), *([w_down] * FFN_TILES))


def _linear_kernel(*refs, n_x, norm, n_main, residual):
    x_refs = refs[:n_x]
    refs = refs[n_x:]
    if norm:
        g_ref, refs = refs[0], refs[1:]
    w_ref, wt_ref = refs[:2]
    refs = refs[2:]
    if residual:
        r_ref, o_ref, h_ref = refs
    else:
        o_ref, h_ref = refs
    j = pl.program_id(1)

    @pl.when(j == 0)
    def _():
        off = 0
        for x_ref in x_refs:
            x = x_ref[...].astype(F32)
            if norm:
                x = _rms(x, g_ref[...])
            h_ref[:, off:off + x.shape[1]] = x.astype(BF16)
            off += x.shape[1]

    def emit(w):
        y = _dot(h_ref[...], w[...].astype(BF16))
        if residual:
            y = y + r_ref[...]
        o_ref[...] = y

    pl.when(j < n_main)(lambda: emit(w_ref))
    pl.when(j >= n_main)(lambda: emit(wt_ref))


def linear(xs, w, w_tail=None, n_main=None, g=None, res=None, tm=TOK_TILE, tn=512):
    xs = list(xs) if isinstance(xs, (list, tuple)) else [xs]
    m = xs[0].shape[0]
    k = sum(x.shape[1] for x in xs)
    if w_tail is None:
        w_tail, n_main = w, w.shape[1] // tn
    n_blocks = n_main + (w_tail.shape[1] // tn if w_tail is not w else 0)
    norm, residual = g is not None, res is not None
    in_specs = [pl.BlockSpec((tm, x.shape[1]), lambda i, j: (i, 0)) for x in xs]
    args = list(xs)
    if norm:
        in_specs.append(pl.BlockSpec((1, k), lambda i, j: (0, 0)))
        args.append(g.reshape(1, k))
    in_specs.append(pl.BlockSpec((k, tn), lambda i, j: (0, jnp.minimum(j, n_main - 1))))
    in_specs.append(pl.BlockSpec((k, tn), lambda i, j: (0, jnp.maximum(j - n_main, 0))))
    args += [w, w_tail]
    if residual:
        in_specs.append(pl.BlockSpec((tm, tn), lambda i, j: (i, j)))
        args.append(res)
    return pl.pallas_call(
        functools.partial(_linear_kernel, n_x=len(xs), norm=norm, n_main=n_main, residual=residual),
        grid=(m // tm, n_blocks),
        in_specs=in_specs,
        out_specs=pl.BlockSpec((tm, tn), lambda i, j: (i, j)),
        out_shape=jax.ShapeDtypeStruct((m, n_blocks * tn), F32),
        scratch_shapes=[pltpu.VMEM((tm, k), BF16)],
        compiler_params=_cparams(("parallel", "arbitrary")),
        name="linear",
    )(*args)


def _compress_kernel(x_ref, pe_ref, w1_ref, w2_ref, o_ref, acc_ref, *, n_valid):
    l = pl.program_id(1)

    @pl.when(l == 0)
    def _():
        acc_ref[...] = jnp.zeros(acc_ref.shape, F32)

    pe = pe_ref[pl.ds(l, 1), :]
    w1 = w1_ref[...].astype(BF16)
    for g in range(NSA_KV_HEADS):
        xg = x_ref[:, g * HEAD_DIM:(g + 1) * HEAD_DIM]
        if n_valid < CMP_BLOCK:
            xg = jnp.where(l < n_valid, xg, 0.0)
        acc_ref[g] += _dot((xg + pe).astype(BF16), w1)

    @pl.when(l == CMP_BLOCK - 1)
    def _():
        w2 = w2_ref[...].astype(BF16)
        for g in range(NSA_KV_HEADS):
            a = acc_ref[g]
            hid = a * jax.nn.sigmoid(a)
            o_ref[:, g * HEAD_DIM:(g + 1) * HEAD_DIM] = _dot(hid.astype(BF16), w2)


def compress(x2d, col_block, n_valid, w1, w2, pe, tr, rows=None, row_block0=0):
    rows = x2d.shape[0] if rows is None else rows
    x_spec = pl.BlockSpec((tr, NSA_KV_W),
                          lambda i, l: (row_block0 + i, col_block(jnp.minimum(l, n_valid - 1))))
    return pl.pallas_call(
        functools.partial(_compress_kernel, n_valid=n_valid),
        grid=(rows // tr, CMP_BLOCK),
        in_specs=[
            x_spec,
            pl.BlockSpec((CMP_BLOCK, HEAD_DIM), lambda i, l: (0, 0)),
            pl.BlockSpec((HEAD_DIM, CMP_HIDDEN), lambda i, l: (l, 0)),
            pl.BlockSpec((CMP_HIDDEN, HEAD_DIM), lambda i, l: (0, 0)),
        ],
        out_specs=pl.BlockSpec((tr, NSA_KV_W), lambda i, l: (i, 0)),
        out_shape=jax.ShapeDtypeStruct((rows, NSA_KV_W), F32),
        scratch_shapes=[pltpu.VMEM((NSA_KV_HEADS, tr, CMP_HIDDEN), F32)],
        compiler_params=_cparams(("parallel", "arbitrary")),
        name="compress",
    )(x2d, pe, w1, w2)


def _compress_pool_kernel(x_ref, pe_ref, w1a_ref, w1b_ref, w2_ref, o_ref, acc_ref):
    lp = pl.program_id(1)
    tr, sub, _ = x_ref.shape
    half = sub // 2

    @pl.when(lp == 0)
    def _():
        acc_ref[...] = jnp.zeros(acc_ref.shape, F32)

    pe2 = pe_ref[pl.ds(pl.multiple_of(2 * lp, 2), 2), :]
    odd8 = lax.broadcasted_iota(jnp.int32, (sub, HEAD_DIM), 0) >= half
    pe8 = jnp.where(odd8, pe2[1:2, :], pe2[0:1, :])
    x = (x_ref[...] + pe8[None]).reshape(tr * sub, HEAD_DIM).astype(BF16)
    odd = lax.broadcasted_iota(jnp.int32, (tr * sub, HEAD_DIM), 0) % sub >= half
    zero = jnp.zeros_like(x)
    x2 = jnp.concatenate([jnp.where(odd, zero, x), jnp.where(odd, x, zero)], axis=1)
    w = jnp.concatenate([w1a_ref[...].astype(BF16), w1b_ref[...].astype(BF16)], axis=0)
    acc_ref[...] += _dot(x2, w)

    @pl.when(lp == CMP_BLOCK // 2 - 1)
    def _():
        a = acc_ref[...]
        a = a + pltpu.roll(a, half, 0)
        hid = a * jax.nn.sigmoid(a)
        o_ref[...] = _dot(hid.astype(BF16), w2_ref[...].astype(BF16)).reshape(tr, sub, HEAD_DIM)


def compress_pool(pool, w1, w2, pe, tr=512):
    n_blk = pool.shape[0] * PAGE_SIZE // CMP_BLOCK
    sub = 2 * NSA_KV_HEADS
    x = pool.reshape(n_blk, CMP_BLOCK * NSA_KV_HEADS, HEAD_DIM)
    out = pl.pallas_call(
        _compress_pool_kernel,
        grid=(n_blk // tr, CMP_BLOCK // 2),
        in_specs=[
            pl.BlockSpec((tr, sub, HEAD_DIM), lambda i, l: (i, l, 0)),
            pl.BlockSpec((CMP_BLOCK, HEAD_DIM), lambda i, l: (0, 0)),
            pl.BlockSpec((HEAD_DIM, CMP_HIDDEN), lambda i, l: (2 * l, 0)),
            pl.BlockSpec((HEAD_DIM, CMP_HIDDEN), lambda i, l: (2 * l + 1, 0)),
            pl.BlockSpec((CMP_HIDDEN, HEAD_DIM), lambda i, l: (0, 0)),
        ],
        out_specs=pl.BlockSpec((tr, sub, HEAD_DIM), lambda i, l: (i, 0, 0)),
        out_shape=jax.ShapeDtypeStruct((n_blk, sub, HEAD_DIM), F32),
        scratch_shapes=[pltpu.VMEM((tr * sub, CMP_HIDDEN), F32)],
        compiler_params=_cparams(("parallel", "arbitrary")),
        name="compress_pool",
    )(x, pe, w1, w1, w2)
    return out[:, NSA_KV_HEADS:, :].reshape(n_blk, NSA_KV_W)


def _nsa_prompt_kernel(rb_ref, q_ref, ks_ref, vs_ref, kw_ref, vw_ref, ck_ref, cv_ref, gl_ref, bg_ref,
                       qx_ref, mk_ref, mv_ref, o_ref, ox_ref, bt_ref, cb_ref, selx_ref, m_ref, l_ref, acc_ref):
    g = pl.program_id(1)
    qt = pl.program_id(2)
    q0 = qt * Q_TILE
    ii = lax.broadcasted_iota(jnp.int32, (Q_TILE, LANE), 0)
    jj = lax.broadcasted_iota(jnp.int32, (Q_TILE, LANE), 1)

    @pl.when(qt == 0)
    def _():
        d_new = (lax.broadcasted_iota(jnp.int32, (Q_TILE, 1), 0) + 1) % CMP_BLOCK
        for r in range(NSA_GROUP):
            h = NSA_GROUP * g + r
            bt_ref[r, 0] = _t5_bias(ii - jj, rb_ref, h)
            bt_ref[r, 1] = _t5_bias(LANE + ii - jj, rb_ref, h)
            cb_ref[r, 0] = _t5_bias(d_new, rb_ref, h)
            cb_ref[r, 1] = _t5_bias(d_new + CMP_BLOCK, rb_ref, h)

    def heads(f):
        return jnp.concatenate([f(r) for r in range(NSA_GROUP)], axis=0)

    qs = heads(lambda r: q_ref[:, r * HEAD_DIM:(r + 1) * HEAD_DIM]).astype(BF16)
    far_bias = heads(lambda r: jnp.full((Q_TILE, 1), rb_ref[N_BUCKETS - 1, NSA_GROUP * g + r], F32))
    diag_bias = heads(lambda r: bt_ref[r, 0])
    near_bias = heads(lambda r: bt_ref[r, 1])
    causal = jnp.concatenate([jj <= ii] * NSA_GROUP, axis=0)

    nb = ck_ref.shape[0]
    row = lax.broadcasted_iota(jnp.int32, (Q_TILE, nb), 0)
    col = lax.broadcasted_iota(jnp.int32, (Q_TILE, nb), 1)
    qpos = q0 + row
    dist = qpos - (col * CMP_BLOCK + (CMP_BLOCK - 1))
    sc = _dot_nt(qs, ck_ref[...].astype(BF16)) * ATTN_SCALE
    newest = (qpos - (CMP_BLOCK - 1)) // CMP_BLOCK
    probs = []
    for r in range(NSA_GROUP):
        bias = jnp.where(col == newest, cb_ref[r, 0],
                         jnp.where(col == newest - 1, cb_ref[r, 1], rb_ref[N_BUCKETS - 1, NSA_GROUP * g + r]))
        probs.append(_softmax_rows(sc[r * Q_TILE:(r + 1) * Q_TILE] + bias, dist >= 0))
    o_c = _dot(jnp.concatenate(probs, axis=0).astype(BF16), cv_ref[...].astype(BF16))
    imp = probs[0] + probs[1] + probs[2]
    sel = _select_blocks(imp, qpos // SLC_BLOCK, col, nb).astype(BF16)
    for t in range(selx_ref.shape[0]):
        kk = lax.broadcasted_iota(jnp.int32, (nb, LANE), 1) + t * LANE
        bj = lax.broadcasted_iota(jnp.int32, (nb, LANE), 0)
        expand = jnp.where(kk // SLC_BLOCK == bj, 1.0, 0.0).astype(BF16)
        selx_ref[t] = _dot(sel, expand)

    every = jnp.concatenate([jj >= 0] * NSA_GROUP, axis=0)

    def key_tile(ref, t):
        return ref[pl.ds(pl.multiple_of(jnp.maximum(t, 0) * LANE, LANE), LANE), :].astype(BF16)

    def scores(k_ref, t, bias):
        return _dot_nt(qs, key_tile(k_ref, t)) * ATTN_SCALE + bias

    def sel_mask(t):
        return jnp.concatenate([selx_ref[jnp.maximum(t, 0)] > 0.5] * NSA_GROUP, axis=0)

    def pair_update(s0, s1, m0, m1, v_ref, t0, t1):
        def pv(p):
            return _dot(p[:, :LANE], key_tile(v_ref, t0)) + _dot(p[:, LANE:], key_tile(v_ref, t1))
        _flash_update(m_ref, l_ref, acc_ref, jnp.concatenate([s0, s1], axis=1), pv,
                      jnp.concatenate([m0, m1], axis=1))

    m_ref[...] = jnp.full(m_ref.shape, NEG_INF, F32)
    l_ref[...] = jnp.zeros(l_ref.shape, F32)
    acc_ref[...] = jnp.zeros(acc_ref.shape, F32)
    n_far = jnp.maximum(qt - 1, 0)

    def far(it, carry):
        t0 = 2 * it
        t1 = jnp.minimum(t0 + 1, n_far - 1)
        pair_update(scores(ks_ref, t0, far_bias), scores(ks_ref, t1, far_bias),
                    sel_mask(t0), sel_mask(t1) & (t0 + 1 < n_far), vs_ref, t0, t1)
        return carry

    lax.fori_loop(0, (n_far + 1) // 2, far, 0)
    pair_update(scores(ks_ref, qt - 1, near_bias), scores(ks_ref, qt, diag_bias),
                sel_mask(qt - 1) & (qt >= 1), sel_mask(qt) & causal, vs_ref, qt - 1, qt)
    o_s = acc_ref[...] / l_ref[...]

    n_win = WINDOW // LANE
    upper = jnp.concatenate([jj > ii] * NSA_GROUP, axis=0)
    band_s, band_m = [], []
    for back in range(n_win, -1, -1):
        bias = diag_bias if back == 0 else near_bias if back == 1 else far_bias
        base = causal if back == 0 else upper if back == n_win else every
        band_s.append(scores(kw_ref, qt - back, bias))
        band_m.append(base & (qt >= back))
    pw = _softmax_rows(jnp.concatenate(band_s, axis=1), jnp.concatenate(band_m, axis=1)).astype(BF16)
    o_w = sum(_dot(pw[:, i * LANE:(i + 1) * LANE], key_tile(vw_ref, qt - back))
              for i, back in enumerate(range(n_win, -1, -1)))

    gate = jax.nn.sigmoid(gl_ref[...] + bg_ref[...])
    for r in range(NSA_GROUP):
        rows = slice(r * Q_TILE, (r + 1) * Q_TILE)
        o = (gate[:, 3 * r:3 * r + 1] * o_c[rows] + gate[:, 3 * r + 1:3 * r + 2] * o_s[rows]
             + gate[:, 3 * r + 2:3 * r + 3] * o_w[rows])
        o_ref[:, r * HEAD_DIM:(r + 1) * HEAD_DIM] = o.astype(o_ref.dtype)

    sx = _dot_nt(qx_ref[...].astype(BF16), mk_ref[...].astype(BF16)) * ATTN_SCALE
    px = _softmax_rows(sx, sx == sx)
    ox_ref[...] = _dot(px.astype(BF16), mv_ref[...].astype(BF16)).astype(ox_ref.dtype)


def nsa_prompt(proj, ck, cv, kvm, rel_bias, b_gate, batch):
    nq = SEQ // Q_TILE
    nb = SEQ // CMP_BLOCK
    hb = lambda c: c // HEAD_DIM
    qrow = lambda b, g, t: b * nq + t
    grid_spec = pltpu.PrefetchScalarGridSpec(
        num_scalar_prefetch=0,
        grid=(batch, NSA_KV_HEADS, nq),
        in_specs=[
            pl.BlockSpec(memory_space=pltpu.SMEM),
            pl.BlockSpec((Q_TILE, NSA_GROUP * HEAD_DIM), lambda b, g, t: (qrow(b, g, t), g)),
            pl.BlockSpec((SEQ, HEAD_DIM), lambda b, g, t: (b, hb(SELF_W + 2 * NSA_KV_W) + g)),
            pl.BlockSpec((SEQ, HEAD_DIM), lambda b, g, t: (b, hb(SELF_W + 3 * NSA_KV_W) + g)),
            pl.BlockSpec((SEQ, HEAD_DIM), lambda b, g, t: (b, hb(SELF_W + 4 * NSA_KV_W) + g)),
            pl.BlockSpec((SEQ, HEAD_DIM), lambda b, g, t: (b, hb(SELF_W + 5 * NSA_KV_W) + g)),
            pl.BlockSpec((nb, HEAD_DIM), lambda b, g, t: (b, g)),
            pl.BlockSpec((nb, HEAD_DIM), lambda b, g, t: (b, g)),
            pl.BlockSpec((Q_TILE, LANE), lambda b, g, t: (qrow(b, g, t), hb(COL_GATE) + g)),
            pl.BlockSpec((1, LANE), lambda b, g, t: (0, g)),
            pl.BlockSpec((Q_TILE, HEAD_DIM), lambda b, g, t: (qrow(b, g, t), hb(COL_QX) + g)),
            pl.BlockSpec((MEM_LEN, HEAD_DIM), lambda b, g, t: (b, g)),
            pl.BlockSpec((MEM_LEN, HEAD_DIM), lambda b, g, t: (b, N_CROSS_HEADS + g)),
        ],
        out_specs=[
            pl.BlockSpec((Q_TILE, NSA_GROUP * HEAD_DIM), lambda b, g, t: (qrow(b, g, t), g)),
            pl.BlockSpec((Q_TILE, HEAD_DIM), lambda b, g, t: (qrow(b, g, t), g)),
        ],
        scratch_shapes=[
            pltpu.VMEM((NSA_GROUP, 2, Q_TILE, LANE), F32),
            pltpu.VMEM((NSA_GROUP, 2, Q_TILE, 1), F32),
            pltpu.VMEM((SEQ // LANE, Q_TILE, LANE), F32),
            pltpu.VMEM((NSA_GROUP * Q_TILE, 1), F32),
            pltpu.VMEM((NSA_GROUP * Q_TILE, 1), F32),
            pltpu.VMEM((NSA_GROUP * Q_TILE, HEAD_DIM), F32),
        ],
    )
    return pl.pallas_call(
        _nsa_prompt_kernel,
        grid_spec=grid_spec,
        out_shape=[jax.ShapeDtypeStruct((batch * SEQ, SELF_W), BF16),
                   jax.ShapeDtypeStruct((batch * SEQ, CROSS_W), BF16)],
        compiler_params=_cparams(("arbitrary", "arbitrary", "arbitrary")),
        name="nsa_prompt",
    )(rel_bias, proj, proj, proj, proj, proj, ck, cv, proj, b_gate, proj, kvm, kvm)


FOX_TILE = 256
HEAD_PAD = 16


def _log_sigmoid(x):
    return jnp.minimum(x, 0.0) - jnp.log1p(jnp.exp(-jnp.abs(x)))


def _fox_prep_kernel(fl_ref, bf_ref, lf_ref, ct_ref):
    n_chunks = fl_ref.shape[0] // LANE
    ii = lax.broadcasted_iota(jnp.int32, (LANE, LANE), 0)
    jj = lax.broadcasted_iota(jnp.int32, (LANE, LANE), 1)
    tri = jnp.where(jj <= ii, 1.0, 0.0).astype(F32)
    carry = jnp.zeros((1, LANE), F32)
    per_tile = FOX_TILE // LANE
    for c in range(n_chunks):
        lf = _log_sigmoid(fl_ref[c * LANE:(c + 1) * LANE, :] + bf_ref[...])
        lf_ref[c * LANE:(c + 1) * LANE, :] = lf
        cs = jnp.dot(tri, lf, precision=lax.Precision.HIGHEST, preferred_element_type=F32) + carry
        carry = cs[LANE - 1:LANE, :]
        ct_ref[:, c // per_tile, (c % per_tile) * LANE:(c % per_tile + 1) * LANE] = cs.T[:HEAD_PAD]


def fox_prep(proj, b_forget, batch, seq_len=SEQ, row_block0=0):
    bf = jnp.pad(b_forget.reshape(1, FOX_HEADS), ((0, 0), (0, LANE - FOX_HEADS)))
    n_tiles = max(seq_len // FOX_TILE, 1)
    return pl.pallas_call(
        _fox_prep_kernel,
        grid=(batch,),
        in_specs=[pl.BlockSpec((seq_len, LANE), lambda b: (row_block0 + b, COL_GATE // LANE)),
                  pl.BlockSpec((1, LANE), lambda b: (0, 0))],
        out_specs=[pl.BlockSpec((seq_len, LANE), lambda b: (b, 0)),
                   pl.BlockSpec((None, HEAD_PAD, n_tiles, FOX_TILE), lambda b: (b, 0, 0, 0))],
        out_shape=[jax.ShapeDtypeStruct((batch * seq_len, LANE), F32),
                   jax.ShapeDtypeStruct((batch, HEAD_PAD, n_tiles, FOX_TILE), F32)],
        compiler_params=_cparams(("arbitrary",)),
        name="fox_prep",
    )(proj, bf)


FOX_HEAD_BLOCK = 4


def _fox_prompt_kernel(q_ref, k_ref, v_ref, c_ref, o_ref, m_ref, l_ref, acc_ref):
    qt = pl.program_id(2)
    nh = FOX_HEAD_BLOCK
    hs = [slice(h * HEAD_DIM, (h + 1) * HEAD_DIM) for h in range(nh)]
    q = [q_ref[:, s].astype(BF16) for s in hs]
    m_ref[...] = jnp.full(m_ref.shape, NEG_INF, F32)
    l_ref[...] = jnp.zeros(l_ref.shape, F32)
    acc_ref[...] = jnp.zeros(acc_ref.shape, F32)

    def tile(t, mask):
        rows = pl.ds(pl.multiple_of(t * FOX_TILE, FOX_TILE), FOX_TILE)
        s = jnp.concatenate([_dot_nt(q[h], k_ref[rows, hs[h]].astype(BF16)) * ATTN_SCALE
                             - c_ref[h, pl.ds(t, 1), :] for h in range(nh)], axis=0)

        def pv(p):
            return jnp.concatenate([_dot(p[h * FOX_TILE:(h + 1) * FOX_TILE], v_ref[rows, hs[h]].astype(BF16))
                                    for h in range(nh)], axis=0)

        _flash_update(m_ref, l_ref, acc_ref, s, pv, mask)

    def far(t, carry):
        tile(t, None)
        return carry

    lax.fori_loop(0, qt, far, 0)
    ii = lax.broadcasted_iota(jnp.int32, (nh * FOX_TILE, FOX_TILE), 0) % FOX_TILE
    jj = lax.broadcasted_iota(jnp.int32, (nh * FOX_TILE, FOX_TILE), 1)
    tile(qt, jj <= ii)
    o = acc_ref[...] / l_ref[...]
    for h in range(nh):
        o_ref[:, hs[h]] = o[h * FOX_TILE:(h + 1) * FOX_TILE].astype(o_ref.dtype)


def fox_prompt(proj, ct, batch):
    nq = SEQ // FOX_TILE
    nh = FOX_HEAD_BLOCK
    wb = nh * HEAD_DIM
    cb = lambda c: c // wb
    rows = nh * FOX_TILE
    return pl.pallas_call(
        _fox_prompt_kernel,
        grid=(batch, FOX_HEADS // nh, nq),
        in_specs=[
            pl.BlockSpec((FOX_TILE, wb), lambda b, h, t: (b * nq + t, h)),
            pl.BlockSpec((SEQ, wb), lambda b, h, t: (b, cb(SELF_W) + h)),
            pl.BlockSpec((SEQ, wb), lambda b, h, t: (b, cb(2 * SELF_W) + h)),
            pl.BlockSpec((None, nh, nq, FOX_TILE), lambda b, h, t: (b, h, 0, 0)),
        ],
        out_specs=pl.BlockSpec((FOX_TILE, wb), lambda b, h, t: (b * nq + t, h)),
        out_shape=jax.ShapeDtypeStruct((batch * SEQ, SELF_W), BF16),
        scratch_shapes=[pltpu.VMEM((rows, 1), F32), pltpu.VMEM((rows, 1), F32),
                        pltpu.VMEM((rows, HEAD_DIM), F32)],
        compiler_params=_cparams(("arbitrary", "arbitrary", "arbitrary")),
        name="fox_prompt",
    )(proj, proj, proj, ct)


def _cross_prompt_kernel(q_ref, mk_ref, mv_ref, o_ref):
    for h in range(N_CROSS_HEADS):
        hs = slice(h * HEAD_DIM, (h + 1) * HEAD_DIM)
        s = _dot_nt(q_ref[:, hs].astype(BF16), mk_ref[:, hs].astype(BF16)) * ATTN_SCALE
        p = _softmax_rows(s, s == s)
        o_ref[:, hs] = _dot(p.astype(BF16), mv_ref[:, hs].astype(BF16)).astype(o_ref.dtype)


def cross_prompt(proj, kvm, batch, tq=512):
    nq = SEQ // tq
    return pl.pallas_call(
        _cross_prompt_kernel,
        grid=(batch, nq),
        in_specs=[pl.BlockSpec((tq, CROSS_W), lambda b, t: (b * nq + t, COL_QX // CROSS_W)),
                  pl.BlockSpec((MEM_LEN, CROSS_W), lambda b, t: (b, 0)),
                  pl.BlockSpec((MEM_LEN, CROSS_W), lambda b, t: (b, 1))],
        out_specs=pl.BlockSpec((tq, CROSS_W), lambda b, t: (b * nq + t, 0)),
        out_shape=jax.ShapeDtypeStruct((batch * SEQ, CROSS_W), BF16),
        compiler_params=_cparams(("arbitrary", "arbitrary")),
        name="cross_prompt",
    )(proj, kvm, kvm)


ROWS = 16


def _pad_rows(x):
    return jnp.pad(x, [(0, 0)] * (x.ndim - 2) + [(0, ROWS - x.shape[-2]), (0, 0)])


def _group_rows(q):
    b = q.shape[0]
    q = q.reshape(b, DEC_SEQ, NSA_KV_HEADS, NSA_GROUP, HEAD_DIM).transpose(0, 2, 3, 1, 4)
    return _pad_rows(q.reshape(b, NSA_KV_HEADS, NSA_GROUP * DEC_SEQ, HEAD_DIM))


def _ungroup_rows(o):
    b = o.shape[0]
    o = o[:, :, :NSA_GROUP * DEC_SEQ].reshape(b, NSA_KV_HEADS, NSA_GROUP, DEC_SEQ, HEAD_DIM)
    return o.transpose(0, 3, 1, 2, 4).reshape(b * DEC_SEQ, SELF_W)


def _gate_rows(gl):
    b = gl.shape[0]
    gl = gl.reshape(b, DEC_SEQ, NSA_KV_HEADS, LANE).transpose(0, 2, 1, 3)
    return _pad_rows(jnp.tile(gl, (1, 1, NSA_GROUP, 1)))


def _head_rows(q, n_heads):
    b = q.shape[0]
    return _pad_rows(q.reshape(b, DEC_SEQ, n_heads, HEAD_DIM).transpose(0, 2, 1, 3))


def _unhead_rows(o):
    b, h = o.shape[:2]
    return o[:, :, :DEC_SEQ].transpose(0, 2, 1, 3).reshape(b * DEC_SEQ, h * HEAD_DIM)


def _row_ids():
    row = lax.broadcasted_iota(jnp.int32, (ROWS, LANE), 0)
    return row, row % DEC_SEQ, row // DEC_SEQ


def _per_head(r_idx, f):
    return jnp.where(r_idx == 0, f(0), jnp.where(r_idx == 1, f(1), f(2)))


def _safe_div(acc, l):
    return acc / jnp.where(l > 0.0, l, 1.0)


def _merge_new_keys(fl, q32, kn, vn, bias, masks):
    s_new = [jnp.where(masks[j], jnp.sum(q32 * kn[j:j + 1, :], axis=-1, keepdims=True) * ATTN_SCALE
                       + bias[j], NEG_INF) for j in range(DEC_SEQ)]
    m_old = fl.m[...]
    m_new = m_old
    for s in s_new:
        m_new = jnp.maximum(m_new, s)
    alpha = jnp.exp(m_old - m_new)
    l = alpha * fl.l[...]
    acc = alpha * fl.acc[...]
    for j in range(DEC_SEQ):
        p = jnp.where(masks[j], jnp.exp(s_new[j] - m_new), 0.0)
        l = l + p
        acc = acc + p * vn[j:j + 1, :]
    return _safe_div(acc, l)


PAGES_PER_STEP = 8
SUB = 8


def _flash_update(m_ref, l_ref, acc_ref, s, pv_fn, mask=None):
    if mask is not None:
        s = jnp.where(mask, s, NEG_INF)
    m_old = m_ref[...]
    m_new = jnp.maximum(m_old, jnp.max(s, axis=-1, keepdims=True))
    alpha = jnp.exp(m_old - m_new)
    p = jnp.exp(s - m_new)
    if mask is not None:
        p = jnp.where(mask, p, 0.0)
    l_ref[...] = alpha * l_ref[...] + jnp.sum(p, axis=-1, keepdims=True)
    acc_ref[...] = alpha * acc_ref[...] + pv_fn(p.astype(BF16))
    m_ref[...] = m_new


def _cumsum_lanes_kernel(x_ref, o_ref):
    ii = lax.broadcasted_iota(jnp.int32, (LANE, LANE), 0)
    jj = lax.broadcasted_iota(jnp.int32, (LANE, LANE), 1)
    upper = jnp.where(ii <= jj, 1.0, 0.0).astype(F32)
    o_ref[...] = jnp.dot(x_ref[...], upper, precision=lax.Precision.HIGHEST, preferred_element_type=F32)


def cumsum_lanes(x, tr=512):
    h, rows, _ = x.shape
    spec = pl.BlockSpec((None, tr, LANE), lambda i, j: (i, j, 0))
    return pl.pallas_call(
        _cumsum_lanes_kernel, grid=(h, rows // tr), in_specs=[spec], out_specs=spec,
        out_shape=jax.ShapeDtypeStruct(x.shape, F32),
        compiler_params=_cparams(("arbitrary", "arbitrary")), name="cumsum_lanes",
    )(x)


def _fox_sample_kernel(pt_ref, q_ref, kn_ref, vn_ref, *refs):
    n = PAGES_PER_STEP
    k_refs, v_refs, c_refs = refs[:n], refs[n:2 * n], refs[2 * n:3 * n]
    lfn_ref, o_ref, m_ref, l_ref, acc_ref, run_ref = refs[3 * n:]
    b = pl.program_id(0)
    step = pl.program_id(1)

    @pl.when(step == 0)
    def _():
        m_ref[...] = jnp.full(m_ref.shape, NEG_INF, F32)
        l_ref[...] = jnp.zeros(l_ref.shape, F32)
        acc_ref[...] = jnp.zeros(acc_ref.shape, F32)
        run_ref[...] = jnp.zeros(run_ref.shape, F32)

    sub = [pt_ref[b, step * n + j] % SUB for j in range(n)]
    s_heads = []
    for h in range(FOX_HEADS):
        q = q_ref[h].astype(BF16)
        run = run_ref[h:h + 1, :]
        parts = []
        for j in range(n):
            ck = run + c_refs[j][h, pl.ds(sub[j], 1), :]
            parts.append(_dot_nt(q, k_refs[j][h].astype(BF16)) * ATTN_SCALE - ck)
            run = jnp.broadcast_to(ck[:, PAGE_SIZE - 1:PAGE_SIZE], (1, LANE))
        run_ref[h:h + 1, :] = run
        s_heads.append(jnp.concatenate(parts, axis=1))
    s = jnp.concatenate(s_heads, axis=0)

    def pv(p):
        return jnp.concatenate(
            [sum(_dot(p[h * ROWS:(h + 1) * ROWS, j * PAGE_SIZE:(j + 1) * PAGE_SIZE], v_refs[j][h].astype(BF16))
                 for j in range(n)) for h in range(FOX_HEADS)], axis=0)

    _flash_update(m_ref, l_ref, acc_ref, s, pv)

    @pl.when(step == N_PAGES // n - 1)
    def _():
        row = lax.broadcasted_iota(jnp.int32, (ROWS, 1), 0)
        lfn = lfn_ref[...]
        for h in range(FOX_HEADS):
            hs = slice(h * HEAD_DIM, (h + 1) * HEAD_DIM)
            rs = slice(h * ROWS, (h + 1) * ROWS)
            c, c_new = run_ref[h:h + 1, 0:1], []
            for j in range(DEC_SEQ):
                c = c + lfn[h:h + 1, j:j + 1]
                c_new.append(-c)
            fl = _Flash(m_ref.at[rs], l_ref.at[rs], acc_ref.at[rs])
            o_ref[h] = _merge_new_keys(fl, q_ref[h], kn_ref[:, hs], vn_ref[:, hs], c_new,
                                       [row >= j for j in range(DEC_SEQ)])


def fox_sample(q16, ps3, pool_k, pool_v, ct_pool, lfn, page_table):
    nb = q16.shape[0]
    n = PAGES_PER_STEP
    page = lambda j: pl.BlockSpec((None, FOX_HEADS, PAGE_SIZE, HEAD_DIM),
                                  lambda b, s, pt: (pt[b, s * n + j], 0, 0, 0))
    cpage = lambda j: pl.BlockSpec((FOX_HEADS, SUB, PAGE_SIZE), lambda b, s, pt: (0, pt[b, s * n + j] // SUB, 0))
    rows = FOX_HEADS * ROWS
    grid_spec = pltpu.PrefetchScalarGridSpec(
        num_scalar_prefetch=1,
        grid=(nb, N_PAGES // n),
        in_specs=[
            pl.BlockSpec((None, FOX_HEADS, ROWS, HEAD_DIM), lambda b, s, pt: (b, 0, 0, 0)),
            pl.BlockSpec((None, DEC_SEQ, SELF_W), lambda b, s, pt: (b, 0, 1)),
            pl.BlockSpec((None, DEC_SEQ, SELF_W), lambda b, s, pt: (b, 0, 2)),
        ] + [page(j) for j in range(n)] + [page(j) for j in range(n)] + [cpage(j) for j in range(n)] + [
            pl.BlockSpec((None, HEAD_PAD, DEC_SEQ), lambda b, s, pt: (b, 0, 0)),
        ],
        out_specs=pl.BlockSpec((None, FOX_HEADS, ROWS, HEAD_DIM), lambda b, s, pt: (b, 0, 0, 0)),
        scratch_shapes=[pltpu.VMEM((rows, 1), F32), pltpu.VMEM((rows, 1), F32),
                        pltpu.VMEM((rows, HEAD_DIM), F32), pltpu.VMEM((HEAD_PAD, LANE), F32)],
    )
    return pl.pallas_call(
        _fox_sample_kernel,
        grid_spec=grid_spec,
        out_shape=jax.ShapeDtypeStruct((nb, FOX_HEADS, ROWS, HEAD_DIM), F32),
        compiler_params=_cparams(("arbitrary", "arbitrary")),
        name="fox_sample",
    )(page_table, q16, ps3, ps3, *([pool_k] * n), *([pool_v] * n), *([ct_pool] * n), lfn)


def _nsa_sample_cmp_kernel(pt_ref, rb_ref, q_ref, *refs):
    ck_pages, cv_pages = refs[:N_PAGES], refs[N_PAGES:2 * N_PAGES]
    ckn_ref, cvn_ref, oc_ref, sx_ref, ck_s, cv_s = refs[2 * N_PAGES:]
    b = pl.program_id(0)
    blk_per_page = PAGE_SIZE // CMP_BLOCK

    @pl.when(b == 0)
    def _():
        ck_s[...] = jnp.zeros(ck_s.shape, F32)
        cv_s[...] = jnp.zeros(cv_s.shape, F32)

    for p in range(N_PAGES):
        ck_s[p * blk_per_page:(p + 1) * blk_per_page, :] = ck_pages[p][...]
        cv_s[p * blk_per_page:(p + 1) * blk_per_page, :] = cv_pages[p][...]
    ck_s[N_PAST_BLK:N_PAST_BLK + 1, :] = ckn_ref[...]
    cv_s[N_PAST_BLK:N_PAST_BLK + 1, :] = cvn_ref[...]

    row, tok, r_idx = _row_ids()
    lane = lax.broadcasted_iota(jnp.int32, (ROWS, LANE), 1)
    qpos = PAST_LEN + tok
    dist = qpos - (lane * CMP_BLOCK + (CMP_BLOCK - 1))
    valid = (dist >= 0) & (lane <= N_PAST_BLK)
    for g in range(NSA_KV_HEADS):
        gs = slice(g * HEAD_DIM, (g + 1) * HEAD_DIM)
        s = _dot_nt(q_ref[g].astype(BF16), ck_s[:, gs].astype(BF16)) * ATTN_SCALE
        s = s + _per_head(r_idx, lambda r: _t5_bias(dist, rb_ref, NSA_GROUP * g + r))
        p = _softmax_rows(s, valid)
        oc_ref[g] = _dot(p.astype(BF16), cv_s[:, gs].astype(BF16))
        p0 = jnp.where(row < NSA_GROUP * DEC_SEQ, p, 0.0)
        imp = p0 + pltpu.roll(p0, ROWS - DEC_SEQ, 0) + pltpu.roll(p0, ROWS - 2 * DEC_SEQ, 0)
        imp = jnp.where(row < DEC_SEQ, imp, 0.0)
        imp = imp + pltpu.roll(imp, DEC_SEQ, 0) + pltpu.roll(imp, 2 * DEC_SEQ, 0)
        sel = _select_blocks(imp, qpos // SLC_BLOCK, lane, N_PAST_BLK + 1)
        sx_ref[g] = jnp.where(sel, 1.0, 0.0)


def nsa_sample_cmp(qg16, ck_pool, cv_pool, ck_new, cv_new, rel_bias, page_table):
    nb = qg16.shape[0]
    page = lambda p: pl.BlockSpec((None, PAGE_SIZE // CMP_BLOCK, NSA_KV_W), lambda b, pt: (pt[b, p], 0, 0))
    grid_spec = pltpu.PrefetchScalarGridSpec(
        num_scalar_prefetch=1,
        grid=(nb,),
        in_specs=[pl.BlockSpec(memory_space=pltpu.SMEM),
                  pl.BlockSpec((None, NSA_KV_HEADS, ROWS, HEAD_DIM), lambda b, pt: (b, 0, 0, 0))]
        + [page(p) for p in range(N_PAGES)] + [page(p) for p in range(N_PAGES)]
        + [pl.BlockSpec((None, 1, NSA_KV_W), lambda b, pt: (b, 0, 0))] * 2,
        out_specs=[pl.BlockSpec((None, NSA_KV_HEADS, ROWS, HEAD_DIM), lambda b, pt: (b, 0, 0, 0)),
                   pl.BlockSpec((None, NSA_KV_HEADS, ROWS, LANE), lambda b, pt: (b, 0, 0, 0))],
        scratch_shapes=[pltpu.VMEM((LANE, NSA_KV_W), F32), pltpu.VMEM((LANE, NSA_KV_W), F32)],
    )
    return pl.pallas_call(
        _nsa_sample_cmp_kernel,
        grid_spec=grid_spec,
        out_shape=[jax.ShapeDtypeStruct((nb, NSA_KV_HEADS, ROWS, HEAD_DIM), F32),
                   jax.ShapeDtypeStruct((nb, NSA_KV_HEADS, ROWS, LANE), F32)],
        compiler_params=_cparams(("arbitrary",)),
        name="nsa_sample_cmp",
    )(page_table, rel_bias, qg16, *([ck_pool] * N_PAGES), *([cv_pool] * N_PAGES), ck_new, cv_new)


def _nsa_sample_kernel(pt_ref, rb_ref, q_ref, *refs):
    n = PAGES_PER_STEP
    n_steps = N_PAGES // n
    k_refs, v_refs = refs[:n], refs[n:2 * n]
    (sx_ref, kn_ref, vn_ref, wk_ref, wv_ref, wkn_ref, wvn_ref, wkr_ref, wvr_ref, oc_ref, gl_ref, bg_ref,
     o_ref, wko_ref, wvo_ref, m_ref, l_ref, acc_ref, ow_ref, b15_ref, bn_ref) = refs[2 * n:]
    b = pl.program_id(0)
    p = pl.program_id(1)
    ng = NSA_KV_HEADS
    rows = ng * ROWS
    page_w = PAGE_SIZE * ng
    row, tok, r_idx = _row_ids()
    lane = lax.broadcasted_iota(jnp.int32, (ROWS, LANE), 1)
    tok1, r1 = tok[:, 0:1], r_idx[:, 0:1]
    new_vis = [tok1 >= j for j in range(DEC_SEQ)]
    q_all = q_ref[...].reshape(rows, HEAD_DIM).astype(BF16)
    grp_of_row = lax.broadcasted_iota(jnp.int32, (rows, 1), 0) // ROWS
    tok_of_row = lax.broadcasted_iota(jnp.int32, (rows, 1), 0) % DEC_SEQ

    def stack(f):
        return jnp.concatenate([f(g) for g in range(ng)], axis=0)

    far_bias = stack(lambda g: _per_head(r1, lambda r: rb_ref[N_BUCKETS - 1, NSA_GROUP * g + r]))

    def own_group(width):
        return lax.broadcasted_iota(jnp.int32, (rows, width), 1) % ng == grp_of_row

    @pl.when((b == 0) & (p == 0))
    def _():
        c_tok = lax.broadcasted_iota(jnp.int32, (ROWS, page_w), 1) // ng
        t_row = lax.broadcasted_iota(jnp.int32, (ROWS, page_w), 0) % DEC_SEQ
        r_row = lax.broadcasted_iota(jnp.int32, (ROWS, page_w), 0) // DEC_SEQ
        for g in range(ng):
            b15_ref[g * ROWS:(g + 1) * ROWS, :] = _per_head(
                r_row, lambda r: _t5_bias(LANE + t_row - c_tok, rb_ref, NSA_GROUP * g + r))
            bn_ref[g] = _per_head(r_idx, lambda r: _t5_bias(tok - lane, rb_ref, NSA_GROUP * g + r))

    def new_scores(kn):
        return [stack(lambda g: jnp.sum(q_ref[g] * kn[j:j + 1, g * HEAD_DIM:(g + 1) * HEAD_DIM], axis=-1,
                                        keepdims=True) * ATTN_SCALE + bn_ref[g][:, j:j + 1])
                for j in range(DEC_SEQ)]

    def new_values(vn, j):
        return stack(lambda g: jnp.broadcast_to(vn[j:j + 1, g * HEAD_DIM:(g + 1) * HEAD_DIM], (ROWS, HEAD_DIM)))

    @pl.when(p == 0)
    def _():
        m_ref[...] = jnp.full(m_ref.shape, NEG_INF, F32)
        l_ref[...] = jnp.zeros(l_ref.shape, F32)
        acc_ref[...] = jnp.zeros(acc_ref.shape, F32)
        width = WINDOW * ng
        s = _dot_nt(q_all, wk_ref[...].astype(BF16)) * ATTN_SCALE
        bias = jnp.concatenate([jnp.broadcast_to(far_bias, (rows, width - page_w)), b15_ref[...]], axis=1)
        c_tok = lax.broadcasted_iota(jnp.int32, (rows, width), 1) // ng
        in_win = own_group(width) & (c_tok > tok_of_row)
        s = jnp.where(in_win, s + bias, NEG_INF)
        vis = [tok_of_row >= j for j in range(DEC_SEQ)]
        s_new = [jnp.where(vis[j], sn, NEG_INF) for j, sn in enumerate(new_scores(wkn_ref))]
        mx = jnp.max(s, axis=-1, keepdims=True)
        for sn in s_new:
            mx = jnp.maximum(mx, sn)
        e = jnp.where(in_win, jnp.exp(s - mx), 0.0)
        den = jnp.sum(e, axis=-1, keepdims=True)
        acc = _dot(e.astype(BF16), wv_ref[...].astype(BF16))
        for j in range(DEC_SEQ):
            pj = jnp.where(vis[j], jnp.exp(s_new[j] - mx), 0.0)
            den = den + pj
            acc = acc + pj * new_values(wvn_ref, j)
        ow_ref[...] = _safe_div(acc, den)
        keep = (WINDOW - DEC_SEQ) * ng
        wko_ref[0:keep, :] = wk_ref[DEC_SEQ * ng:, :]
        wvo_ref[0:keep, :] = wv_ref[DEC_SEQ * ng:, :]
        wko_ref[keep:, :] = wkr_ref[...]
        wvo_ref[keep:, :] = wvr_ref[...]

    blk_per_page = PAGE_SIZE // SLC_BLOCK
    blk = lax.broadcasted_iota(jnp.int32, (LANE, page_w), 0)
    off = lax.broadcasted_iota(jnp.int32, (LANE, page_w), 1) // (SLC_BLOCK * ng)
    sel = sx_ref[...].reshape(rows, LANE).astype(BF16)
    own = own_group(page_w)
    s_parts, m_parts = [], []
    for j in range(n):
        bias = far_bias
        if j == n - 1:
            bias = jnp.where(p == n_steps - 1, b15_ref[...], bias)
        s_parts.append(_dot_nt(q_all, k_refs[j][...].astype(BF16)) * ATTN_SCALE + bias)
        to_cols = jnp.where(blk - off == (p * n + j) * blk_per_page, 1.0, 0.0).astype(BF16)
        m_parts.append((_dot(sel, to_cols) > 0.5) & own)

    def pv(pr):
        return sum(_dot(pr[:, j * page_w:(j + 1) * page_w], v_refs[j][...].astype(BF16)) for j in range(n))

    _flash_update(m_ref, l_ref, acc_ref, jnp.concatenate(s_parts, axis=1), pv, jnp.concatenate(m_parts, axis=1))

    @pl.when(p == n_steps - 1)
    def _():
        gate = jax.nn.sigmoid(gl_ref[...] + bg_ref[...].reshape(ng, 1, LANE))
        picked = stack(lambda g: sx_ref[g][:, N_PAST_BLK:N_PAST_BLK + 1]) > 0.5
        vis = [(tok_of_row >= j) & picked for j in range(DEC_SEQ)]
        s_new = [jnp.where(vis[j], sn, NEG_INF) for j, sn in enumerate(new_scores(kn_ref))]
        m_old = m_ref[...]
        m_new = m_old
        for sn in s_new:
            m_new = jnp.maximum(m_new, sn)
        alpha = jnp.exp(m_old - m_new)
        den = alpha * l_ref[...]
        acc = alpha * acc_ref[...]
        for j in range(DEC_SEQ):
            pj = jnp.where(vis[j], jnp.exp(s_new[j] - m_new), 0.0)
            den = den + pj
            acc = acc + pj * new_values(vn_ref, j)
        o_s = _safe_div(acc, den)
        o_w = ow_ref[...]
        for g in range(ng):
            rs = slice(g * ROWS, (g + 1) * ROWS)
            gt = [_per_head(r1, lambda r: gate[g][:, 3 * r + c:3 * r + c + 1]) for c in range(3)]
            o_ref[g] = gt[0] * oc_ref[g] + gt[1] * o_s[rs] + gt[2] * o_w[rs]


def nsa_sample(qg16, ps3, pool_k, pool_v, selx, buf_k, buf_v, o_c, gl16, b_gate, rel_bias, page_table):
    nb = qg16.shape[0]
    win_cols = slice(SELF_W + 4 * NSA_KV_W, SELF_W + 6 * NSA_KV_W)
    new_rows = ps3[:, :, win_cols].reshape(nb, DEC_SEQ, 2, NSA_KV_HEADS, HEAD_DIM)
    wk_rows = new_rows[:, :, 0].reshape(nb, DEC_SEQ * NSA_KV_HEADS, HEAD_DIM)
    wv_rows = new_rows[:, :, 1].reshape(nb, DEC_SEQ * NSA_KV_HEADS, HEAD_DIM)
    rspec = pl.BlockSpec((None, DEC_SEQ * NSA_KV_HEADS, HEAD_DIM), lambda b, p, pt: (b, 0, 0))
    n = PAGES_PER_STEP
    cb = lambda c: c // NSA_KV_W
    qspec = pl.BlockSpec((None, NSA_KV_HEADS, ROWS, HEAD_DIM), lambda b, p, pt: (b, 0, 0, 0))
    new = lambda c: pl.BlockSpec((None, DEC_SEQ, NSA_KV_W), lambda b, p, pt: (b, 0, cb(c)))
    page_w = PAGE_SIZE * NSA_KV_HEADS
    pool_k = pool_k.reshape(-1, page_w, HEAD_DIM)
    pool_v = pool_v.reshape(-1, page_w, HEAD_DIM)
    buf_k = buf_k.reshape(nb, WINDOW * NSA_KV_HEADS, HEAD_DIM)
    buf_v = buf_v.reshape(nb, WINDOW * NSA_KV_HEADS, HEAD_DIM)
    page = lambda j: pl.BlockSpec((None, page_w, HEAD_DIM), lambda b, p, pt: (pt[b, p * n + j], 0, 0))
    wspec = pl.BlockSpec((None, WINDOW * NSA_KV_HEADS, HEAD_DIM), lambda b, p, pt: (b, 0, 0))
    rows = NSA_KV_HEADS * ROWS
    grid_spec = pltpu.PrefetchScalarGridSpec(
        num_scalar_prefetch=1,
        grid=(nb, N_PAGES // n),
        in_specs=[pl.BlockSpec(memory_space=pltpu.SMEM), qspec]
        + [page(j) for j in range(n)] + [page(j) for j in range(n)] + [
            qspec,
            new(SELF_W + 2 * NSA_KV_W), new(SELF_W + 3 * NSA_KV_W),
            wspec, wspec,
            new(SELF_W + 4 * NSA_KV_W), new(SELF_W + 5 * NSA_KV_W),
            rspec, rspec,
            qspec, qspec,
            pl.BlockSpec((1, CROSS_W), lambda b, p, pt: (0, 0)),
        ],
        out_specs=[qspec, wspec, wspec],
        scratch_shapes=[pltpu.VMEM((rows, 1), F32), pltpu.VMEM((rows, 1), F32),
                        pltpu.VMEM((rows, HEAD_DIM), F32),
                        pltpu.VMEM((rows, HEAD_DIM), F32),
                        pltpu.VMEM((rows, page_w), F32),
                        pltpu.VMEM((NSA_KV_HEADS, ROWS, LANE), F32)],
    )
    o, win_k, win_v = pl.pallas_call(
        _nsa_sample_kernel,
        grid_spec=grid_spec,
        out_shape=[jax.ShapeDtypeStruct((nb, NSA_KV_HEADS, ROWS, HEAD_DIM), F32),
                   jax.ShapeDtypeStruct(buf_k.shape, F32), jax.ShapeDtypeStruct(buf_v.shape, F32)],
        compiler_params=_cparams(("arbitrary", "arbitrary")),
        name="nsa_sample",
    )(page_table, rel_bias, qg16, *([pool_k] * n), *([pool_v] * n), selx, ps3, ps3, buf_k, buf_v,
      ps3, ps3, wk_rows, wv_rows, o_c, gl16, b_gate)
    heads = (nb, WINDOW, NSA_KV_HEADS, HEAD_DIM)
    return o, win_k.reshape(heads), win_v.reshape(heads)


def _cross_sample_kernel(q_ref, mk_ref, mv_ref, o_ref):
    nh = N_CROSS_HEADS
    rows, width = nh * ROWS, MEM_LEN * nh
    q = q_ref[...].reshape(rows, HEAD_DIM).astype(BF16)
    s = _dot_nt(q, mk_ref[...].astype(BF16)) * ATTN_SCALE
    own = (lax.broadcasted_iota(jnp.int32, (rows, width), 1) % nh
           == lax.broadcasted_iota(jnp.int32, (rows, width), 0) // ROWS)
    p = _softmax_rows(s, own)
    o_ref[...] = _dot(p.astype(BF16), mv_ref[...].astype(BF16)).reshape(nh, ROWS, HEAD_DIM)


def cross_sample(qx16, mem_k, mem_v, layer):
    nb = qx16.shape[0]
    n_layers = mem_k.shape[0]
    mem_k = mem_k.reshape(n_layers, nb, MEM_LEN * N_CROSS_HEADS, HEAD_DIM)
    mem_v = mem_v.reshape(n_layers, nb, MEM_LEN * N_CROSS_HEADS, HEAD_DIM)
    qspec = pl.BlockSpec((None, N_CROSS_HEADS, ROWS, HEAD_DIM), lambda b: (b, 0, 0, 0))
    mspec = pl.BlockSpec((None, None, MEM_LEN * N_CROSS_HEADS, HEAD_DIM), lambda b: (layer, b, 0, 0))
    return pl.pallas_call(
        _cross_sample_kernel,
        grid=(nb,),
        in_specs=[qspec, mspec, mspec],
        out_specs=qspec,
        out_shape=jax.ShapeDtypeStruct((nb, N_CROSS_HEADS, ROWS, HEAD_DIM), F32),
        compiler_params=_cparams(("arbitrary",)),
        name="cross_sample",
    )(qx16, mem_k, mem_v)


def _arrange_in_weights(w, n_gate, per_group):
    gate = w[:, COL_QX:COL_QX + n_gate]
    if per_group:
        gate = jnp.pad(gate.reshape(-1, NSA_KV_HEADS, n_gate // NSA_KV_HEADS),
                       ((0, 0), (0, 0), (0, LANE - n_gate // NSA_KV_HEADS))).reshape(-1, CROSS_W)
    else:
        gate = jnp.pad(gate, ((0, 0), (0, CROSS_W - n_gate)))
    return jnp.concatenate([w[:, COL_QX + n_gate:], gate], axis=1)


def kernel(x_prompt, x_sample, mem_prompt, cache_nsa_cmp_k, cache_nsa_cmp_v, cache_nsa_slc_k, cache_nsa_slc_v, cache_nsa_win_k, cache_nsa_win_v, cache_fox_k, cache_fox_v, cache_fox_logf, cache_mem_k, cache_mem_v, page_table, rel_bias, norm_g, mem_norm_g, w_mem_kv, w_ff_gu, w_ff_down, w_in_nsa, b_gate_nsa, w_cmp1, w_cmp2, cmp_pe, w_out_nsa, w_in_fox, b_forget, w_out_fox, final_norm_g):
    n_p = BATCH * SEQ
    x = jnp.concatenate([x_prompt.reshape(n_p, D_MODEL), x_sample.reshape(-1, D_MODEL)], axis=0)
    mem = mem_prompt.reshape(BATCH * MEM_LEN, D_MODEL)
    n_pool = cache_nsa_cmp_k.shape[1]
    kv_heads = (BATCH, SEQ, NSA_KV_HEADS, HEAD_DIM)
    w_ff_gu = w_ff_gu.astype(BF16)
    w_ff_down = w_ff_down.astype(BF16)
    out = {}
    for i in range(DEPTH):
        x = ffn(x, norm_g[i, 0], w_ff_gu, w_ff_down, (i, 0))
        kvm = linear(mem, w_mem_kv[i], g=mem_norm_g[i], tm=512)
        out[f"mem_k{i}"] = kvm[:, :CROSS_W].reshape(BATCH, MEM_LEN, N_CROSS_HEADS, HEAD_DIM)
        out[f"mem_v{i}"] = kvm[:, CROSS_W:].reshape(BATCH, MEM_LEN, N_CROSS_HEADS, HEAD_DIM)
        a = i // 2
        if i % 2 == 0:
            w_tail = _arrange_in_weights(w_in_nsa[a], 3 * N_SELF_HEADS, True).astype(BF16)
            proj = linear(x, w_in_nsa[a].astype(BF16), w_tail, COL_QX // 512, g=norm_g[i, 1], tm=BIG_TOK_TILE)
            ps3 = proj[n_p:].reshape(DEC_BATCH, DEC_SEQ, PROJ_W)
            bg = jnp.pad(b_gate_nsa[a].reshape(NSA_KV_HEADS, 3 * NSA_GROUP),
                         ((0, 0), (0, LANE - 3 * NSA_GROUP))).reshape(1, CROSS_W)
            cmp_w = [(w_cmp1[a, c], w_cmp2[a, c], cmp_pe[a, c]) for c in range(2)]
            kvc = proj[:, SELF_W:SELF_W + 2 * NSA_KV_W]
            x_blk = kvc.reshape(N_TOK // CMP_BLOCK, CMP_BLOCK * 2 * NSA_KV_W)
            x_new = kvc.reshape(N_TOK // DEC_SEQ, DEC_SEQ * 2 * NSA_KV_W)
            pools = [cache_nsa_cmp_k[a], cache_nsa_cmp_v[a]]
            c_prompt, c_pool, c_new = [], [], []
            for c in range(2):
                col = lambda l, c=c: 2 * l + c
                c_prompt.append(compress(x_blk, col, CMP_BLOCK, *cmp_w[c], tr=BATCH * SEQ // CMP_BLOCK,
                                         rows=BATCH * SEQ // CMP_BLOCK))
                c_pool.append(compress_pool(pools[c], *cmp_w[c])
                              .reshape(n_pool, PAGE_SIZE // CMP_BLOCK, NSA_KV_W))
                c_new.append(compress(x_new, col, DEC_SEQ, *cmp_w[c], tr=DEC_BATCH, rows=DEC_BATCH,
                                      row_block0=n_p // DEC_SEQ // DEC_BATCH).reshape(DEC_BATCH, 1, NSA_KV_W))
            o_p, ox_p = nsa_prompt(proj, c_prompt[0], c_prompt[1], kvm, rel_bias, bg, BATCH)
            qg16 = _group_rows(ps3[:, :, :SELF_W])
            o_c, selx = nsa_sample_cmp(qg16, c_pool[0], c_pool[1], c_new[0], c_new[1], rel_bias, page_table)
            o_s, win_k, win_v = nsa_sample(
                qg16, ps3, cache_nsa_slc_k[a], cache_nsa_slc_v[a], selx,
                cache_nsa_win_k[a], cache_nsa_win_v[a],
                o_c, _gate_rows(ps3[:, :, COL_GATE:]), bg, rel_bias, page_table)
            o_s = _ungroup_rows(o_s)
            new_win = {"win_k": win_k, "win_v": win_v}
            for j, name in enumerate(("cmp_k", "cmp_v", "slc_k", "slc_v", "win_k", "win_v")):
                cols = slice(SELF_W + j * NSA_KV_W, SELF_W + (j + 1) * NSA_KV_W)
                st_p = proj[:n_p, cols].reshape(kv_heads)
                st_s = ps3[:, :, cols].reshape(DEC_BATCH, DEC_SEQ, NSA_KV_HEADS, HEAD_DIM)
                if name.startswith("win"):
                    st_p = st_p[:, -WINDOW:]
                    st_s = new_win[name]
                out.setdefault("p_" + name, []).append(st_p)
                out.setdefault("s_" + name, []).append(st_s)
            w_out = w_out_nsa[a]
        else:
            w_tail = _arrange_in_weights(w_in_fox[a], FOX_HEADS, False).astype(BF16)
            proj = linear(x, w_in_fox[a].astype(BF16), w_tail, COL_QX // 512, g=norm_g[i, 1], tm=BIG_TOK_TILE)
            ps3 = proj[n_p:].reshape(DEC_BATCH, DEC_SEQ, PROJ_W)
            lf_p, ct = fox_prep(proj, b_forget[a], BATCH)
            n_s = DEC_BATCH * DEC_SEQ
            lf_s, _ = fox_prep(proj, b_forget[a], 1, seq_len=n_s, row_block0=n_p // n_s)
            o_p = fox_prompt(proj, ct, BATCH)
            ox_p = cross_prompt(proj, kvm, BATCH)
            ct_pool = cumsum_lanes(jnp.transpose(cache_fox_logf[a], (2, 0, 1)))
            lfn = lf_s[:, :HEAD_PAD].reshape(DEC_BATCH, DEC_SEQ, HEAD_PAD).transpose(0, 2, 1)
            o_s = _unhead_rows(fox_sample(
                _head_rows(ps3[:, :, :SELF_W], FOX_HEADS), ps3,
                jnp.transpose(cache_fox_k[a], (0, 2, 1, 3)), jnp.transpose(cache_fox_v[a], (0, 2, 1, 3)),
                ct_pool, lfn, page_table))
            heads = (FOX_HEADS, HEAD_DIM)
            for j, name in enumerate(("fox_k", "fox_v")):
                cols = slice((j + 1) * SELF_W, (j + 2) * SELF_W)
                out.setdefault("p_" + name, []).append(proj[:n_p, cols].reshape(BATCH, SEQ, *heads))
                out.setdefault("s_" + name, []).append(ps3[:, :, cols].reshape(DEC_BATCH, DEC_SEQ, *heads))
            out.setdefault("p_fox_logf", []).append(lf_p[:, :FOX_HEADS].reshape(BATCH, SEQ, FOX_HEADS))
            out.setdefault("s_fox_logf", []).append(lf_s[:, :FOX_HEADS].reshape(DEC_BATCH, DEC_SEQ, FOX_HEADS))
            w_out = w_out_fox[a]
        ox_s = _unhead_rows(cross_sample(
            _head_rows(ps3[:, :, COL_QX:COL_GATE], N_CROSS_HEADS), cache_mem_k, cache_mem_v, i))
        mix_self = jnp.concatenate([o_p, o_s.astype(BF16)], axis=0)
        mix_cross = jnp.concatenate([ox_p, ox_s.astype(BF16)], axis=0)
        x = linear([mix_self, mix_cross], w_out.astype(BF16), res=x, tm=BIG_TOK_TILE)
        x = ffn(x, norm_g[i, 2], w_ff_gu, w_ff_down, (i, 1),
                final_g=final_norm_g if i == DEPTH - 1 else None)
    y_prompt = x[:n_p].reshape(BATCH, SEQ, D_MODEL)
    y_sample = x[n_p:].reshape(DEC_BATCH, DEC_SEQ, D_MODEL)
    st = lambda name: jnp.stack(out[name])
    p_mem_k = jnp.stack([out[f"mem_k{i}"] for i in range(DEPTH)])
    p_mem_v = jnp.stack([out[f"mem_v{i}"] for i in range(DEPTH)])
    return (y_prompt, y_sample,
            st("p_cmp_k"), st("p_cmp_v"), st("p_slc_k"), st("p_slc_v"), st("p_win_k"), st("p_win_v"),
            st("p_fox_k"), st("p_fox_v"), st("p_fox_logf"), p_mem_k, p_mem_v,
            st("s_cmp_k"), st("s_cmp_v"), st("s_slc_k"), st("s_slc_v"), st("s_win_k"), st("s_win_v"),
            st("s_fox_k"), st("s_fox_v"), st("s_fox_logf"))
```

```python
import functools
import math

import jax
import jax.numpy as jnp
from jax import lax
from jax.experimental import pallas as pl
from jax.experimental.pallas import tpu as pltpu

D_MODEL = 2048
BATCH = 4
SEQ = 2048
DEPTH = 2
DEC_BATCH = 128
DEC_SEQ = 4
PAST_LEN = 2048
PAGE_SIZE = 128
HEAD_DIM = 128
N_CROSS_HEADS = 4
N_SELF_HEADS = 12
NSA_KV_HEADS = 4
NSA_GROUP = N_SELF_HEADS // NSA_KV_HEADS
CMP_BLOCK = 64
SLC_BLOCK = 64
N_SEL = 16
WINDOW = 512
CMP_HIDDEN = 256
FOX_HEADS = N_SELF_HEADS
MEM_LEN = 256
N_BUCKETS = 32
MAX_DISTANCE = 128
FFN_DIM = ((8 * D_MODEL) // 3 + 127) // 128 * 128
RMS_EPS = 1e-6
FORCE_SCORE = 1e4
NEG_INF = -1e30
ATTN_SCALE = HEAD_DIM ** -0.5
SELF_W = N_SELF_HEADS * HEAD_DIM
NSA_KV_W = NSA_KV_HEADS * HEAD_DIM
CROSS_W = N_CROSS_HEADS * HEAD_DIM
N_PAGES = PAST_LEN // PAGE_SIZE
N_PAST_BLK = PAST_LEN // SLC_BLOCK

LANE = 128
VMEM_LIMIT = 56 * 1024 * 1024
N_TOK = BATCH * SEQ + DEC_BATCH * DEC_SEQ
TOK_TILE = 544
BIG_TOK_TILE = 2 * TOK_TILE
Q_TILE = 128
COL_QX = 3 * SELF_W
COL_GATE = COL_QX + CROSS_W
PROJ_W = COL_GATE + CROSS_W

F32 = jnp.float32
BF16 = jnp.bfloat16
_NT = (((1,), (1,)), ((), ()))


def _cparams(sem):
    return pltpu.CompilerParams(dimension_semantics=sem, vmem_limit_bytes=VMEM_LIMIT)


def _rms(x, g):
    return x * lax.rsqrt(jnp.mean(x * x, axis=-1, keepdims=True) + RMS_EPS) * g


def _dot(a, b):
    return jnp.dot(a, b, preferred_element_type=F32)


def _dot_nt(a, b):
    return lax.dot_general(a, b, _NT, preferred_element_type=F32)


def _t5_bias(dist, rb_ref, h):
    n = jnp.maximum(dist, 0)
    max_exact = N_BUCKETS // 2
    nf = jnp.maximum(n, 1).astype(F32)
    large = max_exact + (jnp.log(nf / max_exact) / math.log(MAX_DISTANCE / max_exact)
                         * (N_BUCKETS - max_exact)).astype(jnp.int32)
    bucket = jnp.where(n <= max_exact, n, jnp.minimum(large, N_BUCKETS - 1))
    out = jnp.zeros(dist.shape, F32)
    for b in range(N_BUCKETS):
        out = jnp.where(bucket == b, rb_ref[b, h], out)
    return out


def _softmax_rows(s, mask):
    s = jnp.where(mask, s, NEG_INF)
    e = jnp.where(mask, jnp.exp(s - jnp.max(s, axis=-1, keepdims=True)), 0.0)
    den = jnp.sum(e, axis=-1, keepdims=True)
    return e / jnp.where(den > 0.0, den, 1.0)


def _select_blocks(imp, cur, col, n_blocks):
    forced = (col == 0) | (col == cur) | (col == cur - 1)
    score = jnp.where(col <= cur, jnp.where(forced, FORCE_SCORE, imp), -1.0)
    rank = jnp.zeros(score.shape, jnp.int32)
    for i in range(n_blocks):
        ci = score[:, i:i + 1]
        beats = (ci > score) | ((ci == score) & (col > i))
        rank = rank + beats.astype(jnp.int32)
    return (rank < N_SEL) & (col <= cur)


class _Flash:
    def __init__(self, m_ref, l_ref, acc_ref):
        self.m, self.l, self.acc = m_ref, l_ref, acc_ref

    def reset(self):
        self.m[...] = jnp.full(self.m.shape, NEG_INF, F32)
        self.l[...] = jnp.zeros(self.l.shape, F32)
        self.acc[...] = jnp.zeros(self.acc.shape, F32)

    def update(self, s, v, mask=None):
        if mask is not None:
            s = jnp.where(mask, s, NEG_INF)
        m_old = self.m[...]
        m_new = jnp.maximum(m_old, jnp.max(s, axis=-1, keepdims=True))
        alpha = jnp.exp(m_old - m_new)
        p = jnp.exp(s - m_new)
        if mask is not None:
            p = jnp.where(mask, p, 0.0)
        self.l[...] = alpha * self.l[...] + jnp.sum(p, axis=-1, keepdims=True)
        self.acc[...] = alpha * self.acc[...] + _dot(p.astype(BF16), v)
        self.m[...] = m_new

    def result(self):
        return self.acc[...] / self.l[...]


FFN_TILES = 2


def _ffn_kernel(x_ref, g_ref, fg_ref, *refs, n_tiles, n_steps, final_norm):
    t = FFN_TILES
    wg, wu, wd = refs[:t], refs[t:2 * t], refs[2 * t:3 * t]
    o_ref, h_ref = refs[3 * t:]
    f = pl.program_id(1)

    @pl.when(f == 0)
    def _():
        x = x_ref[...]
        h_ref[...] = _rms(x, g_ref[...]).astype(BF16)
        o_ref[...] = x

    h = h_ref[...]
    a = _dot(h, jnp.concatenate([r[...].astype(BF16) for r in wg], axis=1))
    b = _dot(h, jnp.concatenate([r[...].astype(BF16) for r in wu], axis=1))
    act = a * jax.nn.sigmoid(a) * b
    tile = t * f + lax.broadcasted_iota(jnp.int32, act.shape, 1) // LANE
    act = jnp.where(tile < n_tiles, act, 0.0)
    w_down = jnp.concatenate([r[...].astype(BF16) for r in wd], axis=0)
    o_ref[...] += 0.5 * _dot(act.astype(BF16), w_down)

    if final_norm:
        @pl.when(f == n_steps - 1)
        def _():
            o_ref[...] = _rms(o_ref[...], fg_ref[...])


def ffn(x, g, w_gu, w_down, which, final_g=None):
    m, d = x.shape
    n_f = FFN_DIM // LANE
    n_steps = pl.cdiv(n_f, FFN_TILES)
    tm = BIG_TOK_TILE
    once = pl.Buffered(1)
    final_norm = final_g is not None
    fg = final_g if final_norm else g

    def col(off, k):
        return lambda i, f: (*which, 0, off + jnp.minimum(FFN_TILES * f + k, n_f - 1))

    def row(k):
        return lambda i, f: (*which, jnp.minimum(FFN_TILES * f + k, n_f - 1), 0)

    ks = range(FFN_TILES)
    return pl.pallas_call(
        functools.partial(_ffn_kernel, n_tiles=n_f, n_steps=n_steps, final_norm=final_norm),
        grid=(m // tm, n_steps),
        in_specs=[
            pl.BlockSpec((tm, d), lambda i, f: (i, 0), pipeline_mode=once),
            pl.BlockSpec((1, d), lambda i, f: (0, 0)),
            pl.BlockSpec((1, d), lambda i, f: (0, 0)),
        ] + [pl.BlockSpec((None, None, d, LANE), col(0, k)) for k in ks]
        + [pl.BlockSpec((None, None, d, LANE), col(n_f, k)) for k in ks]
        + [pl.BlockSpec((None, None, LANE, d), row(k)) for k in ks],
        out_specs=pl.BlockSpec((tm, d), lambda i, f: (i, 0)),
        out_shape=jax.ShapeDtypeStruct((m, d), F32),
        scratch_shapes=[pltpu.VMEM((tm, d), BF16)],
        compiler_params=_cparams(("parallel", "arbitrary")),
        name="ffn",
    )(x, g.reshape(1, d), fg.reshape(1, d), *([w_gu] * (2 * FFN_TILES)), *([w_down] * FFN_TILES))


def _linear_kernel(*refs, n_x, norm, n_main, residual):
    x_refs = refs[:n_x]
    refs = refs[n_x:]
    if norm:
        g_ref, refs = refs[0], refs[1:]
    w_ref, wt_ref = refs[:2]
    refs = refs[2:]
    if residual:
        r_ref, o_ref, h_ref = refs
    else:
        o_ref, h_ref = refs
    j = pl.program_id(1)

    @pl.when(j == 0)
    def _():
        off = 0
        for x_ref in x_refs:
            x = x_ref[...].astype(F32)
            if norm:
                x = _rms(x, g_ref[...])
            h_ref[:, off:off + x.shape[1]] = x.astype(BF16)
            off += x.shape[1]

    def emit(w):
        y = _dot(h_ref[...], w[...].astype(BF16))
        if residual:
            y = y + r_ref[...]
        o_ref[...] = y

    pl.when(j < n_main)(lambda: emit(w_ref))
    pl.when(j >= n_main)(lambda: emit(wt_ref))


def linear(xs, w, w_tail=None, n_main=None, g=None, res=None, tm=TOK_TILE, tn=512):
    xs = list(xs) if isinstance(xs, (list, tuple)) else [xs]
    m = xs[0].shape[0]
    k = sum(x.shape[1] for x in xs)
    if w_tail is None:
        w_tail, n_main = w, w.shape[1] // tn
    n_blocks = n_main + (w_tail.shape[1] // tn if w_tail is not w else 0)
    norm, residual = g is not None, res is not None
    in_specs = [pl.BlockSpec((tm, x.shape[1]), lambda i, j: (i, 0)) for x in xs]
    args = list(xs)
    if norm:
        in_specs.append(pl.BlockSpec((1, k), lambda i, j: (0, 0)))
        args.append(g.reshape(1, k))
    in_specs.append(pl.BlockSpec((k, tn), lambda i, j: (0, jnp.minimum(j, n_main - 1))))
    in_specs.append(pl.BlockSpec((k, tn), lambda i, j: (0, jnp.maximum(j - n_main, 0))))
    args += [w, w_tail]
    if residual:
        in_specs.append(pl.BlockSpec((tm, tn), lambda i, j: (i, j)))
        args.append(res)
    return pl.pallas_call(
        functools.partial(_linear_kernel, n_x=len(xs), norm=norm, n_main=n_main, residual=residual),
        grid=(m // tm, n_blocks),
        in_specs=in_specs,
        out_specs=pl.BlockSpec((tm, tn), lambda i, j: (i, j)),
        out_shape=jax.ShapeDtypeStruct((m, n_blocks * tn), F32),
        scratch_shapes=[pltpu.VMEM((tm, k), BF16)],
        compiler_params=_cparams(("parallel", "arbitrary")),
        name="linear",
    )(*args)


def _compress_kernel(x_ref, pe_ref, w1_ref, w2_ref, o_ref, acc_ref, *, n_valid):
    l = pl.program_id(1)

    @pl.when(l == 0)
    def _():
        acc_ref[...] = jnp.zeros(acc_ref.shape, F32)

    pe = pe_ref[pl.ds(l, 1), :]
    w1 = w1_ref[...].astype(BF16)
    for g in range(NSA_KV_HEADS):
        xg = x_ref[:, g * HEAD_DIM:(g + 1) * HEAD_DIM]
        if n_valid < CMP_BLOCK:
            xg = jnp.where(l < n_valid, xg, 0.0)
        acc_ref[g] += _dot((xg + pe).astype(BF16), w1)

    @pl.when(l == CMP_BLOCK - 1)
    def _():
        w2 = w2_ref[...].astype(BF16)
        for g in range(NSA_KV_HEADS):
            a = acc_ref[g]
            hid = a * jax.nn.sigmoid(a)
            o_ref[:, g * HEAD_DIM:(g + 1) * HEAD_DIM] = _dot(hid.astype(BF16), w2)


def compress(x2d, col_block, n_valid, w1, w2, pe, tr, rows=None, row_block0=0):
    rows = x2d.shape[0] if rows is None else rows
    x_spec = pl.BlockSpec((tr, NSA_KV_W),
                          lambda i, l: (row_block0 + i, col_block(jnp.minimum(l, n_valid - 1))))
    return pl.pallas_call(
        functools.partial(_compress_kernel, n_valid=n_valid),
        grid=(rows // tr, CMP_BLOCK),
        in_specs=[
            x_spec,
            pl.BlockSpec((CMP_BLOCK, HEAD_DIM), lambda i, l: (0, 0)),
            pl.BlockSpec((HEAD_DIM, CMP_HIDDEN), lambda i, l: (l, 0)),
            pl.BlockSpec((CMP_HIDDEN, HEAD_DIM), lambda i, l: (0, 0)),
        ],
        out_specs=pl.BlockSpec((tr, NSA_KV_W), lambda i, l: (i, 0)),
        out_shape=jax.ShapeDtypeStruct((rows, NSA_KV_W), F32),
        scratch_shapes=[pltpu.VMEM((NSA_KV_HEADS, tr, CMP_HIDDEN), F32)],
        compiler_params=_cparams(("parallel", "arbitrary")),
        name="compress",
    )(x2d, pe, w1, w2)


def _compress_pool_kernel(x_ref, pe_ref, w1a_ref, w1b_ref, w2_ref, o_ref, acc_ref):
    lp = pl.program_id(1)
    tr, sub, _ = x_ref.shape
    half = sub // 2

    @pl.when(lp == 0)
    def _():
        acc_ref[...] = jnp.zeros(acc_ref.shape, F32)

    pe2 = pe_ref[pl.ds(pl.multiple_of(2 * lp, 2), 2), :]
    odd8 = lax.broadcasted_iota(jnp.int32, (sub, HEAD_DIM), 0) >= half
    pe8 = jnp.where(odd8, pe2[1:2, :], pe2[0:1, :])
    x = (x_ref[...] + pe8[None]).reshape(tr * sub, HEAD_DIM).astype(BF16)
    odd = lax.broadcasted_iota(jnp.int32, (tr * sub, HEAD_DIM), 0) % sub >= half
    zero = jnp.zeros_like(x)
    x2 = jnp.concatenate([jnp.where(odd, zero, x), jnp.where(odd, x, zero)], axis=1)
    w = jnp.concatenate([w1a_ref[...].astype(BF16), w1b_ref[...].astype(BF16)], axis=0)
    acc_ref[...] += _dot(x2, w)

    @pl.when(lp == CMP_BLOCK // 2 - 1)
    def _():
        a = acc_ref[...]
        a = a + pltpu.roll(a, half, 0)
        hid = a * jax.nn.sigmoid(a)
        o_ref[...] = _dot(hid.astype(BF16), w2_ref[...].astype(BF16)).reshape(tr, sub, HEAD_DIM)


def compress_pool(pool, w1, w2, pe, tr=512):
    n_blk = pool.shape[0] * PAGE_SIZE // CMP_BLOCK
    sub = 2 * NSA_KV_HEADS
    x = pool.reshape(n_blk, CMP_BLOCK * NSA_KV_HEADS, HEAD_DIM)
    out = pl.pallas_call(
        _compress_pool_kernel,
        grid=(n_blk // tr, CMP_BLOCK // 2),
        in_specs=[
            pl.BlockSpec((tr, sub, HEAD_DIM), lambda i, l: (i, l, 0)),
            pl.BlockSpec((CMP_BLOCK, HEAD_DIM), lambda i, l: (0, 0)),
            pl.BlockSpec((HEAD_DIM, CMP_HIDDEN), lambda i, l: (2 * l, 0)),
            pl.BlockSpec((HEAD_DIM, CMP_HIDDEN), lambda i, l: (2 * l + 1, 0)),
            pl.BlockSpec((CMP_HIDDEN, HEAD_DIM), lambda i, l: (0, 0)),
        ],
        out_specs=pl.BlockSpec((tr, sub, HEAD_DIM), lambda i, l: (i, 0, 0)),
        out_shape=jax.ShapeDtypeStruct((n_blk, sub, HEAD_DIM), F32),
        scratch_shapes=[pltpu.VMEM((tr * sub, CMP_HIDDEN), F32)],
        compiler_params=_cparams(("parallel", "arbitrary")),
        name="compress_pool",
    )(x, pe, w1, w1, w2)
    return out[:, NSA_KV_HEADS:, :].reshape(n_blk, NSA_KV_W)


def _nsa_prompt_kernel(rb_ref, q_ref, ks_ref, vs_ref, kw_ref, vw_ref, ck_ref, cv_ref, gl_ref, bg_ref,
                       qx_ref, mk_ref, mv_ref, o_ref, ox_ref, bt_ref, cb_ref, m_ref, l_ref, acc_ref):
    g = pl.program_id(1)
    qt = pl.program_id(2)
    q0 = qt * Q_TILE
    ii = lax.broadcasted_iota(jnp.int32, (Q_TILE, LANE), 0)
    jj = lax.broadcasted_iota(jnp.int32, (Q_TILE, LANE), 1)

    @pl.when(qt == 0)
    def _():
        d_new = (lax.broadcasted_iota(jnp.int32, (Q_TILE, 1), 0) + 1) % CMP_BLOCK
        for r in range(NSA_GROUP):
            h = NSA_GROUP * g + r
            bt_ref[r, 0] = _t5_bias(ii - jj, rb_ref, h)
            bt_ref[r, 1] = _t5_bias(LANE + ii - jj, rb_ref, h)
            cb_ref[r, 0] = _t5_bias(d_new, rb_ref, h)
            cb_ref[r, 1] = _t5_bias(d_new + CMP_BLOCK, rb_ref, h)

    def heads(f):
        return jnp.concatenate([f(r) for r in range(NSA_GROUP)], axis=0)

    qs = heads(lambda r: q_ref[:, r * HEAD_DIM:(r + 1) * HEAD_DIM]).astype(BF16)
    far_bias = heads(lambda r: jnp.full((Q_TILE, 1), rb_ref[N_BUCKETS - 1, NSA_GROUP * g + r], F32))
    diag_bias = heads(lambda r: bt_ref[r, 0])
    near_bias = heads(lambda r: bt_ref[r, 1])
    causal = jnp.concatenate([jj <= ii] * NSA_GROUP, axis=0)

    nb = ck_ref.shape[0]
    row = lax.broadcasted_iota(jnp.int32, (Q_TILE, nb), 0)
    col = lax.broadcasted_iota(jnp.int32, (Q_TILE, nb), 1)
    qpos = q0 + row
    dist = qpos - (col * CMP_BLOCK + (CMP_BLOCK - 1))
    sc = _dot_nt(qs, ck_ref[...].astype(BF16)) * ATTN_SCALE
    newest = (qpos - (CMP_BLOCK - 1)) // CMP_BLOCK
    probs = []
    for r in range(NSA_GROUP):
        bias = jnp.where(col == newest, cb_ref[r, 0],
                         jnp.where(col == newest - 1, cb_ref[r, 1], rb_ref[N_BUCKETS - 1, NSA_GROUP * g + r]))
        probs.append(_softmax_rows(sc[r * Q_TILE:(r + 1) * Q_TILE] + bias, dist >= 0))
    o_c = _dot(jnp.concatenate(probs, axis=0).astype(BF16), cv_ref[...].astype(BF16))
    imp = probs[0] + probs[1] + probs[2]
    sel = jnp.where(_select_blocks(imp, qpos // SLC_BLOCK, col, nb), 1.0, 0.0).astype(BF16)
    blk_minus_off = (lax.broadcasted_iota(jnp.int32, (nb, LANE), 0)
                     - lax.broadcasted_iota(jnp.int32, (nb, LANE), 1) // SLC_BLOCK)

    every = jnp.concatenate([jj >= 0] * NSA_GROUP, axis=0)

    def key_tile(ref, t):
        return ref[pl.ds(pl.multiple_of(jnp.maximum(t, 0) * LANE, LANE), LANE), :].astype(BF16)

    def scores(k_ref, t, bias):
        return _dot_nt(qs, key_tile(k_ref, t)) * ATTN_SCALE + bias

    def sel_mask(t):
        first = jnp.maximum(t, 0) * (LANE // SLC_BLOCK)
        to_keys = jnp.where(blk_minus_off == first, 1.0, 0.0).astype(BF16)
        return jnp.concatenate([_dot(sel, to_keys) > 0.5] * NSA_GROUP, axis=0)

    def pair_update(s0, s1, m0, m1, v_ref, t0, t1):
        def pv(p):
            return _dot(p[:, :LANE], key_tile(v_ref, t0)) + _dot(p[:, LANE:], key_tile(v_ref, t1))
        _flash_update(m_ref, l_ref, acc_ref, jnp.concatenate([s0, s1], axis=1), pv,
                      jnp.concatenate([m0, m1], axis=1))

    m_ref[...] = jnp.full(m_ref.shape, NEG_INF, F32)
    l_ref[...] = jnp.zeros(l_ref.shape, F32)
    acc_ref[...] = jnp.zeros(acc_ref.shape, F32)
    n_far = jnp.maximum(qt - 1, 0)

    def far(it, carry):
        t0 = 2 * it
        t1 = jnp.minimum(t0 + 1, n_far - 1)
        pair_update(scores(ks_ref, t0, far_bias), scores(ks_ref, t1, far_bias),
                    sel_mask(t0), sel_mask(t1) & (t0 + 1 < n_far), vs_ref, t0, t1)
        return carry

    lax.fori_loop(0, (n_far + 1) // 2, far, 0)
    pair_update(scores(ks_ref, qt - 1, near_bias), scores(ks_ref, qt, diag_bias),
                sel_mask(qt - 1) & (qt >= 1), sel_mask(qt) & causal, vs_ref, qt - 1, qt)
    o_s = acc_ref[...] / l_ref[...]

    n_win = WINDOW // LANE
    upper = jnp.concatenate([jj > ii] * NSA_GROUP, axis=0)
    band_s, band_m = [], []
    for back in range(n_win, -1, -1):
        bias = diag_bias if back == 0 else near_bias if back == 1 else far_bias
        base = causal if back == 0 else upper if back == n_win else every
        band_s.append(scores(kw_ref, qt - back, bias))
        band_m.append(base & (qt >= back))
    pw = _softmax_rows(jnp.concatenate(band_s, axis=1), jnp.concatenate(band_m, axis=1)).astype(BF16)
    o_w = sum(_dot(pw[:, i * LANE:(i + 1) * LANE], key_tile(vw_ref, qt - back))
              for i, back in enumerate(range(n_win, -1, -1)))

    gate = jax.nn.sigmoid(gl_ref[...] + bg_ref[...])
    for r in range(NSA_GROUP):
        rows = slice(r * Q_TILE, (r + 1) * Q_TILE)
        o = (gate[:, 3 * r:3 * r + 1] * o_c[rows] + gate[:, 3 * r + 1:3 * r + 2] * o_s[rows]
             + gate[:, 3 * r + 2:3 * r + 3] * o_w[rows])
        o_ref[:, r * HEAD_DIM:(r + 1) * HEAD_DIM] = o.astype(o_ref.dtype)

    sx = _dot_nt(qx_ref[...].astype(BF16), mk_ref[...].astype(BF16)) * ATTN_SCALE
    px = _softmax_rows(sx, sx == sx)
    ox_ref[...] = _dot(px.astype(BF16), mv_ref[...].astype(BF16)).astype(ox_ref.dtype)


def nsa_prompt(proj, ck, cv, kvm, rel_bias, b_gate, batch):
    nq = SEQ // Q_TILE
    nb = SEQ // CMP_BLOCK
    hb = lambda c: c // HEAD_DIM
    qrow = lambda b, g, t: b * nq + t
    grid_spec = pltpu.PrefetchScalarGridSpec(
        num_scalar_prefetch=0,
        grid=(batch, NSA_KV_HEADS, nq),
        in_specs=[
            pl.BlockSpec(memory_space=pltpu.SMEM),
            pl.BlockSpec((Q_TILE, NSA_GROUP * HEAD_DIM), lambda b, g, t: (qrow(b, g, t), g)),
            pl.BlockSpec((SEQ, HEAD_DIM), lambda b, g, t: (b, hb(SELF_W + 2 * NSA_KV_W) + g)),
            pl.BlockSpec((SEQ, HEAD_DIM), lambda b, g, t: (b, hb(SELF_W + 3 * NSA_KV_W) + g)),
            pl.BlockSpec((SEQ, HEAD_DIM), lambda b, g, t: (b, hb(SELF_W + 4 * NSA_KV_W) + g)),
            pl.BlockSpec((SEQ, HEAD_DIM), lambda b, g, t: (b, hb(SELF_W + 5 * NSA_KV_W) + g)),
            pl.BlockSpec((nb, HEAD_DIM), lambda b, g, t: (b, g)),
            pl.BlockSpec((nb, HEAD_DIM), lambda b, g, t: (b, g)),
            pl.BlockSpec((Q_TILE, LANE), lambda b, g, t: (qrow(b, g, t), hb(COL_GATE) + g)),
            pl.BlockSpec((1, LANE), lambda b, g, t: (0, g)),
            pl.BlockSpec((Q_TILE, HEAD_DIM), lambda b, g, t: (qrow(b, g, t), hb(COL_QX) + g)),
            pl.BlockSpec((MEM_LEN, HEAD_DIM), lambda b, g, t: (b, g)),
            pl.BlockSpec((MEM_LEN, HEAD_DIM), lambda b, g, t: (b, N_CROSS_HEADS + g)),
        ],
        out_specs=[
            pl.BlockSpec((Q_TILE, NSA_GROUP * HEAD_DIM), lambda b, g, t: (qrow(b, g, t), g)),
            pl.BlockSpec((Q_TILE, HEAD_DIM), lambda b, g, t: (qrow(b, g, t), g)),
        ],
        scratch_shapes=[
            pltpu.VMEM((NSA_GROUP, 2, Q_TILE, LANE), F32),
            pltpu.VMEM((NSA_GROUP, 2, Q_TILE, 1), F32),
            pltpu.VMEM((NSA_GROUP * Q_TILE, 1), F32),
            pltpu.VMEM((NSA_GROUP * Q_TILE, 1), F32),
            pltpu.VMEM((NSA_GROUP * Q_TILE, HEAD_DIM), F32),
        ],
    )
    return pl.pallas_call(
        _nsa_prompt_kernel,
        grid_spec=grid_spec,
        out_shape=[jax.ShapeDtypeStruct((batch * SEQ, SELF_W), BF16),
                   jax.ShapeDtypeStruct((batch * SEQ, CROSS_W), BF16)],
        compiler_params=_cparams(("arbitrary", "arbitrary", "arbitrary")),
        name="nsa_prompt",
    )(rel_bias, proj, proj, proj, proj, proj, ck, cv, proj, b_gate, proj, kvm, kvm)


FOX_TILE = 256
HEAD_PAD = 16


def _log_sigmoid(x):
    return jnp.minimum(x, 0.0) - jnp.log1p(jnp.exp(-jnp.abs(x)))


def _fox_prep_kernel(fl_ref, bf_ref, lf_ref, ct_ref):
    n_chunks = fl_ref.shape[0] // LANE
    ii = lax.broadcasted_iota(jnp.int32, (LANE, LANE), 0)
    jj = lax.broadcasted_iota(jnp.int32, (LANE, LANE), 1)
    tri = jnp.where(jj <= ii, 1.0, 0.0).astype(F32)
    carry = jnp.zeros((1, LANE), F32)
    per_tile = FOX_TILE // LANE
    for c in range(n_chunks):
        lf = _log_sigmoid(fl_ref[c * LANE:(c + 1) * LANE, :] + bf_ref[...])
        lf_ref[c * LANE:(c + 1) * LANE, :] = lf
        cs = jnp.dot(tri, lf, precision=lax.Precision.HIGHEST, preferred_element_type=F32) + carry
        carry = cs[LANE - 1:LANE, :]
        ct_ref[:, c // per_tile, (c % per_tile) * LANE:(c % per_tile + 1) * LANE] = cs.T[:HEAD_PAD]


def fox_prep(proj, b_forget, batch, seq_len=SEQ, row_block0=0):
    bf = jnp.pad(b_forget.reshape(1, FOX_HEADS), ((0, 0), (0, LANE - FOX_HEADS)))
    n_tiles = max(seq_len // FOX_TILE, 1)
    return pl.pallas_call(
        _fox_prep_kernel,
        grid=(batch,),
        in_specs=[pl.BlockSpec((seq_len, LANE), lambda b: (row_block0 + b, COL_GATE // LANE)),
                  pl.BlockSpec((1, LANE), lambda b: (0, 0))],
        out_specs=[pl.BlockSpec((seq_len, LANE), lambda b: (b, 0)),
                   pl.BlockSpec((None, HEAD_PAD, n_tiles, FOX_TILE), lambda b: (b, 0, 0, 0))],
        out_shape=[jax.ShapeDtypeStruct((batch * seq_len, LANE), F32),
                   jax.ShapeDtypeStruct((batch, HEAD_PAD, n_tiles, FOX_TILE), F32)],
        compiler_params=_cparams(("arbitrary",)),
        name="fox_prep",
    )(proj, bf)


FOX_HEAD_BLOCK = 4


def _fox_prompt_kernel(q_ref, k_ref, v_ref, c_ref, o_ref, m_ref, l_ref, acc_ref):
    qt = pl.program_id(2)
    nh = FOX_HEAD_BLOCK
    hs = [slice(h * HEAD_DIM, (h + 1) * HEAD_DIM) for h in range(nh)]
    q = [q_ref[:, s].astype(BF16) for s in hs]
    m_ref[...] = jnp.full(m_ref.shape, NEG_INF, F32)
    l_ref[...] = jnp.zeros(l_ref.shape, F32)
    acc_ref[...] = jnp.zeros(acc_ref.shape, F32)

    def tile(t, mask):
        rows = pl.ds(pl.multiple_of(t * FOX_TILE, FOX_TILE), FOX_TILE)
        s = jnp.concatenate([_dot_nt(q[h], k_ref[rows, hs[h]].astype(BF16)) * ATTN_SCALE
                             - c_ref[h, pl.ds(t, 1), :] for h in range(nh)], axis=0)

        def pv(p):
            return jnp.concatenate([_dot(p[h * FOX_TILE:(h + 1) * FOX_TILE], v_ref[rows, hs[h]].astype(BF16))
                                    for h in range(nh)], axis=0)

        _flash_update(m_ref, l_ref, acc_ref, s, pv, mask)

    def far(t, carry):
        tile(t, None)
        return carry

    lax.fori_loop(0, qt, far, 0)
    ii = lax.broadcasted_iota(jnp.int32, (nh * FOX_TILE, FOX_TILE), 0) % FOX_TILE
    jj = lax.broadcasted_iota(jnp.int32, (nh * FOX_TILE, FOX_TILE), 1)
    tile(qt, jj <= ii)
    o = acc_ref[...] / l_ref[...]
    for h in range(nh):
        o_ref[:, hs[h]] = o[h * FOX_TILE:(h + 1) * FOX_TILE].astype(o_ref.dtype)


def fox_prompt(proj, ct, batch):
    nq = SEQ // FOX_TILE
    nh = FOX_HEAD_BLOCK
    wb = nh * HEAD_DIM
    cb = lambda c: c // wb
    rows = nh * FOX_TILE
    return pl.pallas_call(
        _fox_prompt_kernel,
        grid=(batch, FOX_HEADS // nh, nq),
        in_specs=[
            pl.BlockSpec((FOX_TILE, wb), lambda b, h, t: (b * nq + t, h)),
            pl.BlockSpec((SEQ, wb), lambda b, h, t: (b, cb(SELF_W) + h)),
            pl.BlockSpec((SEQ, wb), lambda b, h, t: (b, cb(2 * SELF_W) + h)),
            pl.BlockSpec((None, nh, nq, FOX_TILE), lambda b, h, t: (b, h, 0, 0)),
        ],
        out_specs=pl.BlockSpec((FOX_TILE, wb), lambda b, h, t: (b * nq + t, h)),
        out_shape=jax.ShapeDtypeStruct((batch * SEQ, SELF_W), BF16),
        scratch_shapes=[pltpu.VMEM((rows, 1), F32), pltpu.VMEM((rows, 1), F32),
                        pltpu.VMEM((rows, HEAD_DIM), F32)],
        compiler_params=_cparams(("arbitrary", "arbitrary", "arbitrary")),
        name="fox_prompt",
    )(proj, proj, proj, ct)


def _cross_prompt_kernel(q_ref, mk_ref, mv_ref, o_ref):
    for h in range(N_CROSS_HEADS):
        hs = slice(h * HEAD_DIM, (h + 1) * HEAD_DIM)
        s = _dot_nt(q_ref[:, hs].astype(BF16), mk_ref[:, hs].astype(BF16)) * ATTN_SCALE
        p = _softmax_rows(s, s == s)
        o_ref[:, hs] = _dot(p.astype(BF16), mv_ref[:, hs].astype(BF16)).astype(o_ref.dtype)


def cross_prompt(proj, kvm, batch, tq=512):
    nq = SEQ // tq
    return pl.pallas_call(
        _cross_prompt_kernel,
        grid=(batch, nq),
        in_specs=[pl.BlockSpec((tq, CROSS_W), lambda b, t: (b * nq + t, COL_QX // CROSS_W)),
                  pl.BlockSpec((MEM_LEN, CROSS_W), lambda b, t: (b, 0)),
                  pl.BlockSpec((MEM_LEN, CROSS_W), lambda b, t: (b, 1))],
        out_specs=pl.BlockSpec((tq, CROSS_W), lambda b, t: (b * nq + t, 0)),
        out_shape=jax.ShapeDtypeStruct((batch * SEQ, CROSS_W), BF16),
        compiler_params=_cparams(("arbitrary", "arbitrary")),
        name="cross_prompt",
    )(proj, kvm, kvm)


ROWS = 16


def _pad_rows(x):
    return jnp.pad(x, [(0, 0)] * (x.ndim - 2) + [(0, ROWS - x.shape[-2]), (0, 0)])


def _group_rows(q):
    b = q.shape[0]
    q = q.reshape(b, DEC_SEQ, NSA_KV_HEADS, NSA_GROUP, HEAD_DIM).transpose(0, 2, 3, 1, 4)
    return _pad_rows(q.reshape(b, NSA_KV_HEADS, NSA_GROUP * DEC_SEQ, HEAD_DIM))


def _ungroup_rows(o):
    b = o.shape[0]
    o = o[:, :, :NSA_GROUP * DEC_SEQ].reshape(b, NSA_KV_HEADS, NSA_GROUP, DEC_SEQ, HEAD_DIM)
    return o.transpose(0, 3, 1, 2, 4).reshape(b * DEC_SEQ, SELF_W)


def _gate_rows(gl):
    b = gl.shape[0]
    gl = gl.reshape(b, DEC_SEQ, NSA_KV_HEADS, LANE).transpose(0, 2, 1, 3)
    return _pad_rows(jnp.tile(gl, (1, 1, NSA_GROUP, 1)))


def _head_rows(q, n_heads):
    b = q.shape[0]
    return _pad_rows(q.reshape(b, DEC_SEQ, n_heads, HEAD_DIM).transpose(0, 2, 1, 3))


def _unhead_rows(o):
    b, h = o.shape[:2]
    return o[:, :, :DEC_SEQ].transpose(0, 2, 1, 3).reshape(b * DEC_SEQ, h * HEAD_DIM)


def _row_ids():
    row = lax.broadcasted_iota(jnp.int32, (ROWS, LANE), 0)
    return row, row % DEC_SEQ, row // DEC_SEQ


def _per_head(r_idx, f):
    return jnp.where(r_idx == 0, f(0), jnp.where(r_idx == 1, f(1), f(2)))


def _safe_div(acc, l):
    return acc / jnp.where(l > 0.0, l, 1.0)


def _merge_new_keys(fl, q32, kn, vn, bias, masks):
    s_new = [jnp.where(masks[j], jnp.sum(q32 * kn[j:j + 1, :], axis=-1, keepdims=True) * ATTN_SCALE
                       + bias[j], NEG_INF) for j in range(DEC_SEQ)]
    m_old = fl.m[...]
    m_new = m_old
    for s in s_new:
        m_new = jnp.maximum(m_new, s)
    alpha = jnp.exp(m_old - m_new)
    l = alpha * fl.l[...]
    acc = alpha * fl.acc[...]
    for j in range(DEC_SEQ):
        p = jnp.where(masks[j], jnp.exp(s_new[j] - m_new), 0.0)
        l = l + p
        acc = acc + p * vn[j:j + 1, :]
    return _safe_div(acc, l)


PAGES_PER_STEP = 8
SUB = 8


def _flash_update(m_ref, l_ref, acc_ref, s, pv_fn, mask=None):
    if mask is not None:
        s = jnp.where(mask, s, NEG_INF)
    m_old = m_ref[...]
    m_new = jnp.maximum(m_old, jnp.max(s, axis=-1, keepdims=True))
    alpha = jnp.exp(m_old - m_new)
    p = jnp.exp(s - m_new)
    if mask is not None:
        p = jnp.where(mask, p, 0.0)
    p = p.astype(BF16)
    row_sum = _dot(p, jnp.ones((p.shape[1], LANE), BF16))[:, 0:1]
    l_ref[...] = alpha * l_ref[...] + row_sum
    acc_ref[...] = alpha * acc_ref[...] + pv_fn(p)
    m_ref[...] = m_new


def _cumsum_lanes_kernel(x_ref, o_ref):
    ii = lax.broadcasted_iota(jnp.int32, (LANE, LANE), 0)
    jj = lax.broadcasted_iota(jnp.int32, (LANE, LANE), 1)
    upper = jnp.where(ii <= jj, 1.0, 0.0).astype(F32)
    o_ref[...] = jnp.dot(x_ref[...], upper, precision=lax.Precision.HIGHEST, preferred_element_type=F32)


def cumsum_lanes(x, tr=512):
    h, rows, _ = x.shape
    spec = pl.BlockSpec((None, tr, LANE), lambda i, j: (i, j, 0))
    return pl.pallas_call(
        _cumsum_lanes_kernel, grid=(h, rows // tr), in_specs=[spec], out_specs=spec,
        out_shape=jax.ShapeDtypeStruct(x.shape, F32),
        compiler_params=_cparams(("arbitrary", "arbitrary")), name="cumsum_lanes",
    )(x)


def _fox_sample_kernel(pt_ref, q_ref, kn_ref, vn_ref, *refs):
    n = PAGES_PER_STEP
    k_refs, v_refs, c_refs = refs[:n], refs[n:2 * n], refs[2 * n:3 * n]
    lfn_ref, o_ref, m_ref, l_ref, acc_ref, run_ref = refs[3 * n:]
    b = pl.program_id(0)
    step = pl.program_id(1)

    @pl.when(step == 0)
    def _():
        m_ref[...] = jnp.full(m_ref.shape, NEG_INF, F32)
        l_ref[...] = jnp.zeros(l_ref.shape, F32)
        acc_ref[...] = jnp.zeros(acc_ref.shape, F32)
        run_ref[...] = jnp.zeros(run_ref.shape, F32)

    sub = [pt_ref[b, step * n + j] % SUB for j in range(n)]
    s_heads = []
    for h in range(FOX_HEADS):
        q = q_ref[h].astype(BF16)
        run = run_ref[h:h + 1, :]
        parts = []
        for j in range(n):
            ck = run + c_refs[j][h, pl.ds(sub[j], 1), :]
            parts.append(_dot_nt(q, k_refs[j][h].astype(BF16)) * ATTN_SCALE - ck)
            run = jnp.broadcast_to(ck[:, PAGE_SIZE - 1:PAGE_SIZE], (1, LANE))
        run_ref[h:h + 1, :] = run
        s_heads.append(jnp.concatenate(parts, axis=1))
    s = jnp.concatenate(s_heads, axis=0)

    def pv(p):
        return jnp.concatenate(
            [sum(_dot(p[h * ROWS:(h + 1) * ROWS, j * PAGE_SIZE:(j + 1) * PAGE_SIZE], v_refs[j][h].astype(BF16))
                 for j in range(n)) for h in range(FOX_HEADS)], axis=0)

    _flash_update(m_ref, l_ref, acc_ref, s, pv)

    @pl.when(step == N_PAGES // n - 1)
    def _():
        row = lax.broadcasted_iota(jnp.int32, (ROWS, 1), 0)
        lfn = lfn_ref[...]
        for h in range(FOX_HEADS):
            hs = slice(h * HEAD_DIM, (h + 1) * HEAD_DIM)
            rs = slice(h * ROWS, (h + 1) * ROWS)
            c, c_new = run_ref[h:h + 1, 0:1], []
            for j in range(DEC_SEQ):
                c = c + lfn[h:h + 1, j:j + 1]
                c_new.append(-c)
            fl = _Flash(m_ref.at[rs], l_ref.at[rs], acc_ref.at[rs])
            o_ref[h] = _merge_new_keys(fl, q_ref[h], kn_ref[:, hs], vn_ref[:, hs], c_new,
                                       [row >= j for j in range(DEC_SEQ)])


def fox_sample(q16, ps3, pool_k, pool_v, ct_pool, lfn, page_table):
    nb = q16.shape[0]
    n = PAGES_PER_STEP
    page = lambda j: pl.BlockSpec((None, FOX_HEADS, PAGE_SIZE, HEAD_DIM),
                                  lambda b, s, pt: (pt[b, s * n + j], 0, 0, 0))
    cpage = lambda j: pl.BlockSpec((FOX_HEADS, SUB, PAGE_SIZE), lambda b, s, pt: (0, pt[b, s * n + j] // SUB, 0))
    rows = FOX_HEADS * ROWS
    grid_spec = pltpu.PrefetchScalarGridSpec(
        num_scalar_prefetch=1,
        grid=(nb, N_PAGES // n),
        in_specs=[
            pl.BlockSpec((None, FOX_HEADS, ROWS, HEAD_DIM), lambda b, s, pt: (b, 0, 0, 0)),
            pl.BlockSpec((None, DEC_SEQ, SELF_W), lambda b, s, pt: (b, 0, 1)),
            pl.BlockSpec((None, DEC_SEQ, SELF_W), lambda b, s, pt: (b, 0, 2)),
        ] + [page(j) for j in range(n)] + [page(j) for j in range(n)] + [cpage(j) for j in range(n)] + [
            pl.BlockSpec((None, HEAD_PAD, DEC_SEQ), lambda b, s, pt: (b, 0, 0)),
        ],
        out_specs=pl.BlockSpec((None, FOX_HEADS, ROWS, HEAD_DIM), lambda b, s, pt: (b, 0, 0, 0)),
        scratch_shapes=[pltpu.VMEM((rows, 1), F32), pltpu.VMEM((rows, 1), F32),
                        pltpu.VMEM((rows, HEAD_DIM), F32), pltpu.VMEM((HEAD_PAD, LANE), F32)],
    )
    return pl.pallas_call(
        _fox_sample_kernel,
        grid_spec=grid_spec,
        out_shape=jax.ShapeDtypeStruct((nb, FOX_HEADS, ROWS, HEAD_DIM), F32),
        compiler_params=_cparams(("arbitrary", "arbitrary")),
        name="fox_sample",
    )(page_table, q16, ps3, ps3, *([pool_k] * n), *([pool_v] * n), *([ct_pool] * n), lfn)


def _nsa_sample_cmp_kernel(pt_ref, rb_ref, q_ref, *refs):
    ck_pages, cv_pages = refs[:N_PAGES], refs[N_PAGES:2 * N_PAGES]
    ckn_ref, cvn_ref, oc_ref, sx_ref, ck_s, cv_s = refs[2 * N_PAGES:]
    b = pl.program_id(0)
    blk_per_page = PAGE_SIZE // CMP_BLOCK

    @pl.when(b == 0)
    def _():
        ck_s[...] = jnp.zeros(ck_s.shape, F32)
        cv_s[...] = jnp.zeros(cv_s.shape, F32)

    for p in range(N_PAGES):
        ck_s[p * blk_per_page:(p + 1) * blk_per_page, :] = ck_pages[p][...]
        cv_s[p * blk_per_page:(p + 1) * blk_per_page, :] = cv_pages[p][...]
    ck_s[N_PAST_BLK:N_PAST_BLK + 1, :] = ckn_ref[...]
    cv_s[N_PAST_BLK:N_PAST_BLK + 1, :] = cvn_ref[...]

    row, tok, r_idx = _row_ids()
    lane = lax.broadcasted_iota(jnp.int32, (ROWS, LANE), 1)
    qpos = PAST_LEN + tok
    dist = qpos - (lane * CMP_BLOCK + (CMP_BLOCK - 1))
    valid = (dist >= 0) & (lane <= N_PAST_BLK)
    for g in range(NSA_KV_HEADS):
        gs = slice(g * HEAD_DIM, (g + 1) * HEAD_DIM)
        s = _dot_nt(q_ref[g].astype(BF16), ck_s[:, gs].astype(BF16)) * ATTN_SCALE
        s = s + _per_head(r_idx, lambda r: _t5_bias(dist, rb_ref, NSA_GROUP * g + r))
        p = _softmax_rows(s, valid)
        oc_ref[g] = _dot(p.astype(BF16), cv_s[:, gs].astype(BF16))
        p0 = jnp.where(row < NSA_GROUP * DEC_SEQ, p, 0.0)
        imp = p0 + pltpu.roll(p0, ROWS - DEC_SEQ, 0) + pltpu.roll(p0, ROWS - 2 * DEC_SEQ, 0)
        imp = jnp.where(row < DEC_SEQ, imp, 0.0)
        imp = imp + pltpu.roll(imp, DEC_SEQ, 0) + pltpu.roll(imp, 2 * DEC_SEQ, 0)
        sel = _select_blocks(imp, qpos // SLC_BLOCK, lane, N_PAST_BLK + 1)
        sx_ref[g] = jnp.where(sel, 1.0, 0.0)


def nsa_sample_cmp(qg16, ck_pool, cv_pool, ck_new, cv_new, rel_bias, page_table):
    nb = qg16.shape[0]
    page = lambda p: pl.BlockSpec((None, PAGE_SIZE // CMP_BLOCK, NSA_KV_W), lambda b, pt: (pt[b, p], 0, 0))
    grid_spec = pltpu.PrefetchScalarGridSpec(
        num_scalar_prefetch=1,
        grid=(nb,),
        in_specs=[pl.BlockSpec(memory_space=pltpu.SMEM),
                  pl.BlockSpec((None, NSA_KV_HEADS, ROWS, HEAD_DIM), lambda b, pt: (b, 0, 0, 0))]
        + [page(p) for p in range(N_PAGES)] + [page(p) for p in range(N_PAGES)]
        + [pl.BlockSpec((None, 1, NSA_KV_W), lambda b, pt: (b, 0, 0))] * 2,
        out_specs=[pl.BlockSpec((None, NSA_KV_HEADS, ROWS, HEAD_DIM), lambda b, pt: (b, 0, 0, 0)),
                   pl.BlockSpec((None, NSA_KV_HEADS, ROWS, LANE), lambda b, pt: (b, 0, 0, 0))],
        scratch_shapes=[pltpu.VMEM((LANE, NSA_KV_W), F32), pltpu.VMEM((LANE, NSA_KV_W), F32)],
    )
    return pl.pallas_call(
        _nsa_sample_cmp_kernel,
        grid_spec=grid_spec,
        out_shape=[jax.ShapeDtypeStruct((nb, NSA_KV_HEADS, ROWS, HEAD_DIM), F32),
                   jax.ShapeDtypeStruct((nb, NSA_KV_HEADS, ROWS, LANE), F32)],
        compiler_params=_cparams(("arbitrary",)),
        name="nsa_sample_cmp",
    )(page_table, rel_bias, qg16, *([ck_pool] * N_PAGES), *([cv_pool] * N_PAGES), ck_new, cv_new)


def _nsa_sample_kernel(pt_ref, rb_ref, q_ref, *refs):
    n = PAGES_PER_STEP
    n_steps = N_PAGES // n
    k_refs, v_refs = refs[:n], refs[n:2 * n]
    (sx_ref, kn_ref, vn_ref, wk_ref, wv_ref, wkn_ref, wvn_ref, wkr_ref, wvr_ref, oc_ref, gl_ref, bg_ref,
     o_ref, wko_ref, wvo_ref, m_ref, l_ref, acc_ref, ow_ref, b15_ref, bn_ref) = refs[2 * n:]
    b = pl.program_id(0)
    p = pl.program_id(1)
    ng = NSA_KV_HEADS
    rows = ng * ROWS
    page_w = PAGE_SIZE * ng
    row, tok, r_idx = _row_ids()
    lane = lax.broadcasted_iota(jnp.int32, (ROWS, LANE), 1)
    tok1, r1 = tok[:, 0:1], r_idx[:, 0:1]
    new_vis = [tok1 >= j for j in range(DEC_SEQ)]
    q_all = q_ref[...].reshape(rows, HEAD_DIM).astype(BF16)
    grp_of_row = lax.broadcasted_iota(jnp.int32, (rows, 1), 0) // ROWS
    tok_of_row = lax.broadcasted_iota(jnp.int32, (rows, 1), 0) % DEC_SEQ

    def stack(f):
        return jnp.concatenate([f(g) for g in range(ng)], axis=0)

    far_bias = stack(lambda g: _per_head(r1, lambda r: rb_ref[N_BUCKETS - 1, NSA_GROUP * g + r]))

    def own_group(width):
        return lax.broadcasted_iota(jnp.int32, (rows, width), 1) % ng == grp_of_row

    @pl.when((b == 0) & (p == 0))
    def _():
        c_tok = lax.broadcasted_iota(jnp.int32, (ROWS, page_w), 1) // ng
        t_row = lax.broadcasted_iota(jnp.int32, (ROWS, page_w), 0) % DEC_SEQ
        r_row = lax.broadcasted_iota(jnp.int32, (ROWS, page_w), 0) // DEC_SEQ
        for g in range(ng):
            b15_ref[g * ROWS:(g + 1) * ROWS, :] = _per_head(
                r_row, lambda r: _t5_bias(LANE + t_row - c_tok, rb_ref, NSA_GROUP * g + r))
            bn_ref[g] = _per_head(r_idx, lambda r: _t5_bias(tok - lane, rb_ref, NSA_GROUP * g + r))

    def new_scores(kn):
        return [stack(lambda g: jnp.sum(q_ref[g] * kn[j:j + 1, g * HEAD_DIM:(g + 1) * HEAD_DIM], axis=-1,
                                        keepdims=True) * ATTN_SCALE + bn_ref[g][:, j:j + 1])
                for j in range(DEC_SEQ)]

    def new_values(vn, j):
        return stack(lambda g: jnp.broadcast_to(vn[j:j + 1, g * HEAD_DIM:(g + 1) * HEAD_DIM], (ROWS, HEAD_DIM)))

    @pl.when(p == 0)
    def _():
        m_ref[...] = jnp.full(m_ref.shape, NEG_INF, F32)
        l_ref[...] = jnp.zeros(l_ref.shape, F32)
        acc_ref[...] = jnp.zeros(acc_ref.shape, F32)
        width = WINDOW * ng
        s = _dot_nt(q_all, wk_ref[...].astype(BF16)) * ATTN_SCALE
        bias = jnp.concatenate([jnp.broadcast_to(far_bias, (rows, width - page_w)), b15_ref[...]], axis=1)
        c_tok = lax.broadcasted_iota(jnp.int32, (rows, width), 1) // ng
        in_win = own_group(width) & (c_tok > tok_of_row)
        s = jnp.where(in_win, s + bias, NEG_INF)
        vis = [tok_of_row >= j for j in range(DEC_SEQ)]
        s_new = [jnp.where(vis[j], sn, NEG_INF) for j, sn in enumerate(new_scores(wkn_ref))]
        mx = jnp.max(s, axis=-1, keepdims=True)
        for sn in s_new:
            mx = jnp.maximum(mx, sn)
        e = jnp.where(in_win, jnp.exp(s - mx), 0.0)
        den = jnp.sum(e, axis=-1, keepdims=True)
        acc = _dot(e.astype(BF16), wv_ref[...].astype(BF16))
        for j in range(DEC_SEQ):
            pj = jnp.where(vis[j], jnp.exp(s_new[j] - mx), 0.0)
            den = den + pj
            acc = acc + pj * new_values(wvn_ref, j)
        ow_ref[...] = _safe_div(acc, den)
        keep = (WINDOW - DEC_SEQ) * ng
        wko_ref[0:keep, :] = wk_ref[DEC_SEQ * ng:, :]
        wvo_ref[0:keep, :] = wv_ref[DEC_SEQ * ng:, :]
        wko_ref[keep:, :] = wkr_ref[...]
        wvo_ref[keep:, :] = wvr_ref[...]

    blk_per_page = PAGE_SIZE // SLC_BLOCK
    blk = lax.broadcasted_iota(jnp.int32, (LANE, page_w), 0)
    off = lax.broadcasted_iota(jnp.int32, (LANE, page_w), 1) // (SLC_BLOCK * ng)
    sel = sx_ref[...].reshape(rows, LANE).astype(BF16)
    own = own_group(page_w)
    s_parts, m_parts = [], []
    for j in range(n):
        bias = far_bias
        if j == n - 1:
            bias = jnp.where(p == n_steps - 1, b15_ref[...], bias)
        s_parts.append(_dot_nt(q_all, k_refs[j][...].astype(BF16)) * ATTN_SCALE + bias)
        to_cols = jnp.where(blk - off == (p * n + j) * blk_per_page, 1.0, 0.0).astype(BF16)
        m_parts.append((_dot(sel, to_cols) > 0.5) & own)

    def pv(pr):
        return sum(_dot(pr[:, j * page_w:(j + 1) * page_w], v_refs[j][...].astype(BF16)) for j in range(n))

    _flash_update(m_ref, l_ref, acc_ref, jnp.concatenate(s_parts, axis=1), pv, jnp.concatenate(m_parts, axis=1))

    @pl.when(p == n_steps - 1)
    def _():
        gate = jax.nn.sigmoid(gl_ref[...] + bg_ref[...].reshape(ng, 1, LANE))
        picked = stack(lambda g: sx_ref[g][:, N_PAST_BLK:N_PAST_BLK + 1]) > 0.5
        vis = [(tok_of_row >= j) & picked for j in range(DEC_SEQ)]
        s_new = [jnp.where(vis[j], sn, NEG_INF) for j, sn in enumerate(new_scores(kn_ref))]
        m_old = m_ref[...]
        m_new = m_old
        for sn in s_new:
            m_new = jnp.maximum(m_new, sn)
        alpha = jnp.exp(m_old - m_new)
        den = alpha * l_ref[...]
        acc = alpha * acc_ref[...]
        for j in range(DEC_SEQ):
            pj = jnp.where(vis[j], jnp.exp(s_new[j] - m_new), 0.0)
            den = den + pj
            acc = acc + pj * new_values(vn_ref, j)
        o_s = _safe_div(acc, den)
        o_w = ow_ref[...]
        for g in range(ng):
            rs = slice(g * ROWS, (g + 1) * ROWS)
            gt = [_per_head(r1, lambda r: gate[g][:, 3 * r + c:3 * r + c + 1]) for c in range(3)]
            o_ref[g] = gt[0] * oc_ref[g] + gt[1] * o_s[rs] + gt[2] * o_w[rs]


def nsa_sample(qg16, ps3, pool_k, pool_v, selx, buf_k, buf_v, o_c, gl16, b_gate, rel_bias, page_table):
    nb = qg16.shape[0]
    win_cols = slice(SELF_W + 4 * NSA_KV_W, SELF_W + 6 * NSA_KV_W)
    new_rows = ps3[:, :, win_cols].reshape(nb, DEC_SEQ, 2, NSA_KV_HEADS, HEAD_DIM)
    wk_rows = new_rows[:, :, 0].reshape(nb, DEC_SEQ * NSA_KV_HEADS, HEAD_DIM)
    wv_rows = new_rows[:, :, 1].reshape(nb, DEC_SEQ * NSA_KV_HEADS, HEAD_DIM)
    rspec = pl.BlockSpec((None, DEC_SEQ * NSA_KV_HEADS, HEAD_DIM), lambda b, p, pt: (b, 0, 0))
    n = PAGES_PER_STEP
    cb = lambda c: c // NSA_KV_W
    qspec = pl.BlockSpec((None, NSA_KV_HEADS, ROWS, HEAD_DIM), lambda b, p, pt: (b, 0, 0, 0))
    new = lambda c: pl.BlockSpec((None, DEC_SEQ, NSA_KV_W), lambda b, p, pt: (b, 0, cb(c)))
    page_w = PAGE_SIZE * NSA_KV_HEADS
    pool_k = pool_k.reshape(-1, page_w, HEAD_DIM)
    pool_v = pool_v.reshape(-1, page_w, HEAD_DIM)
    buf_k = buf_k.reshape(nb, WINDOW * NSA_KV_HEADS, HEAD_DIM)
    buf_v = buf_v.reshape(nb, WINDOW * NSA_KV_HEADS, HEAD_DIM)
    page = lambda j: pl.BlockSpec((None, page_w, HEAD_DIM), lambda b, p, pt: (pt[b, p * n + j], 0, 0))
    wspec = pl.BlockSpec((None, WINDOW * NSA_KV_HEADS, HEAD_DIM), lambda b, p, pt: (b, 0, 0))
    rows = NSA_KV_HEADS * ROWS
    grid_spec = pltpu.PrefetchScalarGridSpec(
        num_scalar_prefetch=1,
        grid=(nb, N_PAGES // n),
        in_specs=[pl.BlockSpec(memory_space=pltpu.SMEM), qspec]
        + [page(j) for j in range(n)] + [page(j) for j in range(n)] + [
            qspec,
            new(SELF_W + 2 * NSA_KV_W), new(SELF_W + 3 * NSA_KV_W),
            wspec, wspec,
            new(SELF_W + 4 * NSA_KV_W), new(SELF_W + 5 * NSA_KV_W),
            rspec, rspec,
            qspec, qspec,
            pl.BlockSpec((1, CROSS_W), lambda b, p, pt: (0, 0)),
        ],
        out_specs=[qspec, wspec, wspec],
        scratch_shapes=[pltpu.VMEM((rows, 1), F32), pltpu.VMEM((rows, 1), F32),
                        pltpu.VMEM((rows, HEAD_DIM), F32),
                        pltpu.VMEM((rows, HEAD_DIM), F32),
                        pltpu.VMEM((rows, page_w), F32),
                        pltpu.VMEM((NSA_KV_HEADS, ROWS, LANE), F32)],
    )
    o, win_k, win_v = pl.pallas_call(
        _nsa_sample_kernel,
        grid_spec=grid_spec,
        out_shape=[jax.ShapeDtypeStruct((nb, NSA_KV_HEADS, ROWS, HEAD_DIM), F32),
                   jax.ShapeDtypeStruct(buf_k.shape, F32), jax.ShapeDtypeStruct(buf_v.shape, F32)],
        compiler_params=_cparams(("arbitrary", "arbitrary")),
        name="nsa_sample",
    )(page_table, rel_bias, qg16, *([pool_k] * n), *([pool_v] * n), selx, ps3, ps3, buf_k, buf_v,
      ps3, ps3, wk_rows, wv_rows, o_c, gl16, b_gate)
    heads = (nb, WINDOW, NSA_KV_HEADS, HEAD_DIM)
    return o, win_k.reshape(heads), win_v.reshape(heads)


def _cross_sample_kernel(q_ref, mk_ref, mv_ref, o_ref):
    nh = N_CROSS_HEADS
    rows, width = nh * ROWS, MEM_LEN * nh
    q = q_ref[...].reshape(rows, HEAD_DIM).astype(BF16)
    s = _dot_nt(q, mk_ref[...].astype(BF16)) * ATTN_SCALE
    own = (lax.broadcasted_iota(jnp.int32, (rows, width), 1) % nh
           == lax.broadcasted_iota(jnp.int32, (rows, width), 0) // ROWS)
    p = _softmax_rows(s, own)
    o_ref[...] = _dot(p.astype(BF16), mv_ref[...].astype(BF16)).reshape(nh, ROWS, HEAD_DIM)


def cross_sample(qx16, mem_k, mem_v, layer):
    nb = qx16.shape[0]
    n_layers = mem_k.shape[0]
    mem_k = mem_k.reshape(n_layers, nb, MEM_LEN * N_CROSS_HEADS, HEAD_DIM)
    mem_v = mem_v.reshape(n_layers, nb, MEM_LEN * N_CROSS_HEADS, HEAD_DIM)
    qspec = pl.BlockSpec((None, N_CROSS_HEADS, ROWS, HEAD_DIM), lambda b: (b, 0, 0, 0))
    mspec = pl.BlockSpec((None, None, MEM_LEN * N_CROSS_HEADS, HEAD_DIM), lambda b: (layer, b, 0, 0))
    return pl.pallas_call(
        _cross_sample_kernel,
        grid=(nb,),
        in_specs=[qspec, mspec, mspec],
        out_specs=qspec,
        out_shape=jax.ShapeDtypeStruct((nb, N_CROSS_HEADS, ROWS, HEAD_DIM), F32),
        compiler_params=_cparams(("arbitrary",)),
        name="cross_sample",
    )(qx16, mem_k, mem_v)


def _arrange_in_weights(w, n_gate, per_group):
    gate = w[:, COL_QX:COL_QX + n_gate]
    if per_group:
        gate = jnp.pad(gate.reshape(-1, NSA_KV_HEADS, n_gate // NSA_KV_HEADS),
                       ((0, 0), (0, 0), (0, LANE - n_gate // NSA_KV_HEADS))).reshape(-1, CROSS_W)
    else:
        gate = jnp.pad(gate, ((0, 0), (0, CROSS_W - n_gate)))
    return jnp.concatenate([w[:, COL_QX + n_gate:], gate], axis=1)


def kernel(x_prompt, x_sample, mem_prompt, cache_nsa_cmp_k, cache_nsa_cmp_v, cache_nsa_slc_k, cache_nsa_slc_v, cache_nsa_win_k, cache_nsa_win_v, cache_fox_k, cache_fox_v, cache_fox_logf, cache_mem_k, cache_mem_v, page_table, rel_bias, norm_g, mem_norm_g, w_mem_kv, w_ff_gu, w_ff_down, w_in_nsa, b_gate_nsa, w_cmp1, w_cmp2, cmp_pe, w_out_nsa, w_in_fox, b_forget, w_out_fox, final_norm_g):
    n_p = BATCH * SEQ
    x = jnp.concatenate([x_prompt.reshape(n_p, D_MODEL), x_sample.reshape(-1, D_MODEL)], axis=0)
    mem = mem_prompt.reshape(BATCH * MEM_LEN, D_MODEL)
    n_pool = cache_nsa_cmp_k.shape[1]
    kv_heads = (BATCH, SEQ, NSA_KV_HEADS, HEAD_DIM)
    w_ff_gu = w_ff_gu.astype(BF16)
    w_ff_down = w_ff_down.astype(BF16)
    out = {}
    for i in range(DEPTH):
        x = ffn(x, norm_g[i, 0], w_ff_gu, w_ff_down, (i, 0))
        kvm = linear(mem, w_mem_kv[i], g=mem_norm_g[i], tm=512)
        out[f"mem_k{i}"] = kvm[:, :CROSS_W].reshape(BATCH, MEM_LEN, N_CROSS_HEADS, HEAD_DIM)
        out[f"mem_v{i}"] = kvm[:, CROSS_W:].reshape(BATCH, MEM_LEN, N_CROSS_HEADS, HEAD_DIM)
        a = i // 2
        if i % 2 == 0:
            w_tail = _arrange_in_weights(w_in_nsa[a], 3 * N_SELF_HEADS, True).astype(BF16)
            proj = linear(x, w_in_nsa[a].astype(BF16), w_tail, COL_QX // 512, g=norm_g[i, 1], tm=BIG_TOK_TILE)
            ps3 = proj[n_p:].reshape(DEC_BATCH, DEC_SEQ, PROJ_W)
            bg = jnp.pad(b_gate_nsa[a].reshape(NSA_KV_HEADS, 3 * NSA_GROUP),
                         ((0, 0), (0, LANE - 3 * NSA_GROUP))).reshape(1, CROSS_W)
            cmp_w = [(w_cmp1[a, c], w_cmp2[a, c], cmp_pe[a, c]) for c in range(2)]
            kvc = proj[:, SELF_W:SELF_W + 2 * NSA_KV_W]
            x_blk = kvc.reshape(N_TOK // CMP_BLOCK, CMP_BLOCK * 2 * NSA_KV_W)
            x_new = kvc.reshape(N_TOK // DEC_SEQ, DEC_SEQ * 2 * NSA_KV_W)
            pools = [cache_nsa_cmp_k[a], cache_nsa_cmp_v[a]]
            c_prompt, c_pool, c_new = [], [], []
            for c in range(2):
                col = lambda l, c=c: 2 * l + c
                c_prompt.append(compress(x_blk, col, CMP_BLOCK, *cmp_w[c], tr=BATCH * SEQ // CMP_BLOCK,
                                         rows=BATCH * SEQ // CMP_BLOCK))
                c_pool.append(compress_pool(pools[c], *cmp_w[c])
                              .reshape(n_pool, PAGE_SIZE // CMP_BLOCK, NSA_KV_W))
                c_new.append(compress(x_new, col, DEC_SEQ, *cmp_w[c], tr=DEC_BATCH, rows=DEC_BATCH,
                                      row_block0=n_p // DEC_SEQ // DEC_BATCH).reshape(DEC_BATCH, 1, NSA_KV_W))
            o_p, ox_p = nsa_prompt(proj, c_prompt[0], c_prompt[1], kvm, rel_bias, bg, BATCH)
            qg16 = _group_rows(ps3[:, :, :SELF_W])
            o_c, selx = nsa_sample_cmp(qg16, c_pool[0], c_pool[1], c_new[0], c_new[1], rel_bias, page_table)
            o_s, win_k, win_v = nsa_sample(
                qg16, ps3, cache_nsa_slc_k[a], cache_nsa_slc_v[a], selx,
                cache_nsa_win_k[a], cache_nsa_win_v[a],
                o_c, _gate_rows(ps3[:, :, COL_GATE:]), bg, rel_bias, page_table)
            o_s = _ungroup_rows(o_s)
            new_win = {"win_k": win_k, "win_v": win_v}
            for j, name in enumerate(("cmp_k", "cmp_v", "slc_k", "slc_v", "win_k", "win_v")):
                cols = slice(SELF_W + j * NSA_KV_W, SELF_W + (j + 1) * NSA_KV_W)
                st_p = proj[:n_p, cols].reshape(kv_heads)
                st_s = ps3[:, :, cols].reshape(DEC_BATCH, DEC_SEQ, NSA_KV_HEADS, HEAD_DIM)
                if name.startswith("win"):
                    st_p = st_p[:, -WINDOW:]
                    st_s = new_win[name]
                out.setdefault("p_" + name, []).append(st_p)
                out.setdefault("s_" + name, []).append(st_s)
            w_out = w_out_nsa[a]
        else:
            w_tail = _arrange_in_weights(w_in_fox[a], FOX_HEADS, False).astype(BF16)
            proj = linear(x, w_in_fox[a].astype(BF16), w_tail, COL_QX // 512, g=norm_g[i, 1], tm=BIG_TOK_TILE)
            ps3 = proj[n_p:].reshape(DEC_BATCH, DEC_SEQ, PROJ_W)
            lf_p, ct = fox_prep(proj, b_forget[a], BATCH)
            n_s = DEC_BATCH * DEC_SEQ
            lf_s, _ = fox_prep(proj, b_forget[a], 1, seq_len=n_s, row_block0=n_p // n_s)
            o_p = fox_prompt(proj, ct, BATCH)
            ox_p = cross_prompt(proj, kvm, BATCH)
            ct_pool = cumsum_lanes(jnp.transpose(cache_fox_logf[a], (2, 0, 1)))
            lfn = lf_s[:, :HEAD_PAD].reshape(DEC_BATCH, DEC_SEQ, HEAD_PAD).transpose(0, 2, 1)
            o_s = _unhead_rows(fox_sample(
                _head_rows(ps3[:, :, :SELF_W], FOX_HEADS), ps3,
                jnp.transpose(cache_fox_k[a], (0, 2, 1, 3)), jnp.transpose(cache_fox_v[a], (0, 2, 1, 3)),
                ct_pool, lfn, page_table))
            heads = (FOX_HEADS, HEAD_DIM)
            for j, name in enumerate(("fox_k", "fox_v")):
                cols = slice((j + 1) * SELF_W, (j + 2) * SELF_W)
                out.setdefault("p_" + name, []).append(proj[:n_p, cols].reshape(BATCH, SEQ, *heads))
                out.setdefault("s_" + name, []).append(ps3[:, :, cols].reshape(DEC_BATCH, DEC_SEQ, *heads))
            out.setdefault("p_fox_logf", []).append(lf_p[:, :FOX_HEADS].reshape(BATCH, SEQ, FOX_HEADS))
            out.setdefault("s_fox_logf", []).append(lf_s[:, :FOX_HEADS].reshape(DEC_BATCH, DEC_SEQ, FOX_HEADS))
            w_out = w_out_fox[a]
        ox_s = _unhead_rows(cross_sample(
            _head_rows(ps3[:, :, COL_QX:COL_GATE], N_CROSS_HEADS), cache_mem_k, cache_mem_v, i))
        mix_self = jnp.concatenate([o_p, o_s.astype(BF16)], axis=0)
        mix_cross = jnp.concatenate([ox_p, ox_s.astype(BF16)], axis=0)
        x = linear([mix_self, mix_cross], w_out.astype(BF16), res=x, tm=BIG_TOK_TILE)
        x = ffn(x, norm_g[i, 2], w_ff_gu, w_ff_down, (i, 1),
                final_g=final_norm_g if i == DEPTH - 1 else None)
    y_prompt = x[:n_p].reshape(BATCH, SEQ, D_MODEL)
    y_sample = x[n_p:].reshape(DEC_BATCH, DEC_SEQ, D_MODEL)
    st = lambda name: jnp.stack(out[name])
    p_mem_k = jnp.stack([out[f"mem_k{i}"] for i in range(DEPTH)])
    p_mem_v = jnp.stack([out[f"mem_v{i}"] for i in range(DEPTH)])
    return (y_prompt, y_sample,
            st("p_cmp_k"), st("p_cmp_v"), st("p_slc_k"), st("p_slc_v"), st("p_win_k"), st("p_win_v"),
            st("p_fox_k"), st("p_fox_v"), st("p_fox_logf"), p_mem_k, p_mem_v,
            st("s_cmp_k"), st("s_cmp_v"), st("s_slc_k"), st("s_slc_v"), st("s_win_k"), st("s_win_v"),
            st("s_fox_k"), st("s_fox_v"), st("s_fox_logf"))
```

```python
import functools
import math

import jax
import jax.numpy as jnp
from jax import lax
from jax.experimental import pallas as pl
from jax.experimental.pallas import tpu as pltpu

D_MODEL = 2048
BATCH = 4
SEQ = 2048
DEPTH = 2
DEC_BATCH = 128
DEC_SEQ = 4
PAST_LEN = 2048
PAGE_SIZE = 128
HEAD_DIM = 128
N_CROSS_HEADS = 4
N_SELF_HEADS = 12
NSA_KV_HEADS = 4
NSA_GROUP = N_SELF_HEADS // NSA_KV_HEADS
CMP_BLOCK = 64
SLC_BLOCK = 64
N_SEL = 16
WINDOW = 512
CMP_HIDDEN = 256
FOX_HEADS = N_SELF_HEADS
MEM_LEN = 256
N_BUCKETS = 32
MAX_DISTANCE = 128
FFN_DIM = ((8 * D_MODEL) // 3 + 127) // 128 * 128
RMS_EPS = 1e-6
FORCE_SCORE = 1e4
NEG_INF = -1e30
ATTN_SCALE = HEAD_DIM ** -0.5
SELF_W = N_SELF_HEADS * HEAD_DIM
NSA_KV_W = NSA_KV_HEADS * HEAD_DIM
CROSS_W = N_CROSS_HEADS * HEAD_DIM
N_PAGES = PAST_LEN // PAGE_SIZE
N_PAST_BLK = PAST_LEN // SLC_BLOCK

LANE = 128
VMEM_LIMIT = 56 * 1024 * 1024
N_TOK = BATCH * SEQ + DEC_BATCH * DEC_SEQ
TOK_TILE = 544
BIG_TOK_TILE = 2 * TOK_TILE
Q_TILE = 128
COL_QX = 3 * SELF_W
COL_GATE = COL_QX + CROSS_W
PROJ_W = COL_GATE + CROSS_W

F32 = jnp.float32
BF16 = jnp.bfloat16
_NT = (((1,), (1,)), ((), ()))


def _cparams(sem):
    return pltpu.CompilerParams(dimension_semantics=sem, vmem_limit_bytes=VMEM_LIMIT)


def _rms(x, g):
    return x * lax.rsqrt(jnp.mean(x * x, axis=-1, keepdims=True) + RMS_EPS) * g


def _dot(a, b):
    return jnp.dot(a, b, preferred_element_type=F32)


def _dot_nt(a, b):
    return lax.dot_general(a, b, _NT, preferred_element_type=F32)


def _t5_bias(dist, rb_ref, h):
    n = jnp.maximum(dist, 0)
    max_exact = N_BUCKETS // 2
    nf = jnp.maximum(n, 1).astype(F32)
    large = max_exact + (jnp.log(nf / max_exact) / math.log(MAX_DISTANCE / max_exact)
                         * (N_BUCKETS - max_exact)).astype(jnp.int32)
    bucket = jnp.where(n <= max_exact, n, jnp.minimum(large, N_BUCKETS - 1))
    out = jnp.zeros(dist.shape, F32)
    for b in range(N_BUCKETS):
        out = jnp.where(bucket == b, rb_ref[b, h], out)
    return out


def _softmax_rows(s, mask):
    s = jnp.where(mask, s, NEG_INF)
    e = jnp.where(mask, jnp.exp(s - jnp.max(s, axis=-1, keepdims=True)), 0.0)
    den = jnp.sum(e, axis=-1, keepdims=True)
    return e / jnp.where(den > 0.0, den, 1.0)


def _select_blocks(imp, cur, col, n_blocks):
    forced = (col == 0) | (col == cur) | (col == cur - 1)
    score = jnp.where(col <= cur, jnp.where(forced, FORCE_SCORE, imp), -1.0)
    rank = jnp.zeros(score.shape, jnp.int32)
    for i in range(n_blocks):
        ci = score[:, i:i + 1]
        beats = (ci > score) | ((ci == score) & (col > i))
        rank = rank + beats.astype(jnp.int32)
    return (rank < N_SEL) & (col <= cur)


class _Flash:
    def __init__(self, m_ref, l_ref, acc_ref):
        self.m, self.l, self.acc = m_ref, l_ref, acc_ref

    def reset(self):
        self.m[...] = jnp.full(self.m.shape, NEG_INF, F32)
        self.l[...] = jnp.zeros(self.l.shape, F32)
        self.acc[...] = jnp.zeros(self.acc.shape, F32)

    def update(self, s, v, mask=None):
        if mask is not None:
            s = jnp.where(mask, s, NEG_INF)
        m_old = self.m[...]
        m_new = jnp.maximum(m_old, jnp.max(s, axis=-1, keepdims=True))
        alpha = jnp.exp(m_old - m_new)
        p = jnp.exp(s - m_new)
        if mask is not None:
            p = jnp.where(mask, p, 0.0)
        self.l[...] = alpha * self.l[...] + jnp.sum(p, axis=-1, keepdims=True)
        self.acc[...] = alpha * self.acc[...] + _dot(p.astype(BF16), v)
        self.m[...] = m_new

    def result(self):
        return self.acc[...] / self.l[...]


FFN_TILES = 4


def _ffn_kernel(x_ref, g_ref, fg_ref, *refs, n_tiles, n_steps, final_norm):
    t = FFN_TILES
    wg, wu, wd = refs[:t], refs[t:2 * t], refs[2 * t:3 * t]
    o_ref, h_ref = refs[3 * t:]
    f = pl.program_id(1)

    @pl.when(f == 0)
    def _():
        x = x_ref[...]
        h_ref[...] = _rms(x, g_ref[...]).astype(BF16)
        o_ref[...] = x

    h = h_ref[...]
    a = _dot(h, jnp.concatenate([r[...].astype(BF16) for r in wg], axis=1))
    b = _dot(h, jnp.concatenate([r[...].astype(BF16) for r in wu], axis=1))
    act = a * jax.nn.sigmoid(a) * b
    tile = t * f + lax.broadcasted_iota(jnp.int32, act.shape, 1) // LANE
    act = jnp.where(tile < n_tiles, act, 0.0)
    w_down = jnp.concatenate([r[...].astype(BF16) for r in wd], axis=0)
    o_ref[...] += 0.5 * _dot(act.astype(BF16), w_down)

    if final_norm:
        @pl.when(f == n_steps - 1)
        def _():
            o_ref[...] = _rms(o_ref[...], fg_ref[...])


def ffn(x, g, w_gu, w_down, which, final_g=None):
    m, d = x.shape
    n_f = FFN_DIM // LANE
    n_steps = pl.cdiv(n_f, FFN_TILES)
    tm = TOK_TILE
    final_norm = final_g is not None
    fg = final_g if final_norm else g

    def col(off, k):
        return lambda i, f: (*which, 0, off + jnp.minimum(FFN_TILES * f + k, n_f - 1))

    def row(k):
        return lambda i, f: (*which, jnp.minimum(FFN_TILES * f + k, n_f - 1), 0)

    ks = range(FFN_TILES)
    return pl.pallas_call(
        functools.partial(_ffn_kernel, n_tiles=n_f, n_steps=n_steps, final_norm=final_norm),
        grid=(m // tm, n_steps),
        in_specs=[
            pl.BlockSpec((tm, d), lambda i, f: (i, 0)),
            pl.BlockSpec((1, d), lambda i, f: (0, 0)),
            pl.BlockSpec((1, d), lambda i, f: (0, 0)),
        ] + [pl.BlockSpec((None, None, d, LANE), col(0, k)) for k in ks]
        + [pl.BlockSpec((None, None, d, LANE), col(n_f, k)) for k in ks]
        + [pl.BlockSpec((None, None, LANE, d), row(k)) for k in ks],
        out_specs=pl.BlockSpec((tm, d), lambda i, f: (i, 0)),
        out_shape=jax.ShapeDtypeStruct((m, d), F32),
        scratch_shapes=[pltpu.VMEM((tm, d), BF16)],
        compiler_params=_cparams(("parallel", "arbitrary")),
        name="ffn",
    )(x, g.reshape(1, d), fg.reshape(1, d), *([w_gu] * (2 * FFN_TILES)), *([w_down] * FFN_TILES))


def _linear_kernel(*refs, n_x, norm, n_main, residual):
    x_refs = refs[:n_x]
    refs = refs[n_x:]
    if norm:
        g_ref, refs = refs[0], refs[1:]
    w_ref, wt_ref = refs[:2]
    refs = refs[2:]
    if residual:
        r_ref, o_ref, h_ref = refs
    else:
        o_ref, h_ref = refs
    j = pl.program_id(1)

    @pl.when(j == 0)
    def _():
        off = 0
        for x_ref in x_refs:
            x = x_ref[...].astype(F32)
            if norm:
                x = _rms(x, g_ref[...])
            h_ref[:, off:off + x.shape[1]] = x.astype(BF16)
            off += x.shape[1]

    def emit(w):
        y = _dot(h_ref[...], w[...].astype(BF16))
        if residual:
            y = y + r_ref[...]
        o_ref[...] = y

    pl.when(j < n_main)(lambda: emit(w_ref))
    pl.when(j >= n_main)(lambda: emit(wt_ref))


def linear(xs, w, w_tail=None, n_main=None, g=None, res=None, tm=TOK_TILE, tn=512):
    xs = list(xs) if isinstance(xs, (list, tuple)) else [xs]
    m = xs[0].shape[0]
    k = sum(x.shape[1] for x in xs)
    if w_tail is None:
        w_tail, n_main = w, w.shape[1] // tn
    n_blocks = n_main + (w_tail.shape[1] // tn if w_tail is not w else 0)
    norm, residual = g is not None, res is not None
    in_specs = [pl.BlockSpec((tm, x.shape[1]), lambda i, j: (i, 0)) for x in xs]
    args = list(xs)
    if norm:
        in_specs.append(pl.BlockSpec((1, k), lambda i, j: (0, 0)))
        args.append(g.reshape(1, k))
    in_specs.append(pl.BlockSpec((k, tn), lambda i, j: (0, jnp.minimum(j, n_main - 1))))
    in_specs.append(pl.BlockSpec((k, tn), lambda i, j: (0, jnp.maximum(j - n_main, 0))))
    args += [w, w_tail]
    if residual:
        in_specs.append(pl.BlockSpec((tm, tn), lambda i, j: (i, j)))
        args.append(res)
    return pl.pallas_call(
        functools.partial(_linear_kernel, n_x=len(xs), norm=norm, n_main=n_main, residual=residual),
        grid=(m // tm, n_blocks),
        in_specs=in_specs,
        out_specs=pl.BlockSpec((tm, tn), lambda i, j: (i, j)),
        out_shape=jax.ShapeDtypeStruct((m, n_blocks * tn), F32),
        scratch_shapes=[pltpu.VMEM((tm, k), BF16)],
        compiler_params=_cparams(("parallel", "arbitrary")),
        name="linear",
    )(*args)


def _compress_kernel(x_ref, pe_ref, w1_ref, w2_ref, o_ref, acc_ref, *, n_valid):
    l = pl.program_id(1)

    @pl.when(l == 0)
    def _():
        acc_ref[...] = jnp.zeros(acc_ref.shape, F32)

    pe = pe_ref[pl.ds(l, 1), :]
    w1 = w1_ref[...].astype(BF16)
    for g in range(NSA_KV_HEADS):
        xg = x_ref[:, g * HEAD_DIM:(g + 1) * HEAD_DIM]
        if n_valid < CMP_BLOCK:
            xg = jnp.where(l < n_valid, xg, 0.0)
        acc_ref[g] += _dot((xg + pe).astype(BF16), w1)

    @pl.when(l == CMP_BLOCK - 1)
    def _():
        w2 = w2_ref[...].astype(BF16)
        for g in range(NSA_KV_HEADS):
            a = acc_ref[g]
            hid = a * jax.nn.sigmoid(a)
            o_ref[:, g * HEAD_DIM:(g + 1) * HEAD_DIM] = _dot(hid.astype(BF16), w2)


def compress(x2d, col_block, n_valid, w1, w2, pe, tr, rows=None, row_block0=0):
    rows = x2d.shape[0] if rows is None else rows
    x_spec = pl.BlockSpec((tr, NSA_KV_W),
                          lambda i, l: (row_block0 + i, col_block(jnp.minimum(l, n_valid - 1))))
    return pl.pallas_call(
        functools.partial(_compress_kernel, n_valid=n_valid),
        grid=(rows // tr, CMP_BLOCK),
        in_specs=[
            x_spec,
            pl.BlockSpec((CMP_BLOCK, HEAD_DIM), lambda i, l: (0, 0)),
            pl.BlockSpec((HEAD_DIM, CMP_HIDDEN), lambda i, l: (l, 0)),
            pl.BlockSpec((CMP_HIDDEN, HEAD_DIM), lambda i, l: (0, 0)),
        ],
        out_specs=pl.BlockSpec((tr, NSA_KV_W), lambda i, l: (i, 0)),
        out_shape=jax.ShapeDtypeStruct((rows, NSA_KV_W), F32),
        scratch_shapes=[pltpu.VMEM((NSA_KV_HEADS, tr, CMP_HIDDEN), F32)],
        compiler_params=_cparams(("parallel", "arbitrary")),
        name="compress",
    )(x2d, pe, w1, w2)


def _compress_pool_kernel(x_ref, pe_ref, w1a_ref, w1b_ref, w2_ref, o_ref, acc_ref):
    lp = pl.program_id(1)
    tr, sub, _ = x_ref.shape
    half = sub // 2

    @pl.when(lp == 0)
    def _():
        acc_ref[...] = jnp.zeros(acc_ref.shape, F32)

    pe2 = pe_ref[pl.ds(pl.multiple_of(2 * lp, 2), 2), :]
    odd8 = lax.broadcasted_iota(jnp.int32, (sub, HEAD_DIM), 0) >= half
    pe8 = jnp.where(odd8, pe2[1:2, :], pe2[0:1, :])
    x = (x_ref[...] + pe8[None]).reshape(tr * sub, HEAD_DIM).astype(BF16)
    odd = lax.broadcasted_iota(jnp.int32, (tr * sub, HEAD_DIM), 0) % sub >= half
    zero = jnp.zeros_like(x)
    x2 = jnp.concatenate([jnp.where(odd, zero, x), jnp.where(odd, x, zero)], axis=1)
    w = jnp.concatenate([w1a_ref[...].astype(BF16), w1b_ref[...].astype(BF16)], axis=0)
    acc_ref[...] += _dot(x2, w)

    @pl.when(lp == CMP_BLOCK // 2 - 1)
    def _():
        a = acc_ref[...]
        a = a + pltpu.roll(a, half, 0)
        hid = a * jax.nn.sigmoid(a)
        o_ref[...] = _dot(hid.astype(BF16), w2_ref[...].astype(BF16)).reshape(tr, sub, HEAD_DIM)


def compress_pool(pool, w1, w2, pe, tr=512):
    n_blk = pool.shape[0] * PAGE_SIZE // CMP_BLOCK
    sub = 2 * NSA_KV_HEADS
    x = pool.reshape(n_blk, CMP_BLOCK * NSA_KV_HEADS, HEAD_DIM)
    out = pl.pallas_call(
        _compress_pool_kernel,
        grid=(n_blk // tr, CMP_BLOCK // 2),
        in_specs=[
            pl.BlockSpec((tr, sub, HEAD_DIM), lambda i, l: (i, l, 0)),
            pl.BlockSpec((CMP_BLOCK, HEAD_DIM), lambda i, l: (0, 0)),
            pl.BlockSpec((HEAD_DIM, CMP_HIDDEN), lambda i, l: (2 * l, 0)),
            pl.BlockSpec((HEAD_DIM, CMP_HIDDEN), lambda i, l: (2 * l + 1, 0)),
            pl.BlockSpec((CMP_HIDDEN, HEAD_DIM), lambda i, l: (0, 0)),
        ],
        out_specs=pl.BlockSpec((tr, sub, HEAD_DIM), lambda i, l: (i, 0, 0)),
        out_shape=jax.ShapeDtypeStruct((n_blk, sub, HEAD_DIM), F32),
        scratch_shapes=[pltpu.VMEM((tr * sub, CMP_HIDDEN), F32)],
        compiler_params=_cparams(("parallel", "arbitrary")),
        name="compress_pool",
    )(x, pe, w1, w1, w2)
    return out[:, NSA_KV_HEADS:, :].reshape(n_blk, NSA_KV_W)


def _nsa_prompt_kernel(rb_ref, q_ref, ks_ref, vs_ref, kw_ref, vw_ref, ck_ref, cv_ref, gl_ref, bg_ref,
                       qx_ref, mk_ref, mv_ref, o_ref, ox_ref, bt_ref, cb_ref, selx_ref, m_ref, l_ref, acc_ref):
    g = pl.program_id(1)
    qt = pl.program_id(2)
    q0 = qt * Q_TILE
    ii = lax.broadcasted_iota(jnp.int32, (Q_TILE, LANE), 0)
    jj = lax.broadcasted_iota(jnp.int32, (Q_TILE, LANE), 1)

    @pl.when(qt == 0)
    def _():
        d_new = (lax.broadcasted_iota(jnp.int32, (Q_TILE, 1), 0) + 1) % CMP_BLOCK
        for r in range(NSA_GROUP):
            h = NSA_GROUP * g + r
            bt_ref[r, 0] = _t5_bias(ii - jj, rb_ref, h)
            bt_ref[r, 1] = _t5_bias(LANE + ii - jj, rb_ref, h)
            cb_ref[r, 0] = _t5_bias(d_new, rb_ref, h)
            cb_ref[r, 1] = _t5_bias(d_new + CMP_BLOCK, rb_ref, h)

    def heads(f):
        return jnp.concatenate([f(r) for r in range(NSA_GROUP)], axis=0)

    qs = heads(lambda r: q_ref[:, r * HEAD_DIM:(r + 1) * HEAD_DIM]).astype(BF16)
    far_bias = heads(lambda r: jnp.full((Q_TILE, 1), rb_ref[N_BUCKETS - 1, NSA_GROUP * g + r], F32))
    diag_bias = heads(lambda r: bt_ref[r, 0])
    near_bias = heads(lambda r: bt_ref[r, 1])
    causal = jnp.concatenate([jj <= ii] * NSA_GROUP, axis=0)

    nb = ck_ref.shape[0]
    row = lax.broadcasted_iota(jnp.int32, (Q_TILE, nb), 0)
    col = lax.broadcasted_iota(jnp.int32, (Q_TILE, nb), 1)
    qpos = q0 + row
    dist = qpos - (col * CMP_BLOCK + (CMP_BLOCK - 1))
    sc = _dot_nt(qs, ck_ref[...].astype(BF16)) * ATTN_SCALE
    newest = (qpos - (CMP_BLOCK - 1)) // CMP_BLOCK
    probs = []
    for r in range(NSA_GROUP):
        bias = jnp.where(col == newest, cb_ref[r, 0],
                         jnp.where(col == newest - 1, cb_ref[r, 1], rb_ref[N_BUCKETS - 1, NSA_GROUP * g + r]))
        probs.append(_softmax_rows(sc[r * Q_TILE:(r + 1) * Q_TILE] + bias, dist >= 0))
    o_c = _dot(jnp.concatenate(probs, axis=0).astype(BF16), cv_ref[...].astype(BF16))
    imp = probs[0] + probs[1] + probs[2]
    sel = jnp.where(_select_blocks(imp, qpos // SLC_BLOCK, col, nb), 1.0, 0.0).astype(BF16)
    blk_minus_off = (lax.broadcasted_iota(jnp.int32, (nb, LANE), 0)
                     - lax.broadcasted_iota(jnp.int32, (nb, LANE), 1) // SLC_BLOCK)
    for t in range(selx_ref.shape[0]):
        to_keys = jnp.where(blk_minus_off == t * (LANE // SLC_BLOCK), 1.0, 0.0).astype(BF16)
        selx_ref[t] = _dot(sel, to_keys)

    every = jnp.concatenate([jj >= 0] * NSA_GROUP, axis=0)

    def key_tile(ref, t):
        return ref[pl.ds(pl.multiple_of(jnp.maximum(t, 0) * LANE, LANE), LANE), :].astype(BF16)

    def scores(k_ref, t, bias):
        return _dot_nt(qs, key_tile(k_ref, t)) * ATTN_SCALE + bias

    def sel_mask(t):
        return jnp.concatenate([selx_ref[jnp.maximum(t, 0)] > 0.5] * NSA_GROUP, axis=0)

    def pair_update(s0, s1, m0, m1, v_ref, t0, t1):
        def pv(p):
            return _dot(p[:, :LANE], key_tile(v_ref, t0)) + _dot(p[:, LANE:], key_tile(v_ref, t1))
        _flash_update(m_ref, l_ref, acc_ref, jnp.concatenate([s0, s1], axis=1), pv,
                      jnp.concatenate([m0, m1], axis=1), mxu_sum=True)

    m_ref[...] = jnp.full(m_ref.shape, NEG_INF, F32)
    l_ref[...] = jnp.zeros(l_ref.shape, F32)
    acc_ref[...] = jnp.zeros(acc_ref.shape, F32)
    n_far = jnp.maximum(qt - 1, 0)

    def far(it, carry):
        t0 = 2 * it
        t1 = jnp.minimum(t0 + 1, n_far - 1)
        pair_update(scores(ks_ref, t0, far_bias), scores(ks_ref, t1, far_bias),
                    sel_mask(t0), sel_mask(t1) & (t0 + 1 < n_far), vs_ref, t0, t1)
        return carry

    lax.fori_loop(0, (n_far + 1) // 2, far, 0)
    pair_update(scores(ks_ref, qt - 1, near_bias), scores(ks_ref, qt, diag_bias),
                sel_mask(qt - 1) & (qt >= 1), sel_mask(qt) & causal, vs_ref, qt - 1, qt)
    o_s = acc_ref[...] / l_ref[...]

    n_win = WINDOW // LANE
    upper = jnp.concatenate([jj > ii] * NSA_GROUP, axis=0)
    band_s, band_m = [], []
    for back in range(n_win, -1, -1):
        bias = diag_bias if back == 0 else near_bias if back == 1 else far_bias
        base = causal if back == 0 else upper if back == n_win else every
        band_s.append(scores(kw_ref, qt - back, bias))
        band_m.append(base & (qt >= back))
    pw = _softmax_rows(jnp.concatenate(band_s, axis=1), jnp.concatenate(band_m, axis=1)).astype(BF16)
    o_w = sum(_dot(pw[:, i * LANE:(i + 1) * LANE], key_tile(vw_ref, qt - back))
              for i, back in enumerate(range(n_win, -1, -1)))

    gate = jax.nn.sigmoid(gl_ref[...] + bg_ref[...])
    for r in range(NSA_GROUP):
        rows = slice(r * Q_TILE, (r + 1) * Q_TILE)
        o = (gate[:, 3 * r:3 * r + 1] * o_c[rows] + gate[:, 3 * r + 1:3 * r + 2] * o_s[rows]
             + gate[:, 3 * r + 2:3 * r + 3] * o_w[rows])
        o_ref[:, r * HEAD_DIM:(r + 1) * HEAD_DIM] = o.astype(o_ref.dtype)

    sx = _dot_nt(qx_ref[...].astype(BF16), mk_ref[...].astype(BF16)) * ATTN_SCALE
    px = _softmax_rows(sx, sx == sx)
    ox_ref[...] = _dot(px.astype(BF16), mv_ref[...].astype(BF16)).astype(ox_ref.dtype)


def nsa_prompt(proj, ck, cv, kvm, rel_bias, b_gate, batch):
    nq = SEQ // Q_TILE
    nb = SEQ // CMP_BLOCK
    hb = lambda c: c // HEAD_DIM
    qrow = lambda b, g, t: b * nq + t
    grid_spec = pltpu.PrefetchScalarGridSpec(
        num_scalar_prefetch=0,
        grid=(batch, NSA_KV_HEADS, nq),
        in_specs=[
            pl.BlockSpec(memory_space=pltpu.SMEM),
            pl.BlockSpec((Q_TILE, NSA_GROUP * HEAD_DIM), lambda b, g, t: (qrow(b, g, t), g)),
            pl.BlockSpec((SEQ, HEAD_DIM), lambda b, g, t: (b, hb(SELF_W + 2 * NSA_KV_W) + g)),
            pl.BlockSpec((SEQ, HEAD_DIM), lambda b, g, t: (b, hb(SELF_W + 3 * NSA_KV_W) + g)),
            pl.BlockSpec((SEQ, HEAD_DIM), lambda b, g, t: (b, hb(SELF_W + 4 * NSA_KV_W) + g)),
            pl.BlockSpec((SEQ, HEAD_DIM), lambda b, g, t: (b, hb(SELF_W + 5 * NSA_KV_W) + g)),
            pl.BlockSpec((nb, HEAD_DIM), lambda b, g, t: (b, g)),
            pl.BlockSpec((nb, HEAD_DIM), lambda b, g, t: (b, g)),
            pl.BlockSpec((Q_TILE, LANE), lambda b, g, t: (qrow(b, g, t), hb(COL_GATE) + g)),
            pl.BlockSpec((1, LANE), lambda b, g, t: (0, g)),
            pl.BlockSpec((Q_TILE, HEAD_DIM), lambda b, g, t: (qrow(b, g, t), hb(COL_QX) + g)),
            pl.BlockSpec((MEM_LEN, HEAD_DIM), lambda b, g, t: (b, g)),
            pl.BlockSpec((MEM_LEN, HEAD_DIM), lambda b, g, t: (b, N_CROSS_HEADS + g)),
        ],
        out_specs=[
            pl.BlockSpec((Q_TILE, NSA_GROUP * HEAD_DIM), lambda b, g, t: (qrow(b, g, t), g)),
            pl.BlockSpec((Q_TILE, HEAD_DIM), lambda b, g, t: (qrow(b, g, t), g)),
        ],
        scratch_shapes=[
            pltpu.VMEM((NSA_GROUP, 2, Q_TILE, LANE), F32),
            pltpu.VMEM((NSA_GROUP, 2, Q_TILE, 1), F32),
            pltpu.VMEM((SEQ // LANE, Q_TILE, LANE), F32),
            pltpu.VMEM((NSA_GROUP * Q_TILE, 1), F32),
            pltpu.VMEM((NSA_GROUP * Q_TILE, 1), F32),
            pltpu.VMEM((NSA_GROUP * Q_TILE, HEAD_DIM), F32),
        ],
    )
    return pl.pallas_call(
        _nsa_prompt_kernel,
        grid_spec=grid_spec,
        out_shape=[jax.ShapeDtypeStruct((batch * SEQ, SELF_W), BF16),
                   jax.ShapeDtypeStruct((batch * SEQ, CROSS_W), BF16)],
        compiler_params=_cparams(("arbitrary", "arbitrary", "arbitrary")),
        name="nsa_prompt",
    )(rel_bias, proj, proj, proj, proj, proj, ck, cv, proj, b_gate, proj, kvm, kvm)


FOX_TILE = 256
HEAD_PAD = 16


def _log_sigmoid(x):
    return jnp.minimum(x, 0.0) - jnp.log1p(jnp.exp(-jnp.abs(x)))


def _fox_prep_kernel(fl_ref, bf_ref, lf_ref, ct_ref):
    n_chunks = fl_ref.shape[0] // LANE
    ii = lax.broadcasted_iota(jnp.int32, (LANE, LANE), 0)
    jj = lax.broadcasted_iota(jnp.int32, (LANE, LANE), 1)
    tri = jnp.where(jj <= ii, 1.0, 0.0).astype(F32)
    carry = jnp.zeros((1, LANE), F32)
    per_tile = FOX_TILE // LANE
    for c in range(n_chunks):
        lf = _log_sigmoid(fl_ref[c * LANE:(c + 1) * LANE, :] + bf_ref[...])
        lf_ref[c * LANE:(c + 1) * LANE, :] = lf
        cs = jnp.dot(tri, lf, precision=lax.Precision.HIGHEST, preferred_element_type=F32) + carry
        carry = cs[LANE - 1:LANE, :]
        ct_ref[:, c // per_tile, (c % per_tile) * LANE:(c % per_tile + 1) * LANE] = cs.T[:HEAD_PAD]


def fox_prep(proj, b_forget, batch, seq_len=SEQ, row_block0=0):
    bf = jnp.pad(b_forget.reshape(1, FOX_HEADS), ((0, 0), (0, LANE - FOX_HEADS)))
    n_tiles = max(seq_len // FOX_TILE, 1)
    return pl.pallas_call(
        _fox_prep_kernel,
        grid=(batch,),
        in_specs=[pl.BlockSpec((seq_len, LANE), lambda b: (row_block0 + b, COL_GATE // LANE)),
                  pl.BlockSpec((1, LANE), lambda b: (0, 0))],
        out_specs=[pl.BlockSpec((seq_len, LANE), lambda b: (b, 0)),
                   pl.BlockSpec((None, HEAD_PAD, n_tiles, FOX_TILE), lambda b: (b, 0, 0, 0))],
        out_shape=[jax.ShapeDtypeStruct((batch * seq_len, LANE), F32),
                   jax.ShapeDtypeStruct((batch, HEAD_PAD, n_tiles, FOX_TILE), F32)],
        compiler_params=_cparams(("arbitrary",)),
        name="fox_prep",
    )(proj, bf)


FOX_HEAD_BLOCK = 4


def _fox_prompt_kernel(q_ref, k_ref, v_ref, c_ref, o_ref, m_ref, l_ref, acc_ref):
    qt = pl.program_id(2)
    nh = FOX_HEAD_BLOCK
    hs = [slice(h * HEAD_DIM, (h + 1) * HEAD_DIM) for h in range(nh)]
    q = [q_ref[:, s].astype(BF16) for s in hs]
    m_ref[...] = jnp.full(m_ref.shape, NEG_INF, F32)
    l_ref[...] = jnp.zeros(l_ref.shape, F32)
    acc_ref[...] = jnp.zeros(acc_ref.shape, F32)

    def tile(t, mask):
        rows = pl.ds(pl.multiple_of(t * FOX_TILE, FOX_TILE), FOX_TILE)
        s = jnp.concatenate([_dot_nt(q[h], k_ref[rows, hs[h]].astype(BF16)) * ATTN_SCALE
                             - c_ref[h, pl.ds(t, 1), :] for h in range(nh)], axis=0)

        def pv(p):
            return jnp.concatenate([_dot(p[h * FOX_TILE:(h + 1) * FOX_TILE], v_ref[rows, hs[h]].astype(BF16))
                                    for h in range(nh)], axis=0)

        _flash_update(m_ref, l_ref, acc_ref, s, pv, mask, mxu_sum=True)

    def far(t, carry):
        tile(t, None)
        return carry

    lax.fori_loop(0, qt, far, 0)
    ii = lax.broadcasted_iota(jnp.int32, (nh * FOX_TILE, FOX_TILE), 0) % FOX_TILE
    jj = lax.broadcasted_iota(jnp.int32, (nh * FOX_TILE, FOX_TILE), 1)
    tile(qt, jj <= ii)
    o = acc_ref[...] / l_ref[...]
    for h in range(nh):
        o_ref[:, hs[h]] = o[h * FOX_TILE:(h + 1) * FOX_TILE].astype(o_ref.dtype)


def fox_prompt(proj, ct, batch):
    nq = SEQ // FOX_TILE
    nh = FOX_HEAD_BLOCK
    wb = nh * HEAD_DIM
    cb = lambda c: c // wb
    rows = nh * FOX_TILE
    return pl.pallas_call(
        _fox_prompt_kernel,
        grid=(batch, FOX_HEADS // nh, nq),
        in_specs=[
            pl.BlockSpec((FOX_TILE, wb), lambda b, h, t: (b * nq + t, h)),
            pl.BlockSpec((SEQ, wb), lambda b, h, t: (b, cb(SELF_W) + h)),
            pl.BlockSpec((SEQ, wb), lambda b, h, t: (b, cb(2 * SELF_W) + h)),
            pl.BlockSpec((None, nh, nq, FOX_TILE), lambda b, h, t: (b, h, 0, 0)),
        ],
        out_specs=pl.BlockSpec((FOX_TILE, wb), lambda b, h, t: (b * nq + t, h)),
        out_shape=jax.ShapeDtypeStruct((batch * SEQ, SELF_W), BF16),
        scratch_shapes=[pltpu.VMEM((rows, 1), F32), pltpu.VMEM((rows, 1), F32),
                        pltpu.VMEM((rows, HEAD_DIM), F32)],
        compiler_params=_cparams(("arbitrary", "arbitrary", "arbitrary")),
        name="fox_prompt",
    )(proj, proj, proj, ct)


def _cross_prompt_kernel(q_ref, mk_ref, mv_ref, o_ref):
    for h in range(N_CROSS_HEADS):
        hs = slice(h * HEAD_DIM, (h + 1) * HEAD_DIM)
        s = _dot_nt(q_ref[:, hs].astype(BF16), mk_ref[:, hs].astype(BF16)) * ATTN_SCALE
        p = _softmax_rows(s, s == s)
        o_ref[:, hs] = _dot(p.astype(BF16), mv_ref[:, hs].astype(BF16)).astype(o_ref.dtype)


def cross_prompt(proj, kvm, batch, tq=512):
    nq = SEQ // tq
    return pl.pallas_call(
        _cross_prompt_kernel,
        grid=(batch, nq),
        in_specs=[pl.BlockSpec((tq, CROSS_W), lambda b, t: (b * nq + t, COL_QX // CROSS_W)),
                  pl.BlockSpec((MEM_LEN, CROSS_W), lambda b, t: (b, 0)),
                  pl.BlockSpec((MEM_LEN, CROSS_W), lambda b, t: (b, 1))],
        out_specs=pl.BlockSpec((tq, CROSS_W), lambda b, t: (b * nq + t, 0)),
        out_shape=jax.ShapeDtypeStruct((batch * SEQ, CROSS_W), BF16),
        compiler_params=_cparams(("arbitrary", "arbitrary")),
        name="cross_prompt",
    )(proj, kvm, kvm)


ROWS = 16


def _pad_rows(x):
    return jnp.pad(x, [(0, 0)] * (x.ndim - 2) + [(0, ROWS - x.shape[-2]), (0, 0)])


def _group_rows(q):
    b = q.shape[0]
    q = q.reshape(b, DEC_SEQ, NSA_KV_HEADS, NSA_GROUP, HEAD_DIM).transpose(0, 2, 3, 1, 4)
    return _pad_rows(q.reshape(b, NSA_KV_HEADS, NSA_GROUP * DEC_SEQ, HEAD_DIM))


def _ungroup_rows(o):
    b = o.shape[0]
    o = o[:, :, :NSA_GROUP * DEC_SEQ].reshape(b, NSA_KV_HEADS, NSA_GROUP, DEC_SEQ, HEAD_DIM)
    return o.transpose(0, 3, 1, 2, 4).reshape(b * DEC_SEQ, SELF_W)


def _gate_rows(gl):
    b = gl.shape[0]
    gl = gl.reshape(b, DEC_SEQ, NSA_KV_HEADS, LANE).transpose(0, 2, 1, 3)
    return _pad_rows(jnp.tile(gl, (1, 1, NSA_GROUP, 1)))


def _head_rows(q, n_heads):
    b = q.shape[0]
    return _pad_rows(q.reshape(b, DEC_SEQ, n_heads, HEAD_DIM).transpose(0, 2, 1, 3))


def _unhead_rows(o):
    b, h = o.shape[:2]
    return o[:, :, :DEC_SEQ].transpose(0, 2, 1, 3).reshape(b * DEC_SEQ, h * HEAD_DIM)


def _row_ids():
    row = lax.broadcasted_iota(jnp.int32, (ROWS, LANE), 0)
    return row, row % DEC_SEQ, row // DEC_SEQ


def _per_head(r_idx, f):
    return jnp.where(r_idx == 0, f(0), jnp.where(r_idx == 1, f(1), f(2)))


def _safe_div(acc, l):
    return acc / jnp.where(l > 0.0, l, 1.0)


def _merge_new_keys(fl, q32, kn, vn, bias, masks):
    s_new = [jnp.where(masks[j], jnp.sum(q32 * kn[j:j + 1, :], axis=-1, keepdims=True) * ATTN_SCALE
                       + bias[j], NEG_INF) for j in range(DEC_SEQ)]
    m_old = fl.m[...]
    m_new = m_old
    for s in s_new:
        m_new = jnp.maximum(m_new, s)
    alpha = jnp.exp(m_old - m_new)
    l = alpha * fl.l[...]
    acc = alpha * fl.acc[...]
    for j in range(DEC_SEQ):
        p = jnp.where(masks[j], jnp.exp(s_new[j] - m_new), 0.0)
        l = l + p
        acc = acc + p * vn[j:j + 1, :]
    return _safe_div(acc, l)


PAGES_PER_STEP = 8
NSA_PAGES_PER_STEP = 16
SUB = 8


def _flash_update(m_ref, l_ref, acc_ref, s, pv_fn, mask=None, mxu_sum=False):
    if mask is not None:
        s = jnp.where(mask, s, NEG_INF)
    m_old = m_ref[...]
    m_new = jnp.maximum(m_old, jnp.max(s, axis=-1, keepdims=True))
    alpha = jnp.exp(m_old - m_new)
    p = jnp.exp(s - m_new)
    if mask is not None:
        p = jnp.where(mask, p, 0.0)
    if mxu_sum:
        p = p.astype(BF16)
        row_sum = _dot(p, jnp.ones((p.shape[1], LANE), BF16))[:, 0:1]
    else:
        row_sum = jnp.sum(p, axis=-1, keepdims=True)
        p = p.astype(BF16)
    l_ref[...] = alpha * l_ref[...] + row_sum
    acc_ref[...] = alpha * acc_ref[...] + pv_fn(p)
    m_ref[...] = m_new


def _cumsum_lanes_kernel(x_ref, o_ref):
    ii = lax.broadcasted_iota(jnp.int32, (LANE, LANE), 0)
    jj = lax.broadcasted_iota(jnp.int32, (LANE, LANE), 1)
    upper = jnp.where(ii <= jj, 1.0, 0.0).astype(F32)
    o_ref[...] = jnp.dot(x_ref[...], upper, precision=lax.Precision.HIGHEST, preferred_element_type=F32)


def cumsum_lanes(x, tr=512):
    h, rows, _ = x.shape
    spec = pl.BlockSpec((None, tr, LANE), lambda i, j: (i, j, 0))
    return pl.pallas_call(
        _cumsum_lanes_kernel, grid=(h, rows // tr), in_specs=[spec], out_specs=spec,
        out_shape=jax.ShapeDtypeStruct(x.shape, F32),
        compiler_params=_cparams(("arbitrary", "arbitrary")), name="cumsum_lanes",
    )(x)


def _fox_sample_kernel(pt_ref, q_ref, kn_ref, vn_ref, *refs):
    n = PAGES_PER_STEP
    k_refs, v_refs, c_refs = refs[:n], refs[n:2 * n], refs[2 * n:3 * n]
    lfn_ref, o_ref, m_ref, l_ref, acc_ref, run_ref = refs[3 * n:]
    b = pl.program_id(0)
    step = pl.program_id(1)

    @pl.when(step == 0)
    def _():
        m_ref[...] = jnp.full(m_ref.shape, NEG_INF, F32)
        l_ref[...] = jnp.zeros(l_ref.shape, F32)
        acc_ref[...] = jnp.zeros(acc_ref.shape, F32)
        run_ref[...] = jnp.zeros(run_ref.shape, F32)

    sub = [pt_ref[b, step * n + j] % SUB for j in range(n)]
    s_heads = []
    for h in range(FOX_HEADS):
        q = q_ref[h].astype(BF16)
        run = run_ref[h:h + 1, :]
        parts = []
        for j in range(n):
            ck = run + c_refs[j][h, pl.ds(sub[j], 1), :]
            parts.append(_dot_nt(q, k_refs[j][h].astype(BF16)) * ATTN_SCALE - ck)
            run = jnp.broadcast_to(ck[:, PAGE_SIZE - 1:PAGE_SIZE], (1, LANE))
        run_ref[h:h + 1, :] = run
        s_heads.append(jnp.concatenate(parts, axis=1))
    s = jnp.concatenate(s_heads, axis=0)

    def pv(p):
        return jnp.concatenate(
            [sum(_dot(p[h * ROWS:(h + 1) * ROWS, j * PAGE_SIZE:(j + 1) * PAGE_SIZE], v_refs[j][h].astype(BF16))
                 for j in range(n)) for h in range(FOX_HEADS)], axis=0)

    _flash_update(m_ref, l_ref, acc_ref, s, pv)

    @pl.when(step == N_PAGES // n - 1)
    def _():
        row = lax.broadcasted_iota(jnp.int32, (ROWS, 1), 0)
        lfn = lfn_ref[...]
        for h in range(FOX_HEADS):
            hs = slice(h * HEAD_DIM, (h + 1) * HEAD_DIM)
            rs = slice(h * ROWS, (h + 1) * ROWS)
            c, c_new = run_ref[h:h + 1, 0:1], []
            for j in range(DEC_SEQ):
                c = c + lfn[h:h + 1, j:j + 1]
                c_new.append(-c)
            fl = _Flash(m_ref.at[rs], l_ref.at[rs], acc_ref.at[rs])
            o_ref[h] = _merge_new_keys(fl, q_ref[h], kn_ref[:, hs], vn_ref[:, hs], c_new,
                                       [row >= j for j in range(DEC_SEQ)])


def fox_sample(q16, ps3, pool_k, pool_v, ct_pool, lfn, page_table):
    nb = q16.shape[0]
    n = PAGES_PER_STEP
    page = lambda j: pl.BlockSpec((None, FOX_HEADS, PAGE_SIZE, HEAD_DIM),
                                  lambda b, s, pt: (pt[b, s * n + j], 0, 0, 0))
    cpage = lambda j: pl.BlockSpec((FOX_HEADS, SUB, PAGE_SIZE), lambda b, s, pt: (0, pt[b, s * n + j] // SUB, 0))
    rows = FOX_HEADS * ROWS
    grid_spec = pltpu.PrefetchScalarGridSpec(
        num_scalar_prefetch=1,
        grid=(nb, N_PAGES // n),
        in_specs=[
            pl.BlockSpec((None, FOX_HEADS, ROWS, HEAD_DIM), lambda b, s, pt: (b, 0, 0, 0)),
            pl.BlockSpec((None, DEC_SEQ, SELF_W), lambda b, s, pt: (b, 0, 1)),
            pl.BlockSpec((None, DEC_SEQ, SELF_W), lambda b, s, pt: (b, 0, 2)),
        ] + [page(j) for j in range(n)] + [page(j) for j in range(n)] + [cpage(j) for j in range(n)] + [
            pl.BlockSpec((None, HEAD_PAD, DEC_SEQ), lambda b, s, pt: (b, 0, 0)),
        ],
        out_specs=pl.BlockSpec((None, FOX_HEADS, ROWS, HEAD_DIM), lambda b, s, pt: (b, 0, 0, 0)),
        scratch_shapes=[pltpu.VMEM((rows, 1), F32), pltpu.VMEM((rows, 1), F32),
                        pltpu.VMEM((rows, HEAD_DIM), F32), pltpu.VMEM((HEAD_PAD, LANE), F32)],
    )
    return pl.pallas_call(
        _fox_sample_kernel,
        grid_spec=grid_spec,
        out_shape=jax.ShapeDtypeStruct((nb, FOX_HEADS, ROWS, HEAD_DIM), F32),
        compiler_params=_cparams(("arbitrary", "arbitrary")),
        name="fox_sample",
    )(page_table, q16, ps3, ps3, *([pool_k] * n), *([pool_v] * n), *([ct_pool] * n), lfn)


def _nsa_sample_cmp_kernel(pt_ref, rb_ref, q_ref, *refs):
    ck_pages, cv_pages = refs[:N_PAGES], refs[N_PAGES:2 * N_PAGES]
    ckn_ref, cvn_ref, oc_ref, sx_ref, ck_s, cv_s = refs[2 * N_PAGES:]
    b = pl.program_id(0)
    blk_per_page = PAGE_SIZE // CMP_BLOCK

    @pl.when(b == 0)
    def _():
        ck_s[...] = jnp.zeros(ck_s.shape, F32)
        cv_s[...] = jnp.zeros(cv_s.shape, F32)

    for p in range(N_PAGES):
        ck_s[p * blk_per_page:(p + 1) * blk_per_page, :] = ck_pages[p][...]
        cv_s[p * blk_per_page:(p + 1) * blk_per_page, :] = cv_pages[p][...]
    ck_s[N_PAST_BLK:N_PAST_BLK + 1, :] = ckn_ref[...]
    cv_s[N_PAST_BLK:N_PAST_BLK + 1, :] = cvn_ref[...]

    row, tok, r_idx = _row_ids()
    lane = lax.broadcasted_iota(jnp.int32, (ROWS, LANE), 1)
    qpos = PAST_LEN + tok
    dist = qpos - (lane * CMP_BLOCK + (CMP_BLOCK - 1))
    valid = (dist >= 0) & (lane <= N_PAST_BLK)
    for g in range(NSA_KV_HEADS):
        gs = slice(g * HEAD_DIM, (g + 1) * HEAD_DIM)
        s = _dot_nt(q_ref[g].astype(BF16), ck_s[:, gs].astype(BF16)) * ATTN_SCALE
        s = s + _per_head(r_idx, lambda r: _t5_bias(dist, rb_ref, NSA_GROUP * g + r))
        p = _softmax_rows(s, valid)
        oc_ref[g] = _dot(p.astype(BF16), cv_s[:, gs].astype(BF16))
        p0 = jnp.where(row < NSA_GROUP * DEC_SEQ, p, 0.0)
        imp = p0 + pltpu.roll(p0, ROWS - DEC_SEQ, 0) + pltpu.roll(p0, ROWS - 2 * DEC_SEQ, 0)
        imp = jnp.where(row < DEC_SEQ, imp, 0.0)
        imp = imp + pltpu.roll(imp, DEC_SEQ, 0) + pltpu.roll(imp, 2 * DEC_SEQ, 0)
        sel = _select_blocks(imp, qpos // SLC_BLOCK, lane, N_PAST_BLK + 1)
        sx_ref[g] = jnp.where(sel, 1.0, 0.0)


def nsa_sample_cmp(qg16, ck_pool, cv_pool, ck_new, cv_new, rel_bias, page_table):
    nb = qg16.shape[0]
    page = lambda p: pl.BlockSpec((None, PAGE_SIZE // CMP_BLOCK, NSA_KV_W), lambda b, pt: (pt[b, p], 0, 0))
    grid_spec = pltpu.PrefetchScalarGridSpec(
        num_scalar_prefetch=1,
        grid=(nb,),
        in_specs=[pl.BlockSpec(memory_space=pltpu.SMEM),
                  pl.BlockSpec((None, NSA_KV_HEADS, ROWS, HEAD_DIM), lambda b, pt: (b, 0, 0, 0))]
        + [page(p) for p in range(N_PAGES)] + [page(p) for p in range(N_PAGES)]
        + [pl.BlockSpec((None, 1, NSA_KV_W), lambda b, pt: (b, 0, 0))] * 2,
        out_specs=[pl.BlockSpec((None, NSA_KV_HEADS, ROWS, HEAD_DIM), lambda b, pt: (b, 0, 0, 0)),
                   pl.BlockSpec((None, NSA_KV_HEADS, ROWS, LANE), lambda b, pt: (b, 0, 0, 0))],
        scratch_shapes=[pltpu.VMEM((LANE, NSA_KV_W), F32), pltpu.VMEM((LANE, NSA_KV_W), F32)],
    )
    return pl.pallas_call(
        _nsa_sample_cmp_kernel,
        grid_spec=grid_spec,
        out_shape=[jax.ShapeDtypeStruct((nb, NSA_KV_HEADS, ROWS, HEAD_DIM), F32),
                   jax.ShapeDtypeStruct((nb, NSA_KV_HEADS, ROWS, LANE), F32)],
        compiler_params=_cparams(("arbitrary",)),
        name="nsa_sample_cmp",
    )(page_table, rel_bias, qg16, *([ck_pool] * N_PAGES), *([cv_pool] * N_PAGES), ck_new, cv_new)


def _nsa_sample_kernel(pt_ref, rb_ref, q_ref, *refs):
    n = NSA_PAGES_PER_STEP
    n_steps = N_PAGES // n
    k_refs, v_refs = refs[:n], refs[n:2 * n]
    (sx_ref, kn_ref, vn_ref, wk_ref, wv_ref, wkn_ref, wvn_ref, wkr_ref, wvr_ref, oc_ref, gl_ref, bg_ref,
     o_ref, wko_ref, wvo_ref, m_ref, l_ref, acc_ref, ow_ref, b15_ref, bn_ref) = refs[2 * n:]
    b = pl.program_id(0)
    p = pl.program_id(1)
    ng = NSA_KV_HEADS
    rows = ng * ROWS
    page_w = PAGE_SIZE * ng
    row, tok, r_idx = _row_ids()
    lane = lax.broadcasted_iota(jnp.int32, (ROWS, LANE), 1)
    tok1, r1 = tok[:, 0:1], r_idx[:, 0:1]
    new_vis = [tok1 >= j for j in range(DEC_SEQ)]
    q_all = q_ref[...].reshape(rows, HEAD_DIM).astype(BF16)
    grp_of_row = lax.broadcasted_iota(jnp.int32, (rows, 1), 0) // ROWS
    tok_of_row = lax.broadcasted_iota(jnp.int32, (rows, 1), 0) % DEC_SEQ

    def stack(f):
        return jnp.concatenate([f(g) for g in range(ng)], axis=0)

    far_bias = stack(lambda g: _per_head(r1, lambda r: rb_ref[N_BUCKETS - 1, NSA_GROUP * g + r]))

    def own_group(width):
        return lax.broadcasted_iota(jnp.int32, (rows, width), 1) % ng == grp_of_row

    @pl.when((b == 0) & (p == 0))
    def _():
        c_tok = lax.broadcasted_iota(jnp.int32, (ROWS, page_w), 1) // ng
        t_row = lax.broadcasted_iota(jnp.int32, (ROWS, page_w), 0) % DEC_SEQ
        r_row = lax.broadcasted_iota(jnp.int32, (ROWS, page_w), 0) // DEC_SEQ
        for g in range(ng):
            b15_ref[g * ROWS:(g + 1) * ROWS, :] = _per_head(
                r_row, lambda r: _t5_bias(LANE + t_row - c_tok, rb_ref, NSA_GROUP * g + r))
            bn_ref[g] = _per_head(r_idx, lambda r: _t5_bias(tok - lane, rb_ref, NSA_GROUP * g + r))

    def new_scores(kn):
        return [stack(lambda g: jnp.sum(q_ref[g] * kn[j:j + 1, g * HEAD_DIM:(g + 1) * HEAD_DIM], axis=-1,
                                        keepdims=True) * ATTN_SCALE + bn_ref[g][:, j:j + 1])
                for j in range(DEC_SEQ)]

    def new_values(vn, j):
        return stack(lambda g: jnp.broadcast_to(vn[j:j + 1, g * HEAD_DIM:(g + 1) * HEAD_DIM], (ROWS, HEAD_DIM)))

    @pl.when(p == 0)
    def _():
        m_ref[...] = jnp.full(m_ref.shape, NEG_INF, F32)
        l_ref[...] = jnp.zeros(l_ref.shape, F32)
        acc_ref[...] = jnp.zeros(acc_ref.shape, F32)
        width = WINDOW * ng
        s = _dot_nt(q_all, wk_ref[...].astype(BF16)) * ATTN_SCALE
        bias = jnp.concatenate([jnp.broadcast_to(far_bias, (rows, width - page_w)), b15_ref[...]], axis=1)
        c_tok = lax.broadcasted_iota(jnp.int32, (rows, width), 1) // ng
        in_win = own_group(width) & (c_tok > tok_of_row)
        s = jnp.where(in_win, s + bias, NEG_INF)
        vis = [tok_of_row >= j for j in range(DEC_SEQ)]
        s_new = [jnp.where(vis[j], sn, NEG_INF) for j, sn in enumerate(new_scores(wkn_ref))]
        mx = jnp.max(s, axis=-1, keepdims=True)
        for sn in s_new:
            mx = jnp.maximum(mx, sn)
        e = jnp.where(in_win, jnp.exp(s - mx), 0.0)
        den = jnp.sum(e, axis=-1, keepdims=True)
        acc = _dot(e.astype(BF16), wv_ref[...].astype(BF16))
        for j in range(DEC_SEQ):
            pj = jnp.where(vis[j], jnp.exp(s_new[j] - mx), 0.0)
            den = den + pj
            acc = acc + pj * new_values(wvn_ref, j)
        ow_ref[...] = _safe_div(acc, den)
        keep = (WINDOW - DEC_SEQ) * ng
        wko_ref[0:keep, :] = wk_ref[DEC_SEQ * ng:, :]
        wvo_ref[0:keep, :] = wv_ref[DEC_SEQ * ng:, :]
        wko_ref[keep:, :] = wkr_ref[...]
        wvo_ref[keep:, :] = wvr_ref[...]

    blk_per_page = PAGE_SIZE // SLC_BLOCK
    blk = lax.broadcasted_iota(jnp.int32, (LANE, page_w), 0)
    off = lax.broadcasted_iota(jnp.int32, (LANE, page_w), 1) // (SLC_BLOCK * ng)
    sel = sx_ref[...].reshape(rows, LANE).astype(BF16)
    own = own_group(page_w)
    s_parts, m_parts = [], []
    for j in range(n):
        bias = far_bias
        if j == n - 1:
            bias = jnp.where(p == n_steps - 1, b15_ref[...], bias)
        s_parts.append(_dot_nt(q_all, k_refs[j][...].astype(BF16)) * ATTN_SCALE + bias)
        to_cols = jnp.where(blk - off == (p * n + j) * blk_per_page, 1.0, 0.0).astype(BF16)
        m_parts.append((_dot(sel, to_cols) > 0.5) & own)

    def pv(pr):
        return sum(_dot(pr[:, j * page_w:(j + 1) * page_w], v_refs[j][...].astype(BF16)) for j in range(n))

    _flash_update(m_ref, l_ref, acc_ref, jnp.concatenate(s_parts, axis=1), pv, jnp.concatenate(m_parts, axis=1))

    @pl.when(p == n_steps - 1)
    def _():
        gate = jax.nn.sigmoid(gl_ref[...] + bg_ref[...].reshape(ng, 1, LANE))
        picked = stack(lambda g: sx_ref[g][:, N_PAST_BLK:N_PAST_BLK + 1]) > 0.5
        vis = [(tok_of_row >= j) & picked for j in range(DEC_SEQ)]
        s_new = [jnp.where(vis[j], sn, NEG_INF) for j, sn in enumerate(new_scores(kn_ref))]
        m_old = m_ref[...]
        m_new = m_old
        for sn in s_new:
            m_new = jnp.maximum(m_new, sn)
        alpha = jnp.exp(m_old - m_new)
        den = alpha * l_ref[...]
        acc = alpha * acc_ref[...]
        for j in range(DEC_SEQ):
            pj = jnp.where(vis[j], jnp.exp(s_new[j] - m_new), 0.0)
            den = den + pj
            acc = acc + pj * new_values(vn_ref, j)
        o_s = _safe_div(acc, den)
        o_w = ow_ref[...]
        for g in range(ng):
            rs = slice(g * ROWS, (g + 1) * ROWS)
            gt = [_per_head(r1, lambda r: gate[g][:, 3 * r + c:3 * r + c + 1]) for c in range(3)]
            o_ref[g] = gt[0] * oc_ref[g] + gt[1] * o_s[rs] + gt[2] * o_w[rs]


def nsa_sample(qg16, ps3, pool_k, pool_v, selx, buf_k, buf_v, o_c, gl16, b_gate, rel_bias, page_table):
    nb = qg16.shape[0]
    win_cols = slice(SELF_W + 4 * NSA_KV_W, SELF_W + 6 * NSA_KV_W)
    new_rows = ps3[:, :, win_cols].reshape(nb, DEC_SEQ, 2, NSA_KV_HEADS, HEAD_DIM)
    wk_rows = new_rows[:, :, 0].reshape(nb, DEC_SEQ * NSA_KV_HEADS, HEAD_DIM)
    wv_rows = new_rows[:, :, 1].reshape(nb, DEC_SEQ * NSA_KV_HEADS, HEAD_DIM)
    rspec = pl.BlockSpec((None, DEC_SEQ * NSA_KV_HEADS, HEAD_DIM), lambda b, p, pt: (b, 0, 0))
    n = NSA_PAGES_PER_STEP
    cb = lambda c: c // NSA_KV_W
    qspec = pl.BlockSpec((None, NSA_KV_HEADS, ROWS, HEAD_DIM), lambda b, p, pt: (b, 0, 0, 0))
    new = lambda c: pl.BlockSpec((None, DEC_SEQ, NSA_KV_W), lambda b, p, pt: (b, 0, cb(c)))
    page_w = PAGE_SIZE * NSA_KV_HEADS
    pool_k = pool_k.reshape(-1, page_w, HEAD_DIM)
    pool_v = pool_v.reshape(-1, page_w, HEAD_DIM)
    buf_k = buf_k.reshape(nb, WINDOW * NSA_KV_HEADS, HEAD_DIM)
    buf_v = buf_v.reshape(nb, WINDOW * NSA_KV_HEADS, HEAD_DIM)
    page = lambda j: pl.BlockSpec((None, page_w, HEAD_DIM), lambda b, p, pt: (pt[b, p * n + j], 0, 0))
    wspec = pl.BlockSpec((None, WINDOW * NSA_KV_HEADS, HEAD_DIM), lambda b, p, pt: (b, 0, 0))
    rows = NSA_KV_HEADS * ROWS
    grid_spec = pltpu.PrefetchScalarGridSpec(
        num_scalar_prefetch=1,
        grid=(nb, N_PAGES // n),
        in_specs=[pl.BlockSpec(memory_space=pltpu.SMEM), qspec]
        + [page(j) for j in range(n)] + [page(j) for j in range(n)] + [
            qspec,
            new(SELF_W + 2 * NSA_KV_W), new(SELF_W + 3 * NSA_KV_W),
            wspec, wspec,
            new(SELF_W + 4 * NSA_KV_W), new(SELF_W + 5 * NSA_KV_W),
            rspec, rspec,
            qspec, qspec,
            pl.BlockSpec((1, CROSS_W), lambda b, p, pt: (0, 0)),
        ],
        out_specs=[qspec, wspec, wspec],
        scratch_shapes=[pltpu.VMEM((rows, 1), F32), pltpu.VMEM((rows, 1), F32),
                        pltpu.VMEM((rows, HEAD_DIM), F32),
                        pltpu.VMEM((rows, HEAD_DIM), F32),
                        pltpu.VMEM((rows, page_w), F32),
                        pltpu.VMEM((NSA_KV_HEADS, ROWS, LANE), F32)],
    )
    o, win_k, win_v = pl.pallas_call(
        _nsa_sample_kernel,
        grid_spec=grid_spec,
        out_shape=[jax.ShapeDtypeStruct((nb, NSA_KV_HEADS, ROWS, HEAD_DIM), F32),
                   jax.ShapeDtypeStruct(buf_k.shape, F32), jax.ShapeDtypeStruct(buf_v.shape, F32)],
        compiler_params=_cparams(("arbitrary", "arbitrary")),
        name="nsa_sample",
    )(page_table, rel_bias, qg16, *([pool_k] * n), *([pool_v] * n), selx, ps3, ps3, buf_k, buf_v,
      ps3, ps3, wk_rows, wv_rows, o_c, gl16, b_gate)
    heads = (nb, WINDOW, NSA_KV_HEADS, HEAD_DIM)
    return o, win_k.reshape(heads), win_v.reshape(heads)


def _cross_sample_kernel(q_ref, mk_ref, mv_ref, o_ref):
    nh = N_CROSS_HEADS
    rows, width = nh * ROWS, MEM_LEN * nh
    q = q_ref[...].reshape(rows, HEAD_DIM).astype(BF16)
    s = _dot_nt(q, mk_ref[...].astype(BF16)) * ATTN_SCALE
    own = (lax.broadcasted_iota(jnp.int32, (rows, width), 1) % nh
           == lax.broadcasted_iota(jnp.int32, (rows, width), 0) // ROWS)
    p = _softmax_rows(s, own)
    o_ref[...] = _dot(p.astype(BF16), mv_ref[...].astype(BF16)).reshape(nh, ROWS, HEAD_DIM)


def cross_sample(qx16, mem_k, mem_v, layer):
    nb = qx16.shape[0]
    n_layers = mem_k.shape[0]
    mem_k = mem_k.reshape(n_layers, nb, MEM_LEN * N_CROSS_HEADS, HEAD_DIM)
    mem_v = mem_v.reshape(n_layers, nb, MEM_LEN * N_CROSS_HEADS, HEAD_DIM)
    qspec = pl.BlockSpec((None, N_CROSS_HEADS, ROWS, HEAD_DIM), lambda b: (b, 0, 0, 0))
    mspec = pl.BlockSpec((None, None, MEM_LEN * N_CROSS_HEADS, HEAD_DIM), lambda b: (layer, b, 0, 0))
    return pl.pallas_call(
        _cross_sample_kernel,
        grid=(nb,),
        in_specs=[qspec, mspec, mspec],
        out_specs=qspec,
        out_shape=jax.ShapeDtypeStruct((nb, N_CROSS_HEADS, ROWS, HEAD_DIM), F32),
        compiler_params=_cparams(("arbitrary",)),
        name="cross_sample",
    )(qx16, mem_k, mem_v)


def _arrange_in_weights(w, n_gate, per_group):
    gate = w[:, COL_QX:COL_QX + n_gate]
    if per_group:
        gate = jnp.pad(gate.reshape(-1, NSA_KV_HEADS, n_gate // NSA_KV_HEADS),
                       ((0, 0), (0, 0), (0, LANE - n_gate // NSA_KV_HEADS))).reshape(-1, CROSS_W)
    else:
        gate = jnp.pad(gate, ((0, 0), (0, CROSS_W - n_gate)))
    return jnp.concatenate([w[:, COL_QX + n_gate:], gate], axis=1)


def kernel(x_prompt, x_sample, mem_prompt, cache_nsa_cmp_k, cache_nsa_cmp_v, cache_nsa_slc_k, cache_nsa_slc_v, cache_nsa_win_k, cache_nsa_win_v, cache_fox_k, cache_fox_v, cache_fox_logf, cache_mem_k, cache_mem_v, page_table, rel_bias, norm_g, mem_norm_g, w_mem_kv, w_ff_gu, w_ff_down, w_in_nsa, b_gate_nsa, w_cmp1, w_cmp2, cmp_pe, w_out_nsa, w_in_fox, b_forget, w_out_fox, final_norm_g):
    n_p = BATCH * SEQ
    x = jnp.concatenate([x_prompt.reshape(n_p, D_MODEL), x_sample.reshape(-1, D_MODEL)], axis=0)
    mem = mem_prompt.reshape(BATCH * MEM_LEN, D_MODEL)
    n_pool = cache_nsa_cmp_k.shape[1]
    kv_heads = (BATCH, SEQ, NSA_KV_HEADS, HEAD_DIM)
    w_ff_gu = w_ff_gu.astype(BF16)
    w_ff_down = w_ff_down.astype(BF16)
    out = {}
    for i in range(DEPTH):
        x = ffn(x, norm_g[i, 0], w_ff_gu, w_ff_down, (i, 0))
        kvm = linear(mem, w_mem_kv[i], g=mem_norm_g[i], tm=512)
        out[f"mem_k{i}"] = kvm[:, :CROSS_W].reshape(BATCH, MEM_LEN, N_CROSS_HEADS, HEAD_DIM)
        out[f"mem_v{i}"] = kvm[:, CROSS_W:].reshape(BATCH, MEM_LEN, N_CROSS_HEADS, HEAD_DIM)
        a = i // 2
        if i % 2 == 0:
            w_tail = _arrange_in_weights(w_in_nsa[a], 3 * N_SELF_HEADS, True).astype(BF16)
            proj = linear(x, w_in_nsa[a].astype(BF16), w_tail, COL_QX // 512, g=norm_g[i, 1], tm=BIG_TOK_TILE)
            ps3 = proj[n_p:].reshape(DEC_BATCH, DEC_SEQ, PROJ_W)
            bg = jnp.pad(b_gate_nsa[a].reshape(NSA_KV_HEADS, 3 * NSA_GROUP),
                         ((0, 0), (0, LANE - 3 * NSA_GROUP))).reshape(1, CROSS_W)
            cmp_w = [(w_cmp1[a, c], w_cmp2[a, c], cmp_pe[a, c]) for c in range(2)]
            kvc = proj[:, SELF_W:SELF_W + 2 * NSA_KV_W]
            x_blk = kvc.reshape(N_TOK // CMP_BLOCK, CMP_BLOCK * 2 * NSA_KV_W)
            x_new = kvc.reshape(N_TOK // DEC_SEQ, DEC_SEQ * 2 * NSA_KV_W)
            pools = [cache_nsa_cmp_k[a], cache_nsa_cmp_v[a]]
            c_prompt, c_pool, c_new = [], [], []
            for c in range(2):
                col = lambda l, c=c: 2 * l + c
                c_prompt.append(compress(x_blk, col, CMP_BLOCK, *cmp_w[c], tr=BATCH * SEQ // CMP_BLOCK,
                                         rows=BATCH * SEQ // CMP_BLOCK))
                c_pool.append(compress_pool(pools[c], *cmp_w[c])
                              .reshape(n_pool, PAGE_SIZE // CMP_BLOCK, NSA_KV_W))
                c_new.append(compress(x_new, col, DEC_SEQ, *cmp_w[c], tr=DEC_BATCH, rows=DEC_BATCH,
                                      row_block0=n_p // DEC_SEQ // DEC_BATCH).reshape(DEC_BATCH, 1, NSA_KV_W))
            o_p, ox_p = nsa_prompt(proj, c_prompt[0], c_prompt[1], kvm, rel_bias, bg, BATCH)
            qg16 = _group_rows(ps3[:, :, :SELF_W])
            o_c, selx = nsa_sample_cmp(qg16, c_pool[0], c_pool[1], c_new[0], c_new[1], rel_bias, page_table)
            o_s, win_k, win_v = nsa_sample(
                qg16, ps3, cache_nsa_slc_k[a], cache_nsa_slc_v[a], selx,
                cache_nsa_win_k[a], cache_nsa_win_v[a],
                o_c, _gate_rows(ps3[:, :, COL_GATE:]), bg, rel_bias, page_table)
            o_s = _ungroup_rows(o_s)
            new_win = {"win_k": win_k, "win_v": win_v}
            for j, name in enumerate(("cmp_k", "cmp_v", "slc_k", "slc_v", "win_k", "win_v")):
                cols = slice(SELF_W + j * NSA_KV_W, SELF_W + (j + 1) * NSA_KV_W)
                st_p = proj[:n_p, cols].reshape(kv_heads)
                st_s = ps3[:, :, cols].reshape(DEC_BATCH, DEC_SEQ, NSA_KV_HEADS, HEAD_DIM)
                if name.startswith("win"):
                    st_p = st_p[:, -WINDOW:]
                    st_s = new_win[name]
                out.setdefault("p_" + name, []).append(st_p)
                out.setdefault("s_" + name, []).append(st_s)
            w_out = w_out_nsa[a]
        else:
            w_tail = _arrange_in_weights(w_in_fox[a], FOX_HEADS, False).astype(BF16)
            proj = linear(x, w_in_fox[a].astype(BF16), w_tail, COL_QX // 512, g=norm_g[i, 1], tm=BIG_TOK_TILE)
            ps3 = proj[n_p:].reshape(DEC_BATCH, DEC_SEQ, PROJ_W)
            lf_p, ct = fox_prep(proj, b_forget[a], BATCH)
            n_s = DEC_BATCH * DEC_SEQ
            lf_s, _ = fox_prep(proj, b_forget[a], 1, seq_len=n_s, row_block0=n_p // n_s)
            o_p = fox_prompt(proj, ct, BATCH)
            ox_p = cross_prompt(proj, kvm, BATCH)
            ct_pool = cumsum_lanes(jnp.transpose(cache_fox_logf[a], (2, 0, 1)))
            lfn = lf_s[:, :HEAD_PAD].reshape(DEC_BATCH, DEC_SEQ, HEAD_PAD).transpose(0, 2, 1)
            o_s = _unhead_rows(fox_sample(
                _head_rows(ps3[:, :, :SELF_W], FOX_HEADS), ps3,
                jnp.transpose(cache_fox_k[a], (0, 2, 1, 3)), jnp.transpose(cache_fox_v[a], (0, 2, 1, 3)),
                ct_pool, lfn, page_table))
            heads = (FOX_HEADS, HEAD_DIM)
            for j, name in enumerate(("fox_k", "fox_v")):
                cols = slice((j + 1) * SELF_W, (j + 2) * SELF_W)
                out.setdefault("p_" + name, []).append(proj[:n_p, cols].reshape(BATCH, SEQ, *heads))
                out.setdefault("s_" + name, []).append(ps3[:, :, cols].reshape(DEC_BATCH, DEC_SEQ, *heads))
            out.setdefault("p_fox_logf", []).append(lf_p[:, :FOX_HEADS].reshape(BATCH, SEQ, FOX_HEADS))
            out.setdefault("s_fox_logf", []).append(lf_s[:, :FOX_HEADS].reshape(DEC_BATCH, DEC_SEQ, FOX_HEADS))
            w_out = w_out_fox[a]
        ox_s = _unhead_rows(cross_sample(
            _head_rows(ps3[:, :, COL_QX:COL_GATE], N_CROSS_HEADS), cache_mem_k, cache_mem_v, i))
        mix_self = jnp.concatenate([o_p, o_s.astype(BF16)], axis=0)
        mix_cross = jnp.concatenate([ox_p, ox_s.astype(BF16)], axis=0)
        x = linear([mix_self, mix_cross], w_out.astype(BF16), res=x, tm=BIG_TOK_TILE)
        x = ffn(x, norm_g[i, 2], w_ff_gu, w_ff_down, (i, 1),
                final_g=final_norm_g if i == DEPTH - 1 else None)
    y_prompt = x[:n_p].reshape(BATCH, SEQ, D_MODEL)
    y_sample = x[n_p:].reshape(DEC_BATCH, DEC_SEQ, D_MODEL)
    st = lambda name: jnp.stack(out[name])
    p_mem_k = jnp.stack([out[f"mem_k{i}"] for i in range(DEPTH)])
    p_mem_v = jnp.stack([out[f"mem_v{i}"] for i in range(DEPTH)])
    return (y_prompt, y_sample,
            st("p_cmp_k"), st("p_cmp_v"), st("p_slc_k"), st("p_slc_v"), st("p_win_k"), st("p_win_v"),
            st("p_fox_k"), st("p_fox_v"), st("p_fox_logf"), p_mem_k, p_mem_v,
            st("s_cmp_k"), st("s_cmp_v"), st("s_slc_k"), st("s_slc_v"), st("s_win_k"), st("s_win_v"),
            st("s_fox_k"), st("s_fox_v"), st("s_fox_logf"))
```

```python
import functools
import math

import jax
import jax.numpy as jnp
from jax import lax
from jax.experimental import pallas as pl
from jax.experimental.pallas import tpu as pltpu

D_MODEL = 2048
BATCH = 4
SEQ = 2048
DEPTH = 2
DEC_BATCH = 128
DEC_SEQ = 4
PAST_LEN = 2048
PAGE_SIZE = 128
HEAD_DIM = 128
N_CROSS_HEADS = 4
N_SELF_HEADS = 12
NSA_KV_HEADS = 4
NSA_GROUP = N_SELF_HEADS // NSA_KV_HEADS
CMP_BLOCK = 64
SLC_BLOCK = 64
N_SEL = 16
WINDOW = 512
CMP_HIDDEN = 256
FOX_HEADS = N_SELF_HEADS
MEM_LEN = 256
N_BUCKETS = 32
MAX_DISTANCE = 128
FFN_DIM = ((8 * D_MODEL) // 3 + 127) // 128 * 128
RMS_EPS = 1e-6
FORCE_SCORE = 1e4
NEG_INF = -1e30
ATTN_SCALE = HEAD_DIM ** -0.5
SELF_W = N_SELF_HEADS * HEAD_DIM
NSA_KV_W = NSA_KV_HEADS * HEAD_DIM
CROSS_W = N_CROSS_HEADS * HEAD_DIM
N_PAGES = PAST_LEN // PAGE_SIZE
N_PAST_BLK = PAST_LEN // SLC_BLOCK

LANE = 128
VMEM_LIMIT = 56 * 1024 * 1024
N_TOK = BATCH * SEQ + DEC_BATCH * DEC_SEQ
TOK_TILE = 544
BIG_TOK_TILE = 2 * TOK_TILE
Q_TILE = 128
COL_QX = 3 * SELF_W
COL_GATE = COL_QX + CROSS_W
PROJ_W = COL_GATE + CROSS_W

F32 = jnp.float32
BF16 = jnp.bfloat16
_NT = (((1,), (1,)), ((), ()))


def _cparams(sem):
    return pltpu.CompilerParams(dimension_semantics=sem, vmem_limit_bytes=VMEM_LIMIT)


def _rms(x, g):
    return x * lax.rsqrt(jnp.mean(x * x, axis=-1, keepdims=True) + RMS_EPS) * g


def _dot(a, b):
    return jnp.dot(a, b, preferred_element_type=F32)


def _dot_nt(a, b):
    return lax.dot_general(a, b, _NT, preferred_element_type=F32)


def _t5_bias(dist, rb_ref, h):
    n = jnp.maximum(dist, 0)
    max_exact = N_BUCKETS // 2
    nf = jnp.maximum(n, 1).astype(F32)
    large = max_exact + (jnp.log(nf / max_exact) / math.log(MAX_DISTANCE / max_exact)
                         * (N_BUCKETS - max_exact)).astype(jnp.int32)
    bucket = jnp.where(n <= max_exact, n, jnp.minimum(large, N_BUCKETS - 1))
    out = jnp.zeros(dist.shape, F32)
    for b in range(N_BUCKETS):
        out = jnp.where(bucket == b, rb_ref[b, h], out)
    return out


def _softmax_rows(s, mask):
    s = jnp.where(mask, s, NEG_INF)
    e = jnp.where(mask, jnp.exp(s - jnp.max(s, axis=-1, keepdims=True)), 0.0)
    den = jnp.sum(e, axis=-1, keepdims=True)
    return e / jnp.where(den > 0.0, den, 1.0)


def _select_blocks(imp, cur, col, n_blocks):
    forced = (col == 0) | (col == cur) | (col == cur - 1)
    score = jnp.where(col <= cur, jnp.where(forced, FORCE_SCORE, imp), -1.0)
    rank = jnp.zeros(score.shape, jnp.int32)
    for i in range(n_blocks):
        ci = score[:, i:i + 1]
        beats = (ci > score) | ((ci == score) & (col > i))
        rank = rank + beats.astype(jnp.int32)
    return (rank < N_SEL) & (col <= cur)


class _Flash:
    def __init__(self, m_ref, l_ref, acc_ref):
        self.m, self.l, self.acc = m_ref, l_ref, acc_ref

    def reset(self):
        self.m[...] = jnp.full(self.m.shape, NEG_INF, F32)
        self.l[...] = jnp.zeros(self.l.shape, F32)
        self.acc[...] = jnp.zeros(self.acc.shape, F32)

    def update(self, s, v, mask=None):
        if mask is not None:
            s = jnp.where(mask, s, NEG_INF)
        m_old = self.m[...]
        m_new = jnp.maximum(m_old, jnp.max(s, axis=-1, keepdims=True))
        alpha = jnp.exp(m_old - m_new)
        p = jnp.exp(s - m_new)
        if mask is not None:
            p = jnp.where(mask, p, 0.0)
        self.l[...] = alpha * self.l[...] + jnp.sum(p, axis=-1, keepdims=True)
        self.acc[...] = alpha * self.acc[...] + _dot(p.astype(BF16), v)
        self.m[...] = m_new

    def result(self):
        return self.acc[...] / self.l[...]


FFN_TILES = 4


def _ffn_kernel(x_ref, g_ref, fg_ref, *refs, n_tiles, n_steps, final_norm):
    t = FFN_TILES
    wg, wu, wd = refs[:t], refs[t:2 * t], refs[2 * t:3 * t]
    o_ref, h_ref = refs[3 * t:]
    f = pl.program_id(1)

    @pl.when(f == 0)
    def _():
        x = x_ref[...]
        h_ref[...] = _rms(x, g_ref[...]).astype(BF16)
        o_ref[...] = x

    h = h_ref[...]
    a = _dot(h, jnp.concatenate([r[...].astype(BF16) for r in wg], axis=1))
    b = _dot(h, jnp.concatenate([r[...].astype(BF16) for r in wu], axis=1))
    act = a * jax.nn.sigmoid(a) * b
    tile = t * f + lax.broadcasted_iota(jnp.int32, act.shape, 1) // LANE
    act = jnp.where(tile < n_tiles, act, 0.0)
    w_down = jnp.concatenate([r[...].astype(BF16) for r in wd], axis=0)
    o_ref[...] += 0.5 * _dot(act.astype(BF16), w_down)

    if final_norm:
        @pl.when(f == n_steps - 1)
        def _():
            o_ref[...] = _rms(o_ref[...], fg_ref[...])


def ffn(x, g, w_gu, w_down, which, final_g=None):
    m, d = x.shape
    n_f = FFN_DIM // LANE
    n_steps = pl.cdiv(n_f, FFN_TILES)
    tm = TOK_TILE
    final_norm = final_g is not None
    fg = final_g if final_norm else g

    def col(off, k):
        return lambda i, f: (*which, 0, off + jnp.minimum(FFN_TILES * f + k, n_f - 1))

    def row(k):
        return lambda i, f: (*which, jnp.minimum(FFN_TILES * f + k, n_f - 1), 0)

    ks = range(FFN_TILES)
    return pl.pallas_call(
        functools.partial(_ffn_kernel, n_tiles=n_f, n_steps=n_steps, final_norm=final_norm),
        grid=(m // tm, n_steps),
        in_specs=[
            pl.BlockSpec((tm, d), lambda i, f: (i, 0)),
            pl.BlockSpec((1, d), lambda i, f: (0, 0)),
            pl.BlockSpec((1, d), lambda i, f: (0, 0)),
        ] + [pl.BlockSpec((None, None, d, LANE), col(0, k)) for k in ks]
        + [pl.BlockSpec((None, None, d, LANE), col(n_f, k)) for k in ks]
        + [pl.BlockSpec((None, None, LANE, d), row(k)) for k in ks],
        out_specs=pl.BlockSpec((tm, d), lambda i, f: (i, 0)),
        out_shape=jax.ShapeDtypeStruct((m, d), F32),
        scratch_shapes=[pltpu.VMEM((tm, d), BF16)],
        compiler_params=_cparams(("parallel", "arbitrary")),
        name="ffn",
    )(x, g.reshape(1, d), fg.reshape(1, d), *([w_gu] * (2 * FFN_TILES)), *([w_down] * FFN_TILES))


def _linear_kernel(*refs, n_x, norm, n_main, residual):
    x_refs = refs[:n_x]
    refs = refs[n_x:]
    if norm:
        g_ref, refs = refs[0], refs[1:]
    w_ref, wt_ref = refs[:2]
    refs = refs[2:]
    if residual:
        r_ref, o_ref, h_ref = refs
    else:
        o_ref, h_ref = refs
    j = pl.program_id(1)

    @pl.when(j == 0)
    def _():
        off = 0
        for x_ref in x_refs:
            x = x_ref[...].astype(F32)
            if norm:
                x = _rms(x, g_ref[...])
            h_ref[:, off:off + x.shape[1]] = x.astype(BF16)
            off += x.shape[1]

    def emit(w):
        y = _dot(h_ref[...], w[...].astype(BF16))
        if residual:
            y = y + r_ref[...]
        o_ref[...] = y

    pl.when(j < n_main)(lambda: emit(w_ref))
    pl.when(j >= n_main)(lambda: emit(wt_ref))


def linear(xs, w, w_tail=None, n_main=None, g=None, res=None, tm=TOK_TILE, tn=512):
    xs = list(xs) if isinstance(xs, (list, tuple)) else [xs]
    m = xs[0].shape[0]
    k = sum(x.shape[1] for x in xs)
    if w_tail is None:
        w_tail, n_main = w, w.shape[1] // tn
    n_blocks = n_main + (w_tail.shape[1] // tn if w_tail is not w else 0)
    norm, residual = g is not None, res is not None
    in_specs = [pl.BlockSpec((tm, x.shape[1]), lambda i, j: (i, 0)) for x in xs]
    args = list(xs)
    if norm:
        in_specs.append(pl.BlockSpec((1, k), lambda i, j: (0, 0)))
        args.append(g.reshape(1, k))
    in_specs.append(pl.BlockSpec((k, tn), lambda i, j: (0, jnp.minimum(j, n_main - 1))))
    in_specs.append(pl.BlockSpec((k, tn), lambda i, j: (0, jnp.maximum(j - n_main, 0))))
    args += [w, w_tail]
    if residual:
        in_specs.append(pl.BlockSpec((tm, tn), lambda i, j: (i, j)))
        args.append(res)
    return pl.pallas_call(
        functools.partial(_linear_kernel, n_x=len(xs), norm=norm, n_main=n_main, residual=residual),
        grid=(m // tm, n_blocks),
        in_specs=in_specs,
        out_specs=pl.BlockSpec((tm, tn), lambda i, j: (i, j)),
        out_shape=jax.ShapeDtypeStruct((m, n_blocks * tn), F32),
        scratch_shapes=[pltpu.VMEM((tm, k), BF16)],
        compiler_params=_cparams(("parallel", "arbitrary")),
        name="linear",
    )(*args)


def _compress_kernel(x_ref, pe_ref, w1_ref, w2_ref, o_ref, acc_ref, *, n_valid):
    l = pl.program_id(1)

    @pl.when(l == 0)
    def _():
        acc_ref[...] = jnp.zeros(acc_ref.shape, F32)

    pe = pe_ref[pl.ds(l, 1), :]
    w1 = w1_ref[...].astype(BF16)
    for g in range(NSA_KV_HEADS):
        xg = x_ref[:, g * HEAD_DIM:(g + 1) * HEAD_DIM]
        if n_valid < CMP_BLOCK:
            xg = jnp.where(l < n_valid, xg, 0.0)
        acc_ref[g] += _dot((xg + pe).astype(BF16), w1)

    @pl.when(l == CMP_BLOCK - 1)
    def _():
        w2 = w2_ref[...].astype(BF16)
        for g in range(NSA_KV_HEADS):
            a = acc_ref[g]
            hid = a * jax.nn.sigmoid(a)
            o_ref[:, g * HEAD_DIM:(g + 1) * HEAD_DIM] = _dot(hid.astype(BF16), w2)


def compress(x2d, col_block, n_valid, w1, w2, pe, tr, rows=None, row_block0=0):
    rows = x2d.shape[0] if rows is None else rows
    x_spec = pl.BlockSpec((tr, NSA_KV_W),
                          lambda i, l: (row_block0 + i, col_block(jnp.minimum(l, n_valid - 1))))
    return pl.pallas_call(
        functools.partial(_compress_kernel, n_valid=n_valid),
        grid=(rows // tr, CMP_BLOCK),
        in_specs=[
            x_spec,
            pl.BlockSpec((CMP_BLOCK, HEAD_DIM), lambda i, l: (0, 0)),
            pl.BlockSpec((HEAD_DIM, CMP_HIDDEN), lambda i, l: (l, 0)),
            pl.BlockSpec((CMP_HIDDEN, HEAD_DIM), lambda i, l: (0, 0)),
        ],
        out_specs=pl.BlockSpec((tr, NSA_KV_W), lambda i, l: (i, 0)),
        out_shape=jax.ShapeDtypeStruct((rows, NSA_KV_W), F32),
        scratch_shapes=[pltpu.VMEM((NSA_KV_HEADS, tr, CMP_HIDDEN), F32)],
        compiler_params=_cparams(("parallel", "arbitrary")),
        name="compress",
    )(x2d, pe, w1, w2)


def _compress_pool_kernel(x_ref, pe_ref, w1a_ref, w1b_ref, w2_ref, o_ref, acc_ref):
    lp = pl.program_id(1)
    tr, sub, _ = x_ref.shape
    half = sub // 2

    @pl.when(lp == 0)
    def _():
        acc_ref[...] = jnp.zeros(acc_ref.shape, F32)

    pe2 = pe_ref[pl.ds(pl.multiple_of(2 * lp, 2), 2), :]
    x_even = (x_ref[:, 0:half, :] + pe2[0:1, :][None]).reshape(tr * half, HEAD_DIM).astype(BF16)
    x_odd = (x_ref[:, half:, :] + pe2[1:2, :][None]).reshape(tr * half, HEAD_DIM).astype(BF16)
    acc_ref[...] += _dot(x_even, w1a_ref[...].astype(BF16)) + _dot(x_odd, w1b_ref[...].astype(BF16))

    @pl.when(lp == CMP_BLOCK // 2 - 1)
    def _():
        a = acc_ref[...]
        hid = a * jax.nn.sigmoid(a)
        o_ref[...] = _dot(hid.astype(BF16), w2_ref[...].astype(BF16))


def compress_pool(pool, w1, w2, pe, tr=512):
    n_blk = pool.shape[0] * PAGE_SIZE // CMP_BLOCK
    sub = 2 * NSA_KV_HEADS
    x = pool.reshape(n_blk, CMP_BLOCK * NSA_KV_HEADS, HEAD_DIM)
    out = pl.pallas_call(
        _compress_pool_kernel,
        grid=(n_blk // tr, CMP_BLOCK // 2),
        in_specs=[
            pl.BlockSpec((tr, sub, HEAD_DIM), lambda i, l: (i, l, 0)),
            pl.BlockSpec((CMP_BLOCK, HEAD_DIM), lambda i, l: (0, 0)),
            pl.BlockSpec((HEAD_DIM, CMP_HIDDEN), lambda i, l: (2 * l, 0)),
            pl.BlockSpec((HEAD_DIM, CMP_HIDDEN), lambda i, l: (2 * l + 1, 0)),
            pl.BlockSpec((CMP_HIDDEN, HEAD_DIM), lambda i, l: (0, 0)),
        ],
        out_specs=pl.BlockSpec((tr * NSA_KV_HEADS, HEAD_DIM), lambda i, l: (i, 0)),
        out_shape=jax.ShapeDtypeStruct((n_blk * NSA_KV_HEADS, HEAD_DIM), F32),
        scratch_shapes=[pltpu.VMEM((tr * NSA_KV_HEADS, CMP_HIDDEN), F32)],
        compiler_params=_cparams(("parallel", "arbitrary")),
        name="compress_pool",
    )(x, pe, w1, w1, w2)
    return out.reshape(n_blk, NSA_KV_W)


def _nsa_prompt_kernel(rb_ref, q_ref, ks_ref, vs_ref, kw_ref, vw_ref, ck_ref, cv_ref, gl_ref, bg_ref,
                       qx_ref, mk_ref, mv_ref, o_ref, ox_ref, bt_ref, cb_ref, selx_ref, m_ref, l_ref, acc_ref):
    g = pl.program_id(1)
    qt = pl.program_id(2)
    q0 = qt * Q_TILE
    ii = lax.broadcasted_iota(jnp.int32, (Q_TILE, LANE), 0)
    jj = lax.broadcasted_iota(jnp.int32, (Q_TILE, LANE), 1)

    @pl.when(qt == 0)
    def _():
        d_new = (lax.broadcasted_iota(jnp.int32, (Q_TILE, 1), 0) + 1) % CMP_BLOCK
        for r in range(NSA_GROUP):
            h = NSA_GROUP * g + r
            bt_ref[r, 0] = _t5_bias(ii - jj, rb_ref, h)
            bt_ref[r, 1] = _t5_bias(LANE + ii - jj, rb_ref, h)
            cb_ref[r, 0] = _t5_bias(d_new, rb_ref, h)
            cb_ref[r, 1] = _t5_bias(d_new + CMP_BLOCK, rb_ref, h)

    def heads(f):
        return jnp.concatenate([f(r) for r in range(NSA_GROUP)], axis=0)

    qs = heads(lambda r: q_ref[:, r * HEAD_DIM:(r + 1) * HEAD_DIM]).astype(BF16)
    far_bias = heads(lambda r: jnp.full((Q_TILE, 1), rb_ref[N_BUCKETS - 1, NSA_GROUP * g + r], F32))
    diag_bias = heads(lambda r: bt_ref[r, 0])
    near_bias = heads(lambda r: bt_ref[r, 1])
    causal = jnp.concatenate([jj <= ii] * NSA_GROUP, axis=0)

    nb = ck_ref.shape[0]
    row = lax.broadcasted_iota(jnp.int32, (Q_TILE, nb), 0)
    col = lax.broadcasted_iota(jnp.int32, (Q_TILE, nb), 1)
    qpos = q0 + row
    dist = qpos - (col * CMP_BLOCK + (CMP_BLOCK - 1))
    sc = _dot_nt(qs, ck_ref[...].astype(BF16)) * ATTN_SCALE
    newest = (qpos - (CMP_BLOCK - 1)) // CMP_BLOCK
    probs = []
    for r in range(NSA_GROUP):
        bias = jnp.where(col == newest, cb_ref[r, 0],
                         jnp.where(col == newest - 1, cb_ref[r, 1], rb_ref[N_BUCKETS - 1, NSA_GROUP * g + r]))
        probs.append(_softmax_rows(sc[r * Q_TILE:(r + 1) * Q_TILE] + bias, dist >= 0))
    o_c = _dot(jnp.concatenate(probs, axis=0).astype(BF16), cv_ref[...].astype(BF16))
    imp = probs[0] + probs[1] + probs[2]
    sel = jnp.where(_select_blocks(imp, qpos // SLC_BLOCK, col, nb), 1.0, 0.0).astype(BF16)
    blk_minus_off = (lax.broadcasted_iota(jnp.int32, (nb, LANE), 0)
                     - lax.broadcasted_iota(jnp.int32, (nb, LANE), 1) // SLC_BLOCK)
    for t in range(selx_ref.shape[0]):
        to_keys = jnp.where(blk_minus_off == t * (LANE // SLC_BLOCK), 1.0, 0.0).astype(BF16)
        selx_ref[t] = _dot(sel, to_keys)

    every = jnp.concatenate([jj >= 0] * NSA_GROUP, axis=0)

    def key_tile(ref, t):
        return ref[pl.ds(pl.multiple_of(jnp.maximum(t, 0) * LANE, LANE), LANE), :].astype(BF16)

    def scores(k_ref, t, bias):
        return _dot_nt(qs, key_tile(k_ref, t)) * ATTN_SCALE + bias

    def sel_mask(t):
        return jnp.concatenate([selx_ref[jnp.maximum(t, 0)] > 0.5] * NSA_GROUP, axis=0)

    def pair_update(s0, s1, m0, m1, v_ref, t0, t1):
        def pv(p):
            return _dot(p[:, :LANE], key_tile(v_ref, t0)) + _dot(p[:, LANE:], key_tile(v_ref, t1))
        _flash_update(m_ref, l_ref, acc_ref, jnp.concatenate([s0, s1], axis=1), pv,
                      jnp.concatenate([m0, m1], axis=1), mxu_sum=True)

    m_ref[...] = jnp.full(m_ref.shape, NEG_INF, F32)
    l_ref[...] = jnp.zeros(l_ref.shape, F32)
    acc_ref[...] = jnp.zeros(acc_ref.shape, F32)
    n_far = jnp.maximum(qt - 1, 0)

    def far(it, carry):
        t0 = 2 * it
        t1 = jnp.minimum(t0 + 1, n_far - 1)
        pair_update(scores(ks_ref, t0, far_bias), scores(ks_ref, t1, far_bias),
                    sel_mask(t0), sel_mask(t1) & (t0 + 1 < n_far), vs_ref, t0, t1)
        return carry

    lax.fori_loop(0, (n_far + 1) // 2, far, 0)
    pair_update(scores(ks_ref, qt - 1, near_bias), scores(ks_ref, qt, diag_bias),
                sel_mask(qt - 1) & (qt >= 1), sel_mask(qt) & causal, vs_ref, qt - 1, qt)
    o_s = acc_ref[...] / l_ref[...]

    n_win = WINDOW // LANE
    upper = jnp.concatenate([jj > ii] * NSA_GROUP, axis=0)
    band_s, band_m = [], []
    for back in range(n_win, -1, -1):
        bias = diag_bias if back == 0 else near_bias if back == 1 else far_bias
        base = causal if back == 0 else upper if back == n_win else every
        band_s.append(scores(kw_ref, qt - back, bias))
        band_m.append(base & (qt >= back))
    pw = _softmax_rows(jnp.concatenate(band_s, axis=1), jnp.concatenate(band_m, axis=1)).astype(BF16)
    o_w = sum(_dot(pw[:, i * LANE:(i + 1) * LANE], key_tile(vw_ref, qt - back))
              for i, back in enumerate(range(n_win, -1, -1)))

    gate = jax.nn.sigmoid(gl_ref[...] + bg_ref[...])
    for r in range(NSA_GROUP):
        rows = slice(r * Q_TILE, (r + 1) * Q_TILE)
        o = (gate[:, 3 * r:3 * r + 1] * o_c[rows] + gate[:, 3 * r + 1:3 * r + 2] * o_s[rows]
             + gate[:, 3 * r + 2:3 * r + 3] * o_w[rows])
        o_ref[:, r * HEAD_DIM:(r + 1) * HEAD_DIM] = o.astype(o_ref.dtype)

    sx = _dot_nt(qx_ref[...].astype(BF16), mk_ref[...].astype(BF16)) * ATTN_SCALE
    px = _softmax_rows(sx, sx == sx)
    ox_ref[...] = _dot(px.astype(BF16), mv_ref[...].astype(BF16)).astype(ox_ref.dtype)


def nsa_prompt(proj, ck, cv, kvm, rel_bias, b_gate, batch):
    nq = SEQ // Q_TILE
    nb = SEQ // CMP_BLOCK
    hb = lambda c: c // HEAD_DIM
    qrow = lambda b, g, t: b * nq + t
    grid_spec = pltpu.PrefetchScalarGridSpec(
        num_scalar_prefetch=0,
        grid=(batch, NSA_KV_HEADS, nq),
        in_specs=[
            pl.BlockSpec(memory_space=pltpu.SMEM),
            pl.BlockSpec((Q_TILE, NSA_GROUP * HEAD_DIM), lambda b, g, t: (qrow(b, g, t), g)),
            pl.BlockSpec((SEQ, HEAD_DIM), lambda b, g, t: (b, hb(SELF_W + 2 * NSA_KV_W) + g)),
            pl.BlockSpec((SEQ, HEAD_DIM), lambda b, g, t: (b, hb(SELF_W + 3 * NSA_KV_W) + g)),
            pl.BlockSpec((SEQ, HEAD_DIM), lambda b, g, t: (b, hb(SELF_W + 4 * NSA_KV_W) + g)),
            pl.BlockSpec((SEQ, HEAD_DIM), lambda b, g, t: (b, hb(SELF_W + 5 * NSA_KV_W) + g)),
            pl.BlockSpec((nb, HEAD_DIM), lambda b, g, t: (b, g)),
            pl.BlockSpec((nb, HEAD_DIM), lambda b, g, t: (b, g)),
            pl.BlockSpec((Q_TILE, LANE), lambda b, g, t: (qrow(b, g, t), hb(COL_GATE) + g)),
            pl.BlockSpec((1, LANE), lambda b, g, t: (0, g)),
            pl.BlockSpec((Q_TILE, HEAD_DIM), lambda b, g, t: (qrow(b, g, t), hb(COL_QX) + g)),
            pl.BlockSpec((MEM_LEN, HEAD_DIM), lambda b, g, t: (b, g)),
            pl.BlockSpec((MEM_LEN, HEAD_DIM), lambda b, g, t: (b, N_CROSS_HEADS + g)),
        ],
        out_specs=[
            pl.BlockSpec((Q_TILE, NSA_GROUP * HEAD_DIM), lambda b, g, t: (qrow(b, g, t), g)),
            pl.BlockSpec((Q_TILE, HEAD_DIM), lambda b, g, t: (qrow(b, g, t), g)),
        ],
        scratch_shapes=[
            pltpu.VMEM((NSA_GROUP, 2, Q_TILE, LANE), F32),
            pltpu.VMEM((NSA_GROUP, 2, Q_TILE, 1), F32),
            pltpu.VMEM((SEQ // LANE, Q_TILE, LANE), F32),
            pltpu.VMEM((NSA_GROUP * Q_TILE, 1), F32),
            pltpu.VMEM((NSA_GROUP * Q_TILE, 1), F32),
            pltpu.VMEM((NSA_GROUP * Q_TILE, HEAD_DIM), F32),
        ],
    )
    return pl.pallas_call(
        _nsa_prompt_kernel,
        grid_spec=grid_spec,
        out_shape=[jax.ShapeDtypeStruct((batch * SEQ, SELF_W), BF16),
                   jax.ShapeDtypeStruct((batch * SEQ, CROSS_W), BF16)],
        compiler_params=_cparams(("arbitrary", "arbitrary", "arbitrary")),
        name="nsa_prompt",
    )(rel_bias, proj, proj, proj, proj, proj, ck, cv, proj, b_gate, proj, kvm, kvm)


FOX_TILE = 256
HEAD_PAD = 16


def _log_sigmoid(x):
    return jnp.minimum(x, 0.0) - jnp.log1p(jnp.exp(-jnp.abs(x)))


def _fox_prep_kernel(fl_ref, bf_ref, lf_ref, ct_ref):
    n_chunks = fl_ref.shape[0] // LANE
    ii = lax.broadcasted_iota(jnp.int32, (LANE, LANE), 0)
    jj = lax.broadcasted_iota(jnp.int32, (LANE, LANE), 1)
    tri = jnp.where(jj <= ii, 1.0, 0.0).astype(F32)
    carry = jnp.zeros((1, LANE), F32)
    per_tile = FOX_TILE // LANE
    for c in range(n_chunks):
        lf = _log_sigmoid(fl_ref[c * LANE:(c + 1) * LANE, :] + bf_ref[...])
        lf_ref[c * LANE:(c + 1) * LANE, :] = lf
        cs = jnp.dot(tri, lf, precision=lax.Precision.HIGHEST, preferred_element_type=F32) + carry
        carry = cs[LANE - 1:LANE, :]
        ct_ref[:, c // per_tile, (c % per_tile) * LANE:(c % per_tile + 1) * LANE] = cs.T[:HEAD_PAD]


def fox_prep(proj, b_forget, batch, seq_len=SEQ, row_block0=0):
    bf = jnp.pad(b_forget.reshape(1, FOX_HEADS), ((0, 0), (0, LANE - FOX_HEADS)))
    n_tiles = max(seq_len // FOX_TILE, 1)
    return pl.pallas_call(
        _fox_prep_kernel,
        grid=(batch,),
        in_specs=[pl.BlockSpec((seq_len, LANE), lambda b: (row_block0 + b, COL_GATE // LANE)),
                  pl.BlockSpec((1, LANE), lambda b: (0, 0))],
        out_specs=[pl.BlockSpec((seq_len, LANE), lambda b: (b, 0)),
                   pl.BlockSpec((None, HEAD_PAD, n_tiles, FOX_TILE), lambda b: (b, 0, 0, 0))],
        out_shape=[jax.ShapeDtypeStruct((batch * seq_len, LANE), F32),
                   jax.ShapeDtypeStruct((batch, HEAD_PAD, n_tiles, FOX_TILE), F32)],
        compiler_params=_cparams(("arbitrary",)),
        name="fox_prep",
    )(proj, bf)


FOX_HEAD_BLOCK = 4


def _fox_prompt_kernel(q_ref, k_ref, v_ref, c_ref, o_ref, m_ref, l_ref, acc_ref):
    qt = pl.program_id(2)
    nh = FOX_HEAD_BLOCK
    hs = [slice(h * HEAD_DIM, (h + 1) * HEAD_DIM) for h in range(nh)]
    q = [q_ref[:, s].astype(BF16) for s in hs]
    m_ref[...] = jnp.full(m_ref.shape, NEG_INF, F32)
    l_ref[...] = jnp.zeros(l_ref.shape, F32)
    acc_ref[...] = jnp.zeros(acc_ref.shape, F32)

    def tile(t, mask):
        rows = pl.ds(pl.multiple_of(t * FOX_TILE, FOX_TILE), FOX_TILE)
        s = jnp.concatenate([_dot_nt(q[h], k_ref[rows, hs[h]].astype(BF16)) * ATTN_SCALE
                             - c_ref[h, pl.ds(t, 1), :] for h in range(nh)], axis=0)

        def pv(p):
            return jnp.concatenate([_dot(p[h * FOX_TILE:(h + 1) * FOX_TILE], v_ref[rows, hs[h]].astype(BF16))
                                    for h in range(nh)], axis=0)

        _flash_update(m_ref, l_ref, acc_ref, s, pv, mask, mxu_sum=True)

    def far(t, carry):
        tile(t, None)
        return carry

    lax.fori_loop(0, qt, far, 0)
    ii = lax.broadcasted_iota(jnp.int32, (nh * FOX_TILE, FOX_TILE), 0) % FOX_TILE
    jj = lax.broadcasted_iota(jnp.int32, (nh * FOX_TILE, FOX_TILE), 1)
    tile(qt, jj <= ii)
    o = acc_ref[...] / l_ref[...]
    for h in range(nh):
        o_ref[:, hs[h]] = o[h * FOX_TILE:(h + 1) * FOX_TILE].astype(o_ref.dtype)


def fox_prompt(proj, ct, batch):
    nq = SEQ // FOX_TILE
    nh = FOX_HEAD_BLOCK
    wb = nh * HEAD_DIM
    cb = lambda c: c // wb
    rows = nh * FOX_TILE
    return pl.pallas_call(
        _fox_prompt_kernel,
        grid=(batch, FOX_HEADS // nh, nq),
        in_specs=[
            pl.BlockSpec((FOX_TILE, wb), lambda b, h, t: (b * nq + t, h)),
            pl.BlockSpec((SEQ, wb), lambda b, h, t: (b, cb(SELF_W) + h)),
            pl.BlockSpec((SEQ, wb), lambda b, h, t: (b, cb(2 * SELF_W) + h)),
            pl.BlockSpec((None, nh, nq, FOX_TILE), lambda b, h, t: (b, h, 0, 0)),
        ],
        out_specs=pl.BlockSpec((FOX_TILE, wb), lambda b, h, t: (b * nq + t, h)),
        out_shape=jax.ShapeDtypeStruct((batch * SEQ, SELF_W), BF16),
        scratch_shapes=[pltpu.VMEM((rows, 1), F32), pltpu.VMEM((rows, 1), F32),
                        pltpu.VMEM((rows, HEAD_DIM), F32)],
        compiler_params=_cparams(("arbitrary", "arbitrary", "arbitrary")),
        name="fox_prompt",
    )(proj, proj, proj, ct)


def _cross_prompt_kernel(q_ref, mk_ref, mv_ref, o_ref):
    for h in range(N_CROSS_HEADS):
        hs = slice(h * HEAD_DIM, (h + 1) * HEAD_DIM)
        s = _dot_nt(q_ref[:, hs].astype(BF16), mk_ref[:, hs].astype(BF16)) * ATTN_SCALE
        p = _softmax_rows(s, s == s)
        o_ref[:, hs] = _dot(p.astype(BF16), mv_ref[:, hs].astype(BF16)).astype(o_ref.dtype)


def cross_prompt(proj, kvm, batch, tq=512):
    nq = SEQ // tq
    return pl.pallas_call(
        _cross_prompt_kernel,
        grid=(batch, nq),
        in_specs=[pl.BlockSpec((tq, CROSS_W), lambda b, t: (b * nq + t, COL_QX // CROSS_W)),
                  pl.BlockSpec((MEM_LEN, CROSS_W), lambda b, t: (b, 0)),
                  pl.BlockSpec((MEM_LEN, CROSS_W), lambda b, t: (b, 1))],
        out_specs=pl.BlockSpec((tq, CROSS_W), lambda b, t: (b * nq + t, 0)),
        out_shape=jax.ShapeDtypeStruct((batch * SEQ, CROSS_W), BF16),
        compiler_params=_cparams(("arbitrary", "arbitrary")),
        name="cross_prompt",
    )(proj, kvm, kvm)


ROWS = 16


def _pad_rows(x):
    return jnp.pad(x, [(0, 0)] * (x.ndim - 2) + [(0, ROWS - x.shape[-2]), (0, 0)])


def _group_rows(q):
    b = q.shape[0]
    q = q.reshape(b, DEC_SEQ, NSA_KV_HEADS, NSA_GROUP, HEAD_DIM).transpose(0, 2, 3, 1, 4)
    return _pad_rows(q.reshape(b, NSA_KV_HEADS, NSA_GROUP * DEC_SEQ, HEAD_DIM))


def _ungroup_rows(o):
    b = o.shape[0]
    o = o[:, :, :NSA_GROUP * DEC_SEQ].reshape(b, NSA_KV_HEADS, NSA_GROUP, DEC_SEQ, HEAD_DIM)
    return o.transpose(0, 3, 1, 2, 4).reshape(b * DEC_SEQ, SELF_W)


def _gate_rows(gl):
    b = gl.shape[0]
    gl = gl.reshape(b, DEC_SEQ, NSA_KV_HEADS, LANE).transpose(0, 2, 1, 3)
    return _pad_rows(jnp.tile(gl, (1, 1, NSA_GROUP, 1)))


def _head_rows(q, n_heads):
    b = q.shape[0]
    return _pad_rows(q.reshape(b, DEC_SEQ, n_heads, HEAD_DIM).transpose(0, 2, 1, 3))


def _unhead_rows(o):
    b, h = o.shape[:2]
    return o[:, :, :DEC_SEQ].transpose(0, 2, 1, 3).reshape(b * DEC_SEQ, h * HEAD_DIM)


def _row_ids():
    row = lax.broadcasted_iota(jnp.int32, (ROWS, LANE), 0)
    return row, row % DEC_SEQ, row // DEC_SEQ


def _per_head(r_idx, f):
    return jnp.where(r_idx == 0, f(0), jnp.where(r_idx == 1, f(1), f(2)))


def _safe_div(acc, l):
    return acc / jnp.where(l > 0.0, l, 1.0)


def _merge_new_keys(fl, q32, kn, vn, bias, masks):
    s_new = [jnp.where(masks[j], jnp.sum(q32 * kn[j:j + 1, :], axis=-1, keepdims=True) * ATTN_SCALE
                       + bias[j], NEG_INF) for j in range(DEC_SEQ)]
    m_old = fl.m[...]
    m_new = m_old
    for s in s_new:
        m_new = jnp.maximum(m_new, s)
    alpha = jnp.exp(m_old - m_new)
    l = alpha * fl.l[...]
    acc = alpha * fl.acc[...]
    for j in range(DEC_SEQ):
        p = jnp.where(masks[j], jnp.exp(s_new[j] - m_new), 0.0)
        l = l + p
        acc = acc + p * vn[j:j + 1, :]
    return _safe_div(acc, l)


PAGES_PER_STEP = 8
NSA_PAGES_PER_STEP = 16
SUB = 8


def _flash_update(m_ref, l_ref, acc_ref, s, pv_fn, mask=None, mxu_sum=False):
    if mask is not None:
        s = jnp.where(mask, s, NEG_INF)
    m_old = m_ref[...]
    m_new = jnp.maximum(m_old, jnp.max(s, axis=-1, keepdims=True))
    alpha = jnp.exp(m_old - m_new)
    p = jnp.exp(s - m_new)
    if mask is not None:
        p = jnp.where(mask, p, 0.0)
    if mxu_sum:
        p = p.astype(BF16)
        row_sum = _dot(p, jnp.ones((p.shape[1], LANE), BF16))[:, 0:1]
    else:
        row_sum = jnp.sum(p, axis=-1, keepdims=True)
        p = p.astype(BF16)
    l_ref[...] = alpha * l_ref[...] + row_sum
    acc_ref[...] = alpha * acc_ref[...] + pv_fn(p)
    m_ref[...] = m_new


def _cumsum_lanes_kernel(x_ref, o_ref):
    ii = lax.broadcasted_iota(jnp.int32, (LANE, LANE), 0)
    jj = lax.broadcasted_iota(jnp.int32, (LANE, LANE), 1)
    upper = jnp.where(ii <= jj, 1.0, 0.0).astype(F32)
    o_ref[...] = jnp.dot(x_ref[...], upper, precision=lax.Precision.HIGHEST, preferred_element_type=F32)


def cumsum_lanes(x, tr=512):
    h, rows, _ = x.shape
    spec = pl.BlockSpec((None, tr, LANE), lambda i, j: (i, j, 0))
    return pl.pallas_call(
        _cumsum_lanes_kernel, grid=(h, rows // tr), in_specs=[spec], out_specs=spec,
        out_shape=jax.ShapeDtypeStruct(x.shape, F32),
        compiler_params=_cparams(("arbitrary", "arbitrary")), name="cumsum_lanes",
    )(x)


def _fox_sample_kernel(pt_ref, q_ref, kn_ref, vn_ref, *refs):
    n = PAGES_PER_STEP
    k_refs, v_refs, c_refs = refs[:n], refs[n:2 * n], refs[2 * n:3 * n]
    lfn_ref, o_ref, m_ref, l_ref, acc_ref, run_ref = refs[3 * n:]
    b = pl.program_id(0)
    step = pl.program_id(1)

    @pl.when(step == 0)
    def _():
        m_ref[...] = jnp.full(m_ref.shape, NEG_INF, F32)
        l_ref[...] = jnp.zeros(l_ref.shape, F32)
        acc_ref[...] = jnp.zeros(acc_ref.shape, F32)
        run_ref[...] = jnp.zeros(run_ref.shape, F32)

    sub = [pt_ref[b, step * n + j] % SUB for j in range(n)]
    s_heads = []
    for h in range(FOX_HEADS):
        q = q_ref[h].astype(BF16)
        run = run_ref[h:h + 1, :]
        parts = []
        for j in range(n):
            ck = run + c_refs[j][h, pl.ds(sub[j], 1), :]
            parts.append(_dot_nt(q, k_refs[j][h].astype(BF16)) * ATTN_SCALE - ck)
            run = jnp.broadcast_to(ck[:, PAGE_SIZE - 1:PAGE_SIZE], (1, LANE))
        run_ref[h:h + 1, :] = run
        s_heads.append(jnp.concatenate(parts, axis=1))
    s = jnp.concatenate(s_heads, axis=0)

    def pv(p):
        return jnp.concatenate(
            [sum(_dot(p[h * ROWS:(h + 1) * ROWS, j * PAGE_SIZE:(j + 1) * PAGE_SIZE], v_refs[j][h].astype(BF16))
                 for j in range(n)) for h in range(FOX_HEADS)], axis=0)

    _flash_update(m_ref, l_ref, acc_ref, s, pv)

    @pl.when(step == N_PAGES // n - 1)
    def _():
        row = lax.broadcasted_iota(jnp.int32, (ROWS, 1), 0)
        lfn = lfn_ref[...]
        for h in range(FOX_HEADS):
            hs = slice(h * HEAD_DIM, (h + 1) * HEAD_DIM)
            rs = slice(h * ROWS, (h + 1) * ROWS)
            c, c_new = run_ref[h:h + 1, 0:1], []
            for j in range(DEC_SEQ):
                c = c + lfn[h:h + 1, j:j + 1]
                c_new.append(-c)
            fl = _Flash(m_ref.at[rs], l_ref.at[rs], acc_ref.at[rs])
            o_ref[h] = _merge_new_keys(fl, q_ref[h], kn_ref[:, hs], vn_ref[:, hs], c_new,
                                       [row >= j for j in range(DEC_SEQ)])


def fox_sample(q16, ps3, pool_k, pool_v, ct_pool, lfn, page_table):
    nb = q16.shape[0]
    n = PAGES_PER_STEP
    page = lambda j: pl.BlockSpec((None, FOX_HEADS, PAGE_SIZE, HEAD_DIM),
                                  lambda b, s, pt: (pt[b, s * n + j], 0, 0, 0))
    cpage = lambda j: pl.BlockSpec((FOX_HEADS, SUB, PAGE_SIZE), lambda b, s, pt: (0, pt[b, s * n + j] // SUB, 0))
    rows = FOX_HEADS * ROWS
    grid_spec = pltpu.PrefetchScalarGridSpec(
        num_scalar_prefetch=1,
        grid=(nb, N_PAGES // n),
        in_specs=[
            pl.BlockSpec((None, FOX_HEADS, ROWS, HEAD_DIM), lambda b, s, pt: (b, 0, 0, 0)),
            pl.BlockSpec((None, DEC_SEQ, SELF_W), lambda b, s, pt: (b, 0, 1)),
            pl.BlockSpec((None, DEC_SEQ, SELF_W), lambda b, s, pt: (b, 0, 2)),
        ] + [page(j) for j in range(n)] + [page(j) for j in range(n)] + [cpage(j) for j in range(n)] + [
            pl.BlockSpec((None, HEAD_PAD, DEC_SEQ), lambda b, s, pt: (b, 0, 0)),
        ],
        out_specs=pl.BlockSpec((None, FOX_HEADS, ROWS, HEAD_DIM), lambda b, s, pt: (b, 0, 0, 0)),
        scratch_shapes=[pltpu.VMEM((rows, 1), F32), pltpu.VMEM((rows, 1), F32),
                        pltpu.VMEM((rows, HEAD_DIM), F32), pltpu.VMEM((HEAD_PAD, LANE), F32)],
    )
    return pl.pallas_call(
        _fox_sample_kernel,
        grid_spec=grid_spec,
        out_shape=jax.ShapeDtypeStruct((nb, FOX_HEADS, ROWS, HEAD_DIM), F32),
        compiler_params=_cparams(("arbitrary", "arbitrary")),
        name="fox_sample",
    )(page_table, q16, ps3, ps3, *([pool_k] * n), *([pool_v] * n), *([ct_pool] * n), lfn)


CMP_SEQS_PER_STEP = 2


def _nsa_sample_cmp_kernel(pt_ref, rb_ref, q_ref, *refs):
    ns = CMP_SEQS_PER_STEP
    n_in = 2 * N_PAGES
    pages = [refs[i * n_in:(i + 1) * n_in] for i in range(ns)]
    ckn_ref, cvn_ref, oc_ref, sx_ref, ck_s, cv_s, bias_ref = refs[ns * n_in:]
    b = pl.program_id(0)
    blk_per_page = PAGE_SIZE // CMP_BLOCK
    row, tok, r_idx = _row_ids()
    lane = lax.broadcasted_iota(jnp.int32, (ROWS, LANE), 1)
    qpos = PAST_LEN + tok
    dist = qpos - (lane * CMP_BLOCK + (CMP_BLOCK - 1))
    valid = (dist >= 0) & (lane <= N_PAST_BLK)

    @pl.when(b == 0)
    def _():
        ck_s[...] = jnp.zeros(ck_s.shape, F32)
        cv_s[...] = jnp.zeros(cv_s.shape, F32)
        for g in range(NSA_KV_HEADS):
            bias_ref[g] = _per_head(r_idx, lambda r: _t5_bias(dist, rb_ref, NSA_GROUP * g + r))

    for i in range(ns):
        for p in range(N_PAGES):
            ck_s[i, p * blk_per_page:(p + 1) * blk_per_page, :] = pages[i][p][...]
            cv_s[i, p * blk_per_page:(p + 1) * blk_per_page, :] = pages[i][N_PAGES + p][...]
        ck_s[i, N_PAST_BLK:N_PAST_BLK + 1, :] = ckn_ref[i]
        cv_s[i, N_PAST_BLK:N_PAST_BLK + 1, :] = cvn_ref[i]
        for g in range(NSA_KV_HEADS):
            gs = slice(g * HEAD_DIM, (g + 1) * HEAD_DIM)
            s = _dot_nt(q_ref[i, g].astype(BF16), ck_s[i, :, gs].astype(BF16)) * ATTN_SCALE + bias_ref[g]
            p = _softmax_rows(s, valid)
            oc_ref[i, g] = _dot(p.astype(BF16), cv_s[i, :, gs].astype(BF16))
            p0 = jnp.where(row < NSA_GROUP * DEC_SEQ, p, 0.0)
            imp = p0 + pltpu.roll(p0, ROWS - DEC_SEQ, 0) + pltpu.roll(p0, ROWS - 2 * DEC_SEQ, 0)
            imp = jnp.where(row < DEC_SEQ, imp, 0.0)
            imp = imp + pltpu.roll(imp, DEC_SEQ, 0) + pltpu.roll(imp, 2 * DEC_SEQ, 0)
            sel = _select_blocks(imp, qpos // SLC_BLOCK, lane, N_PAST_BLK + 1)
            sx_ref[i, g] = jnp.where(sel, 1.0, 0.0)


def nsa_sample_cmp(qg16, ck_pool, cv_pool, ck_new, cv_new, rel_bias, page_table):
    nb = qg16.shape[0]
    ns = CMP_SEQS_PER_STEP
    page = lambda i, p: pl.BlockSpec((None, PAGE_SIZE // CMP_BLOCK, NSA_KV_W),
                                     lambda b, pt: (pt[b * ns + i, p], 0, 0))
    seq_pages = lambda i: [page(i, p) for p in range(N_PAGES)] * 2
    qspec = pl.BlockSpec((ns, NSA_KV_HEADS, ROWS, HEAD_DIM), lambda b, pt: (b, 0, 0, 0))
    grid_spec = pltpu.PrefetchScalarGridSpec(
        num_scalar_prefetch=1,
        grid=(nb // ns,),
        in_specs=[pl.BlockSpec(memory_space=pltpu.SMEM), qspec]
        + [spec for i in range(ns) for spec in seq_pages(i)]
        + [pl.BlockSpec((ns, 1, NSA_KV_W), lambda b, pt: (b, 0, 0))] * 2,
        out_specs=[qspec, pl.BlockSpec((ns, NSA_KV_HEADS, ROWS, LANE), lambda b, pt: (b, 0, 0, 0))],
        scratch_shapes=[pltpu.VMEM((ns, LANE, NSA_KV_W), F32), pltpu.VMEM((ns, LANE, NSA_KV_W), F32),
                        pltpu.VMEM((NSA_KV_HEADS, ROWS, LANE), F32)],
    )
    pools = [pool for _ in range(ns) for pool in [ck_pool] * N_PAGES + [cv_pool] * N_PAGES]
    return pl.pallas_call(
        _nsa_sample_cmp_kernel,
        grid_spec=grid_spec,
        out_shape=[jax.ShapeDtypeStruct((nb, NSA_KV_HEADS, ROWS, HEAD_DIM), F32),
                   jax.ShapeDtypeStruct((nb, NSA_KV_HEADS, ROWS, LANE), F32)],
        compiler_params=_cparams(("arbitrary",)),
        name="nsa_sample_cmp",
    )(page_table, rel_bias, qg16, *pools, ck_new, cv_new)


def _nsa_sample_kernel(pt_ref, rb_ref, q_ref, *refs):
    n = NSA_PAGES_PER_STEP
    n_steps = N_PAGES // n
    k_refs, v_refs = refs[:n], refs[n:2 * n]
    (sx_ref, kn_ref, vn_ref, wk_ref, wv_ref, wkn_ref, wvn_ref, wkr_ref, wvr_ref, oc_ref, gl_ref, bg_ref,
     o_ref, wko_ref, wvo_ref, m_ref, l_ref, acc_ref, ow_ref, b15_ref, bn_ref) = refs[2 * n:]
    b = pl.program_id(0)
    p = pl.program_id(1)
    ng = NSA_KV_HEADS
    rows = ng * ROWS
    page_w = PAGE_SIZE * ng
    row, tok, r_idx = _row_ids()
    lane = lax.broadcasted_iota(jnp.int32, (ROWS, LANE), 1)
    tok1, r1 = tok[:, 0:1], r_idx[:, 0:1]
    new_vis = [tok1 >= j for j in range(DEC_SEQ)]
    q_all = q_ref[...].reshape(rows, HEAD_DIM).astype(BF16)
    grp_of_row = lax.broadcasted_iota(jnp.int32, (rows, 1), 0) // ROWS
    tok_of_row = lax.broadcasted_iota(jnp.int32, (rows, 1), 0) % DEC_SEQ

    def stack(f):
        return jnp.concatenate([f(g) for g in range(ng)], axis=0)

    far_bias = stack(lambda g: _per_head(r1, lambda r: rb_ref[N_BUCKETS - 1, NSA_GROUP * g + r]))

    def own_group(width):
        return lax.broadcasted_iota(jnp.int32, (rows, width), 1) % ng == grp_of_row

    @pl.when((b == 0) & (p == 0))
    def _():
        c_tok = lax.broadcasted_iota(jnp.int32, (ROWS, page_w), 1) // ng
        t_row = lax.broadcasted_iota(jnp.int32, (ROWS, page_w), 0) % DEC_SEQ
        r_row = lax.broadcasted_iota(jnp.int32, (ROWS, page_w), 0) // DEC_SEQ
        for g in range(ng):
            b15_ref[g * ROWS:(g + 1) * ROWS, :] = _per_head(
                r_row, lambda r: _t5_bias(LANE + t_row - c_tok, rb_ref, NSA_GROUP * g + r))
            bn_ref[g] = _per_head(r_idx, lambda r: _t5_bias(tok - lane, rb_ref, NSA_GROUP * g + r))

    def new_scores(kn):
        return [stack(lambda g: jnp.sum(q_ref[g] * kn[j:j + 1, g * HEAD_DIM:(g + 1) * HEAD_DIM], axis=-1,
                                        keepdims=True) * ATTN_SCALE + bn_ref[g][:, j:j + 1])
                for j in range(DEC_SEQ)]

    def new_values(vn, j):
        return stack(lambda g: jnp.broadcast_to(vn[j:j + 1, g * HEAD_DIM:(g + 1) * HEAD_DIM], (ROWS, HEAD_DIM)))

    @pl.when(p == 0)
    def _():
        m_ref[...] = jnp.full(m_ref.shape, NEG_INF, F32)
        l_ref[...] = jnp.zeros(l_ref.shape, F32)
        acc_ref[...] = jnp.zeros(acc_ref.shape, F32)
        width = WINDOW * ng
        s = _dot_nt(q_all, wk_ref[...].astype(BF16)) * ATTN_SCALE
        bias = jnp.concatenate([jnp.broadcast_to(far_bias, (rows, width - page_w)), b15_ref[...]], axis=1)
        c_tok = lax.broadcasted_iota(jnp.int32, (rows, width), 1) // ng
        in_win = own_group(width) & (c_tok > tok_of_row)
        s = jnp.where(in_win, s + bias, NEG_INF)
        vis = [tok_of_row >= j for j in range(DEC_SEQ)]
        s_new = [jnp.where(vis[j], sn, NEG_INF) for j, sn in enumerate(new_scores(wkn_ref))]
        mx = jnp.max(s, axis=-1, keepdims=True)
        for sn in s_new:
            mx = jnp.maximum(mx, sn)
        e = jnp.where(in_win, jnp.exp(s - mx), 0.0)
        den = jnp.sum(e, axis=-1, keepdims=True)
        acc = _dot(e.astype(BF16), wv_ref[...].astype(BF16))
        for j in range(DEC_SEQ):
            pj = jnp.where(vis[j], jnp.exp(s_new[j] - mx), 0.0)
            den = den + pj
            acc = acc + pj * new_values(wvn_ref, j)
        ow_ref[...] = _safe_div(acc, den)
        keep = (WINDOW - DEC_SEQ) * ng
        wko_ref[0:keep, :] = wk_ref[DEC_SEQ * ng:, :]
        wvo_ref[0:keep, :] = wv_ref[DEC_SEQ * ng:, :]
        wko_ref[keep:, :] = wkr_ref[...]
        wvo_ref[keep:, :] = wvr_ref[...]

    blk_per_page = PAGE_SIZE // SLC_BLOCK
    blk = lax.broadcasted_iota(jnp.int32, (LANE, page_w), 0)
    off = lax.broadcasted_iota(jnp.int32, (LANE, page_w), 1) // (SLC_BLOCK * ng)
    sel = sx_ref[...].reshape(rows, LANE).astype(BF16)
    own = own_group(page_w)
    s_parts, m_parts = [], []
    for j in range(n):
        bias = far_bias
        if j == n - 1:
            bias = jnp.where(p == n_steps - 1, b15_ref[...], bias)
        s_parts.append(_dot_nt(q_all, k_refs[j][...].astype(BF16)) * ATTN_SCALE + bias)
        to_cols = jnp.where(blk - off == (p * n + j) * blk_per_page, 1.0, 0.0).astype(BF16)
        m_parts.append((_dot(sel, to_cols) > 0.5) & own)

    def pv(pr):
        return sum(_dot(pr[:, j * page_w:(j + 1) * page_w], v_refs[j][...].astype(BF16)) for j in range(n))

    _flash_update(m_ref, l_ref, acc_ref, jnp.concatenate(s_parts, axis=1), pv, jnp.concatenate(m_parts, axis=1))

    @pl.when(p == n_steps - 1)
    def _():
        gate = jax.nn.sigmoid(gl_ref[...] + bg_ref[...].reshape(ng, 1, LANE))
        picked = stack(lambda g: sx_ref[g][:, N_PAST_BLK:N_PAST_BLK + 1]) > 0.5
        vis = [(tok_of_row >= j) & picked for j in range(DEC_SEQ)]
        s_new = [jnp.where(vis[j], sn, NEG_INF) for j, sn in enumerate(new_scores(kn_ref))]
        m_old = m_ref[...]
        m_new = m_old
        for sn in s_new:
            m_new = jnp.maximum(m_new, sn)
        alpha = jnp.exp(m_old - m_new)
        den = alpha * l_ref[...]
        acc = alpha * acc_ref[...]
        for j in range(DEC_SEQ):
            pj = jnp.where(vis[j], jnp.exp(s_new[j] - m_new), 0.0)
            den = den + pj
            acc = acc + pj * new_values(vn_ref, j)
        o_s = _safe_div(acc, den)
        o_w = ow_ref[...]
        for g in range(ng):
            rs = slice(g * ROWS, (g + 1) * ROWS)
            gt = [_per_head(r1, lambda r: gate[g][:, 3 * r + c:3 * r + c + 1]) for c in range(3)]
            o_ref[g] = gt[0] * oc_ref[g] + gt[1] * o_s[rs] + gt[2] * o_w[rs]


def nsa_sample(qg16, ps3, pool_k, pool_v, selx, buf_k, buf_v, o_c, gl16, b_gate, rel_bias, page_table):
    nb = qg16.shape[0]
    win_cols = slice(SELF_W + 4 * NSA_KV_W, SELF_W + 6 * NSA_KV_W)
    new_rows = ps3[:, :, win_cols].reshape(nb, DEC_SEQ, 2, NSA_KV_HEADS, HEAD_DIM)
    wk_rows = new_rows[:, :, 0].reshape(nb, DEC_SEQ * NSA_KV_HEADS, HEAD_DIM)
    wv_rows = new_rows[:, :, 1].reshape(nb, DEC_SEQ * NSA_KV_HEADS, HEAD_DIM)
    rspec = pl.BlockSpec((None, DEC_SEQ * NSA_KV_HEADS, HEAD_DIM), lambda b, p, pt: (b, 0, 0))
    n = NSA_PAGES_PER_STEP
    cb = lambda c: c // NSA_KV_W
    qspec = pl.BlockSpec((None, NSA_KV_HEADS, ROWS, HEAD_DIM), lambda b, p, pt: (b, 0, 0, 0))
    new = lambda c: pl.BlockSpec((None, DEC_SEQ, NSA_KV_W), lambda b, p, pt: (b, 0, cb(c)))
    page_w = PAGE_SIZE * NSA_KV_HEADS
    pool_k = pool_k.reshape(-1, page_w, HEAD_DIM)
    pool_v = pool_v.reshape(-1, page_w, HEAD_DIM)
    buf_k = buf_k.reshape(nb, WINDOW * NSA_KV_HEADS, HEAD_DIM)
    buf_v = buf_v.reshape(nb, WINDOW * NSA_KV_HEADS, HEAD_DIM)
    page = lambda j: pl.BlockSpec((None, page_w, HEAD_DIM), lambda b, p, pt: (pt[b, p * n + j], 0, 0))
    wspec = pl.BlockSpec((None, WINDOW * NSA_KV_HEADS, HEAD_DIM), lambda b, p, pt: (b, 0, 0))
    rows = NSA_KV_HEADS * ROWS
    grid_spec = pltpu.PrefetchScalarGridSpec(
        num_scalar_prefetch=1,
        grid=(nb, N_PAGES // n),
        in_specs=[pl.BlockSpec(memory_space=pltpu.SMEM), qspec]
        + [page(j) for j in range(n)] + [page(j) for j in range(n)] + [
            qspec,
            new(SELF_W + 2 * NSA_KV_W), new(SELF_W + 3 * NSA_KV_W),
            wspec, wspec,
            new(SELF_W + 4 * NSA_KV_W), new(SELF_W + 5 * NSA_KV_W),
            rspec, rspec,
            qspec, qspec,
            pl.BlockSpec((1, CROSS_W), lambda b, p, pt: (0, 0)),
        ],
        out_specs=[qspec, wspec, wspec],
        scratch_shapes=[pltpu.VMEM((rows, 1), F32), pltpu.VMEM((rows, 1), F32),
                        pltpu.VMEM((rows, HEAD_DIM), F32),
                        pltpu.VMEM((rows, HEAD_DIM), F32),
                        pltpu.VMEM((rows, page_w), F32),
                        pltpu.VMEM((NSA_KV_HEADS, ROWS, LANE), F32)],
    )
    o, win_k, win_v = pl.pallas_call(
        _nsa_sample_kernel,
        grid_spec=grid_spec,
        out_shape=[jax.ShapeDtypeStruct((nb, NSA_KV_HEADS, ROWS, HEAD_DIM), F32),
                   jax.ShapeDtypeStruct(buf_k.shape, F32), jax.ShapeDtypeStruct(buf_v.shape, F32)],
        compiler_params=_cparams(("arbitrary", "arbitrary")),
        name="nsa_sample",
    )(page_table, rel_bias, qg16, *([pool_k] * n), *([pool_v] * n), selx, ps3, ps3, buf_k, buf_v,
      ps3, ps3, wk_rows, wv_rows, o_c, gl16, b_gate)
    heads = (nb, WINDOW, NSA_KV_HEADS, HEAD_DIM)
    return o, win_k.reshape(heads), win_v.reshape(heads)


def _cross_sample_kernel(q_ref, mk_ref, mv_ref, o_ref):
    nh = N_CROSS_HEADS
    rows, width = nh * ROWS, MEM_LEN * nh
    own = (lax.broadcasted_iota(jnp.int32, (rows, width), 1) % nh
           == lax.broadcasted_iota(jnp.int32, (rows, width), 0) // ROWS)
    for i in range(q_ref.shape[0]):
        q = q_ref[i].reshape(rows, HEAD_DIM).astype(BF16)
        s = _dot_nt(q, mk_ref[i].astype(BF16)) * ATTN_SCALE
        p = _softmax_rows(s, own)
        o_ref[i] = _dot(p.astype(BF16), mv_ref[i].astype(BF16)).reshape(nh, ROWS, HEAD_DIM)


CROSS_SEQS_PER_STEP = 4


def cross_sample(qx16, mem_k, mem_v, layer):
    nb = qx16.shape[0]
    ns = CROSS_SEQS_PER_STEP
    n_layers = mem_k.shape[0]
    mem_k = mem_k.reshape(n_layers, nb, MEM_LEN * N_CROSS_HEADS, HEAD_DIM)
    mem_v = mem_v.reshape(n_layers, nb, MEM_LEN * N_CROSS_HEADS, HEAD_DIM)
    qspec = pl.BlockSpec((ns, N_CROSS_HEADS, ROWS, HEAD_DIM), lambda b: (b, 0, 0, 0))
    mspec = pl.BlockSpec((None, ns, MEM_LEN * N_CROSS_HEADS, HEAD_DIM), lambda b: (layer, b, 0, 0))
    return pl.pallas_call(
        _cross_sample_kernel,
        grid=(nb // ns,),
        in_specs=[qspec, mspec, mspec],
        out_specs=qspec,
        out_shape=jax.ShapeDtypeStruct((nb, N_CROSS_HEADS, ROWS, HEAD_DIM), F32),
        compiler_params=_cparams(("arbitrary",)),
        name="cross_sample",
    )(qx16, mem_k, mem_v)


def _arrange_in_weights(w, n_gate, per_group):
    gate = w[:, COL_QX:COL_QX + n_gate]
    if per_group:
        gate = jnp.pad(gate.reshape(-1, NSA_KV_HEADS, n_gate // NSA_KV_HEADS),
                       ((0, 0), (0, 0), (0, LANE - n_gate // NSA_KV_HEADS))).reshape(-1, CROSS_W)
    else:
        gate = jnp.pad(gate, ((0, 0), (0, CROSS_W - n_gate)))
    return jnp.concatenate([w[:, COL_QX + n_gate:], gate], axis=1)


def kernel(x_prompt, x_sample, mem_prompt, cache_nsa_cmp_k, cache_nsa_cmp_v, cache_nsa_slc_k, cache_nsa_slc_v, cache_nsa_win_k, cache_nsa_win_v, cache_fox_k, cache_fox_v, cache_fox_logf, cache_mem_k, cache_mem_v, page_table, rel_bias, norm_g, mem_norm_g, w_mem_kv, w_ff_gu, w_ff_down, w_in_nsa, b_gate_nsa, w_cmp1, w_cmp2, cmp_pe, w_out_nsa, w_in_fox, b_forget, w_out_fox, final_norm_g):
    n_p = BATCH * SEQ
    x = jnp.concatenate([x_prompt.reshape(n_p, D_MODEL), x_sample.reshape(-1, D_MODEL)], axis=0)
    mem = mem_prompt.reshape(BATCH * MEM_LEN, D_MODEL)
    n_pool = cache_nsa_cmp_k.shape[1]
    kv_heads = (BATCH, SEQ, NSA_KV_HEADS, HEAD_DIM)
    w_ff_gu = w_ff_gu.astype(BF16)
    w_ff_down = w_ff_down.astype(BF16)
    out = {}
    for i in range(DEPTH):
        x = ffn(x, norm_g[i, 0], w_ff_gu, w_ff_down, (i, 0))
        kvm = linear(mem, w_mem_kv[i], g=mem_norm_g[i], tm=512)
        out[f"mem_k{i}"] = kvm[:, :CROSS_W].reshape(BATCH, MEM_LEN, N_CROSS_HEADS, HEAD_DIM)
        out[f"mem_v{i}"] = kvm[:, CROSS_W:].reshape(BATCH, MEM_LEN, N_CROSS_HEADS, HEAD_DIM)
        a = i // 2
        if i % 2 == 0:
            w_tail = _arrange_in_weights(w_in_nsa[a], 3 * N_SELF_HEADS, True).astype(BF16)
            proj = linear(x, w_in_nsa[a].astype(BF16), w_tail, COL_QX // 512, g=norm_g[i, 1], tm=BIG_TOK_TILE)
            ps3 = proj[n_p:].reshape(DEC_BATCH, DEC_SEQ, PROJ_W)
            bg = jnp.pad(b_gate_nsa[a].reshape(NSA_KV_HEADS, 3 * NSA_GROUP),
                         ((0, 0), (0, LANE - 3 * NSA_GROUP))).reshape(1, CROSS_W)
            cmp_w = [(w_cmp1[a, c], w_cmp2[a, c], cmp_pe[a, c]) for c in range(2)]
            kvc = proj[:, SELF_W:SELF_W + 2 * NSA_KV_W]
            x_blk = kvc.reshape(N_TOK // CMP_BLOCK, CMP_BLOCK * 2 * NSA_KV_W)
            x_new = kvc.reshape(N_TOK // DEC_SEQ, DEC_SEQ * 2 * NSA_KV_W)
            pools = [cache_nsa_cmp_k[a], cache_nsa_cmp_v[a]]
            c_prompt, c_pool, c_new = [], [], []
            for c in range(2):
                col = lambda l, c=c: 2 * l + c
                c_prompt.append(compress(x_blk, col, CMP_BLOCK, *cmp_w[c], tr=BATCH * SEQ // CMP_BLOCK,
                                         rows=BATCH * SEQ // CMP_BLOCK))
                c_pool.append(compress_pool(pools[c], *cmp_w[c])
                              .reshape(n_pool, PAGE_SIZE // CMP_BLOCK, NSA_KV_W))
                c_new.append(compress(x_new, col, DEC_SEQ, *cmp_w[c], tr=DEC_BATCH, rows=DEC_BATCH,
                                      row_block0=n_p // DEC_SEQ // DEC_BATCH).reshape(DEC_BATCH, 1, NSA_KV_W))
            o_p, ox_p = nsa_prompt(proj, c_prompt[0], c_prompt[1], kvm, rel_bias, bg, BATCH)
            qg16 = _group_rows(ps3[:, :, :SELF_W])
            o_c, selx = nsa_sample_cmp(qg16, c_pool[0], c_pool[1], c_new[0], c_new[1], rel_bias, page_table)
            o_s, win_k, win_v = nsa_sample(
                qg16, ps3, cache_nsa_slc_k[a], cache_nsa_slc_v[a], selx,
                cache_nsa_win_k[a], cache_nsa_win_v[a],
                o_c, _gate_rows(ps3[:, :, COL_GATE:]), bg, rel_bias, page_table)
            o_s = _ungroup_rows(o_s)
            new_win = {"win_k": win_k, "win_v": win_v}
            for j, name in enumerate(("cmp_k", "cmp_v", "slc_k", "slc_v", "win_k", "win_v")):
                cols = slice(SELF_W + j * NSA_KV_W, SELF_W + (j + 1) * NSA_KV_W)
                st_p = proj[:n_p, cols].reshape(kv_heads)
                st_s = ps3[:, :, cols].reshape(DEC_BATCH, DEC_SEQ, NSA_KV_HEADS, HEAD_DIM)
                if name.startswith("win"):
                    st_p = st_p[:, -WINDOW:]
                    st_s = new_win[name]
                out.setdefault("p_" + name, []).append(st_p)
                out.setdefault("s_" + name, []).append(st_s)
            w_out = w_out_nsa[a]
        else:
            w_tail = _arrange_in_weights(w_in_fox[a], FOX_HEADS, False).astype(BF16)
            proj = linear(x, w_in_fox[a].astype(BF16), w_tail, COL_QX // 512, g=norm_g[i, 1], tm=BIG_TOK_TILE)
            ps3 = proj[n_p:].reshape(DEC_BATCH, DEC_SEQ, PROJ_W)
            lf_p, ct = fox_prep(proj, b_forget[a], BATCH)
            n_s = DEC_BATCH * DEC_SEQ
            lf_s, _ = fox_prep(proj, b_forget[a], 1, seq_len=n_s, row_block0=n_p // n_s)
            o_p = fox_prompt(proj, ct, BATCH)
            ox_p = cross_prompt(proj, kvm, BATCH)
            ct_pool = cumsum_lanes(jnp.transpose(cache_fox_logf[a], (2, 0, 1)))
            lfn = lf_s[:, :HEAD_PAD].reshape(DEC_BATCH, DEC_SEQ, HEAD_PAD).transpose(0, 2, 1)
            o_s = _unhead_rows(fox_sample(
                _head_rows(ps3[:, :, :SELF_W], FOX_HEADS), ps3,
                jnp.transpose(cache_fox_k[a], (0, 2, 1, 3)), jnp.transpose(cache_fox_v[a], (0, 2, 1, 3)),
                ct_pool, lfn, page_table))
            heads = (FOX_HEADS, HEAD_DIM)
            for j, name in enumerate(("fox_k", "fox_v")):
                cols = slice((j + 1) * SELF_W, (j + 2) * SELF_W)
                out.setdefault("p_" + name, []).append(proj[:n_p, cols].reshape(BATCH, SEQ, *heads))
                out.setdefault("s_" + name, []).append(ps3[:, :, cols].reshape(DEC_BATCH, DEC_SEQ, *heads))
            out.setdefault("p_fox_logf", []).append(lf_p[:, :FOX_HEADS].reshape(BATCH, SEQ, FOX_HEADS))
            out.setdefault("s_fox_logf", []).append(lf_s[:, :FOX_HEADS].reshape(DEC_BATCH, DEC_SEQ, FOX_HEADS))
            w_out = w_out_fox[a]
        ox_s = _unhead_rows(cross_sample(
            _head_rows(ps3[:, :, COL_QX:COL_GATE], N_CROSS_HEADS), cache_mem_k, cache_mem_v, i))
        mix_self = jnp.concatenate([o_p, o_s.astype(BF16)], axis=0)
        mix_cross = jnp.concatenate([ox_p, ox_s.astype(BF16)], axis=0)
        x = linear([mix_self, mix_cross], w_out.astype(BF16), res=x, tm=BIG_TOK_TILE)
        x = ffn(x, norm_g[i, 2], w_ff_gu, w_ff_down, (i, 1),
                final_g=final_norm_g if i == DEPTH - 1 else None)
    y_prompt = x[:n_p].reshape(BATCH, SEQ, D_MODEL)
    y_sample = x[n_p:].reshape(DEC_BATCH, DEC_SEQ, D_MODEL)
    st = lambda name: jnp.stack(out[name])
    p_mem_k = jnp.stack([out[f"mem_k{i}"] for i in range(DEPTH)])
    p_mem_v = jnp.stack([out[f"mem_v{i}"] for i in range(DEPTH)])
    return (y_prompt, y_sample,
            st("p_cmp_k"), st("p_cmp_v"), st("p_slc_k"), st("p_slc_v"), st("p_win_k"), st("p_win_v"),
            st("p_fox_k"), st("p_fox_v"), st("p_fox_logf"), p_mem_k, p_mem_v,
            st("s_cmp_k"), st("s_cmp_v"), st("s_slc_k"), st("s_slc_v"), st("s_win_k"), st("s_win_v"),
            st("s_fox_k"), st("s_fox_v"), st("s_fox_logf"))
```

```python
import functools
import math

import jax
import jax.numpy as jnp
from jax import lax
from jax.experimental import pallas as pl
from jax.experimental.pallas import tpu as pltpu

D_MODEL = 2048
BATCH = 4
SEQ = 2048
DEPTH = 2
DEC_BATCH = 128
DEC_SEQ = 4
PAST_LEN = 2048
PAGE_SIZE = 128
HEAD_DIM = 128
N_CROSS_HEADS = 4
N_SELF_HEADS = 12
NSA_KV_HEADS = 4
NSA_GROUP = N_SELF_HEADS // NSA_KV_HEADS
CMP_BLOCK = 64
SLC_BLOCK = 64
N_SEL = 16
WINDOW = 512
CMP_HIDDEN = 256
FOX_HEADS = N_SELF_HEADS
MEM_LEN = 256
N_BUCKETS = 32
MAX_DISTANCE = 128
FFN_DIM = ((8 * D_MODEL) // 3 + 127) // 128 * 128
RMS_EPS = 1e-6
FORCE_SCORE = 1e4
NEG_INF = -1e30
ATTN_SCALE = HEAD_DIM ** -0.5
SELF_W = N_SELF_HEADS * HEAD_DIM
NSA_KV_W = NSA_KV_HEADS * HEAD_DIM
CROSS_W = N_CROSS_HEADS * HEAD_DIM
N_PAGES = PAST_LEN // PAGE_SIZE
N_PAST_BLK = PAST_LEN // SLC_BLOCK

LANE = 128
VMEM_LIMIT = 56 * 1024 * 1024
N_TOK = BATCH * SEQ + DEC_BATCH * DEC_SEQ
TOK_TILE = 544
BIG_TOK_TILE = 2 * TOK_TILE
Q_TILE = 128
COL_QX = 3 * SELF_W
COL_GATE = COL_QX + CROSS_W
PROJ_W = COL_GATE + CROSS_W

F32 = jnp.float32
BF16 = jnp.bfloat16
_NT = (((1,), (1,)), ((), ()))


def _cparams(sem):
    return pltpu.CompilerParams(dimension_semantics=sem, vmem_limit_bytes=VMEM_LIMIT)


def _rms(x, g):
    return x * lax.rsqrt(jnp.mean(x * x, axis=-1, keepdims=True) + RMS_EPS) * g


def _dot(a, b):
    return jnp.dot(a, b, preferred_element_type=F32)


def _dot_nt(a, b):
    return lax.dot_general(a, b, _NT, preferred_element_type=F32)


def _t5_bias(dist, rb_ref, h):
    n = jnp.maximum(dist, 0)
    max_exact = N_BUCKETS // 2
    nf = jnp.maximum(n, 1).astype(F32)
    large = max_exact + (jnp.log(nf / max_exact) / math.log(MAX_DISTANCE / max_exact)
                         * (N_BUCKETS - max_exact)).astype(jnp.int32)
    bucket = jnp.where(n <= max_exact, n, jnp.minimum(large, N_BUCKETS - 1))
    out = jnp.zeros(dist.shape, F32)
    for b in range(N_BUCKETS):
        out = jnp.where(bucket == b, rb_ref[b, h], out)
    return out


def _softmax_rows(s, mask):
    s = jnp.where(mask, s, NEG_INF)
    e = jnp.where(mask, jnp.exp(s - jnp.max(s, axis=-1, keepdims=True)), 0.0)
    den = jnp.sum(e, axis=-1, keepdims=True)
    return e / jnp.where(den > 0.0, den, 1.0)


def _select_blocks(imp, cur, col, n_blocks):
    forced = (col == 0) | (col == cur) | (col == cur - 1)
    score = jnp.where(col <= cur, jnp.where(forced, FORCE_SCORE, imp), -1.0)
    rank = jnp.zeros(score.shape, jnp.int32)
    for i in range(n_blocks):
        ci = score[:, i:i + 1]
        beats = (ci > score) | ((ci == score) & (col > i))
        rank = rank + beats.astype(jnp.int32)
    return (rank < N_SEL) & (col <= cur)


class _Flash:
    def __init__(self, m_ref, l_ref, acc_ref):
        self.m, self.l, self.acc = m_ref, l_ref, acc_ref

    def reset(self):
        self.m[...] = jnp.full(self.m.shape, NEG_INF, F32)
        self.l[...] = jnp.zeros(self.l.shape, F32)
        self.acc[...] = jnp.zeros(self.acc.shape, F32)

    def update(self, s, v, mask=None):
        if mask is not None:
            s = jnp.where(mask, s, NEG_INF)
        m_old = self.m[...]
        m_new = jnp.maximum(m_old, jnp.max(s, axis=-1, keepdims=True))
        alpha = jnp.exp(m_old - m_new)
        p = jnp.exp(s - m_new)
        if mask is not None:
            p = jnp.where(mask, p, 0.0)
        self.l[...] = alpha * self.l[...] + jnp.sum(p, axis=-1, keepdims=True)
        self.acc[...] = alpha * self.acc[...] + _dot(p.astype(BF16), v)
        self.m[...] = m_new

    def result(self):
        return self.acc[...] / self.l[...]


FFN_TILES = 4


def _ffn_kernel(x_ref, g_ref, fg_ref, *refs, n_tiles, n_steps, final_norm):
    t = FFN_TILES
    wg, wu, wd = refs[:t], refs[t:2 * t], refs[2 * t:3 * t]
    o_ref, h_ref = refs[3 * t:]
    f = pl.program_id(1)

    @pl.when(f == 0)
    def _():
        x = x_ref[...]
        h_ref[...] = _rms(x, g_ref[...]).astype(BF16)
        o_ref[...] = x

    h = h_ref[...]
    a = _dot(h, jnp.concatenate([r[...].astype(BF16) for r in wg], axis=1))
    b = _dot(h, jnp.concatenate([r[...].astype(BF16) for r in wu], axis=1))
    act = a * jax.nn.sigmoid(a) * b
    tile = t * f + lax.broadcasted_iota(jnp.int32, act.shape, 1) // LANE
    act = jnp.where(tile < n_tiles, act, 0.0)
    w_down = jnp.concatenate([r[...].astype(BF16) for r in wd], axis=0)
    o_ref[...] += 0.5 * _dot(act.astype(BF16), w_down)

    if final_norm:
        @pl.when(f == n_steps - 1)
        def _():
            o_ref[...] = _rms(o_ref[...], fg_ref[...])


def ffn(x, g, w_gu, w_down, which, final_g=None):
    m, d = x.shape
    n_f = FFN_DIM // LANE
    n_steps = pl.cdiv(n_f, FFN_TILES)
    tm = TOK_TILE
    final_norm = final_g is not None
    fg = final_g if final_norm else g

    def col(off, k):
        return lambda i, f: (*which, 0, off + jnp.minimum(FFN_TILES * f + k, n_f - 1))

    def row(k):
        return lambda i, f: (*which, jnp.minimum(FFN_TILES * f + k, n_f - 1), 0)

    ks = range(FFN_TILES)
    return pl.pallas_call(
        functools.partial(_ffn_kernel, n_tiles=n_f, n_steps=n_steps, final_norm=final_norm),
        grid=(m // tm, n_steps),
        in_specs=[
            pl.BlockSpec((tm, d), lambda i, f: (i, 0)),
            pl.BlockSpec((1, d), lambda i, f: (0, 0)),
            pl.BlockSpec((1, d), lambda i, f: (0, 0)),
        ] + [pl.BlockSpec((None, None, d, LANE), col(0, k)) for k in ks]
        + [pl.BlockSpec((None, None, d, LANE), col(n_f, k)) for k in ks]
        + [pl.BlockSpec((None, None, LANE, d), row(k)) for k in ks],
        out_specs=pl.BlockSpec((tm, d), lambda i, f: (i, 0)),
        out_shape=jax.ShapeDtypeStruct((m, d), F32),
        scratch_shapes=[pltpu.VMEM((tm, d), BF16)],
        compiler_params=_cparams(("parallel", "arbitrary")),
        name="ffn",
    )(x, g.reshape(1, d), fg.reshape(1, d), *([w_gu] * (2 * FFN_TILES)), *([w_down] * FFN_TILES))


def _linear_kernel(*refs, n_x, norm, n_main, residual):
    x_refs = refs[:n_x]
    refs = refs[n_x:]
    if norm:
        g_ref, refs = refs[0], refs[1:]
    w_ref, wt_ref = refs[:2]
    refs = refs[2:]
    if residual:
        r_ref, o_ref, h_ref = refs
    else:
        o_ref, h_ref = refs
    j = pl.program_id(1)

    @pl.when(j == 0)
    def _():
        off = 0
        for x_ref in x_refs:
            x = x_ref[...].astype(F32)
            if norm:
                x = _rms(x, g_ref[...])
            h_ref[:, off:off + x.shape[1]] = x.astype(BF16)
            off += x.shape[1]

    def emit(w):
        y = _dot(h_ref[...], w[...].astype(BF16))
        if residual:
            y = y + r_ref[...]
        o_ref[...] = y

    pl.when(j < n_main)(lambda: emit(w_ref))
    pl.when(j >= n_main)(lambda: emit(wt_ref))


def linear(xs, w, w_tail=None, n_main=None, g=None, res=None, tm=TOK_TILE, tn=512):
    xs = list(xs) if isinstance(xs, (list, tuple)) else [xs]
    m = xs[0].shape[0]
    k = sum(x.shape[1] for x in xs)
    if w_tail is None:
        w_tail, n_main = w, w.shape[1] // tn
    n_blocks = n_main + (w_tail.shape[1] // tn if w_tail is not w else 0)
    norm, residual = g is not None, res is not None
    in_specs = [pl.BlockSpec((tm, x.shape[1]), lambda i, j: (i, 0)) for x in xs]
    args = list(xs)
    if norm:
        in_specs.append(pl.BlockSpec((1, k), lambda i, j: (0, 0)))
        args.append(g.reshape(1, k))
    in_specs.append(pl.BlockSpec((k, tn), lambda i, j: (0, jnp.minimum(j, n_main - 1))))
    in_specs.append(pl.BlockSpec((k, tn), lambda i, j: (0, jnp.maximum(j - n_main, 0))))
    args += [w, w_tail]
    if residual:
        in_specs.append(pl.BlockSpec((tm, tn), lambda i, j: (i, j)))
        args.append(res)
    return pl.pallas_call(
        functools.partial(_linear_kernel, n_x=len(xs), norm=norm, n_main=n_main, residual=residual),
        grid=(m // tm, n_blocks),
        in_specs=in_specs,
        out_specs=pl.BlockSpec((tm, tn), lambda i, j: (i, j)),
        out_shape=jax.ShapeDtypeStruct((m, n_blocks * tn), F32),
        scratch_shapes=[pltpu.VMEM((tm, k), BF16)],
        compiler_params=_cparams(("parallel", "arbitrary")),
        name="linear",
    )(*args)


def _compress_kernel(x_ref, pe_ref, w1_ref, w2_ref, o_ref, acc_ref, *, n_valid):
    l = pl.program_id(1)

    @pl.when(l == 0)
    def _():
        acc_ref[...] = jnp.zeros(acc_ref.shape, F32)

    pe = pe_ref[pl.ds(l, 1), :]
    w1 = w1_ref[...].astype(BF16)
    for g in range(NSA_KV_HEADS):
        xg = x_ref[:, g * HEAD_DIM:(g + 1) * HEAD_DIM]
        if n_valid < CMP_BLOCK:
            xg = jnp.where(l < n_valid, xg, 0.0)
        acc_ref[g] += _dot((xg + pe).astype(BF16), w1)

    @pl.when(l == CMP_BLOCK - 1)
    def _():
        w2 = w2_ref[...].astype(BF16)
        for g in range(NSA_KV_HEADS):
            a = acc_ref[g]
            hid = a * jax.nn.sigmoid(a)
            o_ref[:, g * HEAD_DIM:(g + 1) * HEAD_DIM] = _dot(hid.astype(BF16), w2)


def compress(x2d, col_block, n_valid, w1, w2, pe, tr, rows=None, row_block0=0):
    rows = x2d.shape[0] if rows is None else rows
    x_spec = pl.BlockSpec((tr, NSA_KV_W),
                          lambda i, l: (row_block0 + i, col_block(jnp.minimum(l, n_valid - 1))))
    return pl.pallas_call(
        functools.partial(_compress_kernel, n_valid=n_valid),
        grid=(rows // tr, CMP_BLOCK),
        in_specs=[
            x_spec,
            pl.BlockSpec((CMP_BLOCK, HEAD_DIM), lambda i, l: (0, 0)),
            pl.BlockSpec((HEAD_DIM, CMP_HIDDEN), lambda i, l: (l, 0)),
            pl.BlockSpec((CMP_HIDDEN, HEAD_DIM), lambda i, l: (0, 0)),
        ],
        out_specs=pl.BlockSpec((tr, NSA_KV_W), lambda i, l: (i, 0)),
        out_shape=jax.ShapeDtypeStruct((rows, NSA_KV_W), F32),
        scratch_shapes=[pltpu.VMEM((NSA_KV_HEADS, tr, CMP_HIDDEN), F32)],
        compiler_params=_cparams(("parallel", "arbitrary")),
        name="compress",
    )(x2d, pe, w1, w2)


POOL_OFFSETS = 4


def _compress_pool_kernel(x_ref, pe_ref, *refs):
    w1_refs = refs[:POOL_OFFSETS]
    w2_ref, o_ref, acc_ref = refs[POOL_OFFSETS:]
    lp = pl.program_id(1)
    tr = x_ref.shape[0]
    ng = NSA_KV_HEADS

    @pl.when(lp == 0)
    def _():
        acc_ref[...] = jnp.zeros(acc_ref.shape, F32)

    pe = pe_ref[pl.ds(pl.multiple_of(POOL_OFFSETS * lp, POOL_OFFSETS), POOL_OFFSETS), :]
    part = 0.0
    for k in range(POOL_OFFSETS):
        xk = (x_ref[:, k * ng:(k + 1) * ng, :] + pe[k:k + 1, :][None]).reshape(tr * ng, HEAD_DIM)
        part = part + _dot(xk.astype(BF16), w1_refs[k][...].astype(BF16))
    acc_ref[...] += part

    @pl.when(lp == CMP_BLOCK // POOL_OFFSETS - 1)
    def _():
        a = acc_ref[...]
        hid = a * jax.nn.sigmoid(a)
        o_ref[...] = _dot(hid.astype(BF16), w2_ref[...].astype(BF16))


def compress_pool(pool, w1, w2, pe, tr=512):
    n_blk = pool.shape[0] * PAGE_SIZE // CMP_BLOCK
    sub = POOL_OFFSETS * NSA_KV_HEADS
    x = pool.reshape(n_blk, CMP_BLOCK * NSA_KV_HEADS, HEAD_DIM)
    w1_spec = lambda k: pl.BlockSpec((HEAD_DIM, CMP_HIDDEN), lambda i, l: (POOL_OFFSETS * l + k, 0))
    out = pl.pallas_call(
        _compress_pool_kernel,
        grid=(n_blk // tr, CMP_BLOCK // POOL_OFFSETS),
        in_specs=[
            pl.BlockSpec((tr, sub, HEAD_DIM), lambda i, l: (i, l, 0)),
            pl.BlockSpec((CMP_BLOCK, HEAD_DIM), lambda i, l: (0, 0)),
        ] + [w1_spec(k) for k in range(POOL_OFFSETS)] + [
            pl.BlockSpec((CMP_HIDDEN, HEAD_DIM), lambda i, l: (0, 0)),
        ],
        out_specs=pl.BlockSpec((tr * NSA_KV_HEADS, HEAD_DIM), lambda i, l: (i, 0)),
        out_shape=jax.ShapeDtypeStruct((n_blk * NSA_KV_HEADS, HEAD_DIM), F32),
        scratch_shapes=[pltpu.VMEM((tr * NSA_KV_HEADS, CMP_HIDDEN), F32)],
        compiler_params=_cparams(("parallel", "arbitrary")),
        name="compress_pool",
    )(x, pe, *([w1] * POOL_OFFSETS), w2)
    return out.reshape(n_blk, NSA_KV_W)


def _nsa_prompt_kernel(rb_ref, q_ref, ks_ref, vs_ref, kw_ref, vw_ref, ck_ref, cv_ref, gl_ref, bg_ref,
                       qx_ref, mk_ref, mv_ref, o_ref, ox_ref, bt_ref, cb_ref, selx_ref, m_ref, l_ref, acc_ref):
    g = pl.program_id(1)
    qt = pl.program_id(2)
    q0 = qt * Q_TILE
    ii = lax.broadcasted_iota(jnp.int32, (Q_TILE, LANE), 0)
    jj = lax.broadcasted_iota(jnp.int32, (Q_TILE, LANE), 1)

    @pl.when(qt == 0)
    def _():
        d_new = (lax.broadcasted_iota(jnp.int32, (Q_TILE, 1), 0) + 1) % CMP_BLOCK
        for r in range(NSA_GROUP):
            h = NSA_GROUP * g + r
            bt_ref[r, 0] = _t5_bias(ii - jj, rb_ref, h)
            bt_ref[r, 1] = _t5_bias(LANE + ii - jj, rb_ref, h)
            cb_ref[r, 0] = _t5_bias(d_new, rb_ref, h)
            cb_ref[r, 1] = _t5_bias(d_new + CMP_BLOCK, rb_ref, h)

    def heads(f):
        return jnp.concatenate([f(r) for r in range(NSA_GROUP)], axis=0)

    qs = heads(lambda r: q_ref[:, r * HEAD_DIM:(r + 1) * HEAD_DIM]).astype(BF16)
    far_bias = heads(lambda r: jnp.full((Q_TILE, 1), rb_ref[N_BUCKETS - 1, NSA_GROUP * g + r], F32))
    diag_bias = heads(lambda r: bt_ref[r, 0])
    near_bias = heads(lambda r: bt_ref[r, 1])
    causal = jnp.concatenate([jj <= ii] * NSA_GROUP, axis=0)

    nb = ck_ref.shape[0]
    row = lax.broadcasted_iota(jnp.int32, (Q_TILE, nb), 0)
    col = lax.broadcasted_iota(jnp.int32, (Q_TILE, nb), 1)
    qpos = q0 + row
    dist = qpos - (col * CMP_BLOCK + (CMP_BLOCK - 1))
    sc = _dot_nt(qs, ck_ref[...].astype(BF16)) * ATTN_SCALE
    newest = (qpos - (CMP_BLOCK - 1)) // CMP_BLOCK
    probs = []
    for r in range(NSA_GROUP):
        bias = jnp.where(col == newest, cb_ref[r, 0],
                         jnp.where(col == newest - 1, cb_ref[r, 1], rb_ref[N_BUCKETS - 1, NSA_GROUP * g + r]))
        probs.append(_softmax_rows(sc[r * Q_TILE:(r + 1) * Q_TILE] + bias, dist >= 0))
    o_c = _dot(jnp.concatenate(probs, axis=0).astype(BF16), cv_ref[...].astype(BF16))
    imp = probs[0] + probs[1] + probs[2]
    sel = jnp.where(_select_blocks(imp, qpos // SLC_BLOCK, col, nb), 1.0, 0.0).astype(BF16)
    blk_minus_off = (lax.broadcasted_iota(jnp.int32, (nb, LANE), 0)
                     - lax.broadcasted_iota(jnp.int32, (nb, LANE), 1) // SLC_BLOCK)
    for t in range(selx_ref.shape[0]):
        to_keys = jnp.where(blk_minus_off == t * (LANE // SLC_BLOCK), 1.0, 0.0).astype(BF16)
        selx_ref[t] = _dot(sel, to_keys)

    every = jnp.concatenate([jj >= 0] * NSA_GROUP, axis=0)

    def key_tile(ref, t):
        return ref[pl.ds(pl.multiple_of(jnp.maximum(t, 0) * LANE, LANE), LANE), :].astype(BF16)

    def scores(k_ref, t, bias):
        return _dot_nt(qs, key_tile(k_ref, t)) * ATTN_SCALE + bias

    def sel_mask(t):
        return jnp.concatenate([selx_ref[jnp.maximum(t, 0)] > 0.5] * NSA_GROUP, axis=0)

    def pair_update(s0, s1, m0, m1, v_ref, t0, t1):
        def pv(p):
            return _dot(p[:, :LANE], key_tile(v_ref, t0)) + _dot(p[:, LANE:], key_tile(v_ref, t1))
        _flash_update(m_ref, l_ref, acc_ref, jnp.concatenate([s0, s1], axis=1), pv,
                      jnp.concatenate([m0, m1], axis=1), mxu_sum=True)

    m_ref[...] = jnp.full(m_ref.shape, NEG_INF, F32)
    l_ref[...] = jnp.zeros(l_ref.shape, F32)
    acc_ref[...] = jnp.zeros(acc_ref.shape, F32)
    n_far = jnp.maximum(qt - 1, 0)

    def far(it, carry):
        t0 = 2 * it
        t1 = jnp.minimum(t0 + 1, n_far - 1)
        pair_update(scores(ks_ref, t0, far_bias), scores(ks_ref, t1, far_bias),
                    sel_mask(t0), sel_mask(t1) & (t0 + 1 < n_far), vs_ref, t0, t1)
        return carry

    lax.fori_loop(0, (n_far + 1) // 2, far, 0)
    pair_update(scores(ks_ref, qt - 1, near_bias), scores(ks_ref, qt, diag_bias),
                sel_mask(qt - 1) & (qt >= 1), sel_mask(qt) & causal, vs_ref, qt - 1, qt)
    o_s = acc_ref[...] / l_ref[...]

    n_win = WINDOW // LANE
    upper = jnp.concatenate([jj > ii] * NSA_GROUP, axis=0)
    band_s, band_m = [], []
    for back in range(n_win, -1, -1):
        bias = diag_bias if back == 0 else near_bias if back == 1 else far_bias
        base = causal if back == 0 else upper if back == n_win else every
        band_s.append(scores(kw_ref, qt - back, bias))
        band_m.append(base & (qt >= back))
    pw = _softmax_rows(jnp.concatenate(band_s, axis=1), jnp.concatenate(band_m, axis=1)).astype(BF16)
    o_w = sum(_dot(pw[:, i * LANE:(i + 1) * LANE], key_tile(vw_ref, qt - back))
              for i, back in enumerate(range(n_win, -1, -1)))

    gate = jax.nn.sigmoid(gl_ref[...] + bg_ref[...])
    for r in range(NSA_GROUP):
        rows = slice(r * Q_TILE, (r + 1) * Q_TILE)
        o = (gate[:, 3 * r:3 * r + 1] * o_c[rows] + gate[:, 3 * r + 1:3 * r + 2] * o_s[rows]
             + gate[:, 3 * r + 2:3 * r + 3] * o_w[rows])
        o_ref[:, r * HEAD_DIM:(r + 1) * HEAD_DIM] = o.astype(o_ref.dtype)

    sx = _dot_nt(qx_ref[...].astype(BF16), mk_ref[...].astype(BF16)) * ATTN_SCALE
    px = _softmax_rows(sx, sx == sx)
    ox_ref[...] = _dot(px.astype(BF16), mv_ref[...].astype(BF16)).astype(ox_ref.dtype)


def nsa_prompt(proj, ck, cv, kvm, rel_bias, b_gate, batch):
    nq = SEQ // Q_TILE
    nb = SEQ // CMP_BLOCK
    hb = lambda c: c // HEAD_DIM
    qrow = lambda b, g, t: b * nq + t
    grid_spec = pltpu.PrefetchScalarGridSpec(
        num_scalar_prefetch=0,
        grid=(batch, NSA_KV_HEADS, nq),
        in_specs=[
            pl.BlockSpec(memory_space=pltpu.SMEM),
            pl.BlockSpec((Q_TILE, NSA_GROUP * HEAD_DIM), lambda b, g, t: (qrow(b, g, t), g)),
            pl.BlockSpec((SEQ, HEAD_DIM), lambda b, g, t: (b, hb(SELF_W + 2 * NSA_KV_W) + g)),
            pl.BlockSpec((SEQ, HEAD_DIM), lambda b, g, t: (b, hb(SELF_W + 3 * NSA_KV_W) + g)),
            pl.BlockSpec((SEQ, HEAD_DIM), lambda b, g, t: (b, hb(SELF_W + 4 * NSA_KV_W) + g)),
            pl.BlockSpec((SEQ, HEAD_DIM), lambda b, g, t: (b, hb(SELF_W + 5 * NSA_KV_W) + g)),
            pl.BlockSpec((nb, HEAD_DIM), lambda b, g, t: (b, g)),
            pl.BlockSpec((nb, HEAD_DIM), lambda b, g, t: (b, g)),
            pl.BlockSpec((Q_TILE, LANE), lambda b, g, t: (qrow(b, g, t), hb(COL_GATE) + g)),
            pl.BlockSpec((1, LANE), lambda b, g, t: (0, g)),
            pl.BlockSpec((Q_TILE, HEAD_DIM), lambda b, g, t: (qrow(b, g, t), hb(COL_QX) + g)),
            pl.BlockSpec((MEM_LEN, HEAD_DIM), lambda b, g, t: (b, g)),
            pl.BlockSpec((MEM_LEN, HEAD_DIM), lambda b, g, t: (b, N_CROSS_HEADS + g)),
        ],
        out_specs=[
            pl.BlockSpec((Q_TILE, NSA_GROUP * HEAD_DIM), lambda b, g, t: (qrow(b, g, t), g)),
            pl.BlockSpec((Q_TILE, HEAD_DIM), lambda b, g, t: (qrow(b, g, t), g)),
        ],
        scratch_shapes=[
            pltpu.VMEM((NSA_GROUP, 2, Q_TILE, LANE), F32),
            pltpu.VMEM((NSA_GROUP, 2, Q_TILE, 1), F32),
            pltpu.VMEM((SEQ // LANE, Q_TILE, LANE), F32),
            pltpu.VMEM((NSA_GROUP * Q_TILE, 1), F32),
            pltpu.VMEM((NSA_GROUP * Q_TILE, 1), F32),
            pltpu.VMEM((NSA_GROUP * Q_TILE, HEAD_DIM), F32),
        ],
    )
    return pl.pallas_call(
        _nsa_prompt_kernel,
        grid_spec=grid_spec,
        out_shape=[jax.ShapeDtypeStruct((batch * SEQ, SELF_W), BF16),
                   jax.ShapeDtypeStruct((batch * SEQ, CROSS_W), BF16)],
        compiler_params=_cparams(("arbitrary", "arbitrary", "arbitrary")),
        name="nsa_prompt",
    )(rel_bias, proj, proj, proj, proj, proj, ck, cv, proj, b_gate, proj, kvm, kvm)


FOX_TILE = 256
HEAD_PAD = 16


def _log_sigmoid(x):
    return jnp.minimum(x, 0.0) - jnp.log1p(jnp.exp(-jnp.abs(x)))


def _fox_prep_kernel(fl_ref, bf_ref, lf_ref, ct_ref):
    n_chunks = fl_ref.shape[0] // LANE
    ii = lax.broadcasted_iota(jnp.int32, (LANE, LANE), 0)
    jj = lax.broadcasted_iota(jnp.int32, (LANE, LANE), 1)
    tri = jnp.where(jj <= ii, 1.0, 0.0).astype(F32)
    carry = jnp.zeros((1, LANE), F32)
    per_tile = FOX_TILE // LANE
    for c in range(n_chunks):
        lf = _log_sigmoid(fl_ref[c * LANE:(c + 1) * LANE, :] + bf_ref[...])
        lf_ref[c * LANE:(c + 1) * LANE, :] = lf
        cs = jnp.dot(tri, lf, precision=lax.Precision.HIGHEST, preferred_element_type=F32) + carry
        carry = cs[LANE - 1:LANE, :]
        ct_ref[:, c // per_tile, (c % per_tile) * LANE:(c % per_tile + 1) * LANE] = cs.T[:HEAD_PAD]


def fox_prep(proj, b_forget, batch, seq_len=SEQ, row_block0=0):
    bf = jnp.pad(b_forget.reshape(1, FOX_HEADS), ((0, 0), (0, LANE - FOX_HEADS)))
    n_tiles = max(seq_len // FOX_TILE, 1)
    return pl.pallas_call(
        _fox_prep_kernel,
        grid=(batch,),
        in_specs=[pl.BlockSpec((seq_len, LANE), lambda b: (row_block0 + b, COL_GATE // LANE)),
                  pl.BlockSpec((1, LANE), lambda b: (0, 0))],
        out_specs=[pl.BlockSpec((seq_len, LANE), lambda b: (b, 0)),
                   pl.BlockSpec((None, HEAD_PAD, n_tiles, FOX_TILE), lambda b: (b, 0, 0, 0))],
        out_shape=[jax.ShapeDtypeStruct((batch * seq_len, LANE), F32),
                   jax.ShapeDtypeStruct((batch, HEAD_PAD, n_tiles, FOX_TILE), F32)],
        compiler_params=_cparams(("arbitrary",)),
        name="fox_prep",
    )(proj, bf)


FOX_HEAD_BLOCK = 4


def _fox_prompt_kernel(q_ref, k_ref, v_ref, c_ref, o_ref, m_ref, l_ref, acc_ref):
    qt = pl.program_id(2)
    nh = FOX_HEAD_BLOCK
    hs = [slice(h * HEAD_DIM, (h + 1) * HEAD_DIM) for h in range(nh)]
    q = [q_ref[:, s].astype(BF16) for s in hs]
    m_ref[...] = jnp.full(m_ref.shape, NEG_INF, F32)
    l_ref[...] = jnp.zeros(l_ref.shape, F32)
    acc_ref[...] = jnp.zeros(acc_ref.shape, F32)

    def tile(t, mask):
        rows = pl.ds(pl.multiple_of(t * FOX_TILE, FOX_TILE), FOX_TILE)
        s = jnp.concatenate([_dot_nt(q[h], k_ref[rows, hs[h]].astype(BF16)) * ATTN_SCALE
                             - c_ref[h, pl.ds(t, 1), :] for h in range(nh)], axis=0)

        def pv(p):
            return jnp.concatenate([_dot(p[h * FOX_TILE:(h + 1) * FOX_TILE], v_ref[rows, hs[h]].astype(BF16))
                                    for h in range(nh)], axis=0)

        _flash_update(m_ref, l_ref, acc_ref, s, pv, mask, mxu_sum=True)

    def far(t, carry):
        tile(t, None)
        return carry

    lax.fori_loop(0, qt, far, 0)
    ii = lax.broadcasted_iota(jnp.int32, (nh * FOX_TILE, FOX_TILE), 0) % FOX_TILE
    jj = lax.broadcasted_iota(jnp.int32, (nh * FOX_TILE, FOX_TILE), 1)
    tile(qt, jj <= ii)
    o = acc_ref[...] / l_ref[...]
    for h in range(nh):
        o_ref[:, hs[h]] = o[h * FOX_TILE:(h + 1) * FOX_TILE].astype(o_ref.dtype)


def fox_prompt(proj, ct, batch):
    nq = SEQ // FOX_TILE
    nh = FOX_HEAD_BLOCK
    wb = nh * HEAD_DIM
    cb = lambda c: c // wb
    rows = nh * FOX_TILE
    return pl.pallas_call(
        _fox_prompt_kernel,
        grid=(batch, FOX_HEADS // nh, nq),
        in_specs=[
            pl.BlockSpec((FOX_TILE, wb), lambda b, h, t: (b * nq + t, h)),
            pl.BlockSpec((SEQ, wb), lambda b, h, t: (b, cb(SELF_W) + h)),
            pl.BlockSpec((SEQ, wb), lambda b, h, t: (b, cb(2 * SELF_W) + h)),
            pl.BlockSpec((None, nh, nq, FOX_TILE), lambda b, h, t: (b, h, 0, 0)),
        ],
        out_specs=pl.BlockSpec((FOX_TILE, wb), lambda b, h, t: (b * nq + t, h)),
        out_shape=jax.ShapeDtypeStruct((batch * SEQ, SELF_W), BF16),
        scratch_shapes=[pltpu.VMEM((rows, 1), F32), pltpu.VMEM((rows, 1), F32),
                        pltpu.VMEM((rows, HEAD_DIM), F32)],
        compiler_params=_cparams(("arbitrary", "arbitrary", "arbitrary")),
        name="fox_prompt",
    )(proj, proj, proj, ct)


def _cross_prompt_kernel(q_ref, mk_ref, mv_ref, o_ref):
    for h in range(N_CROSS_HEADS):
        hs = slice(h * HEAD_DIM, (h + 1) * HEAD_DIM)
        s = _dot_nt(q_ref[:, hs].astype(BF16), mk_ref[:, hs].astype(BF16)) * ATTN_SCALE
        p = _softmax_rows(s, s == s)
        o_ref[:, hs] = _dot(p.astype(BF16), mv_ref[:, hs].astype(BF16)).astype(o_ref.dtype)


def cross_prompt(proj, kvm, batch, tq=512):
    nq = SEQ // tq
    return pl.pallas_call(
        _cross_prompt_kernel,
        grid=(batch, nq),
        in_specs=[pl.BlockSpec((tq, CROSS_W), lambda b, t: (b * nq + t, COL_QX // CROSS_W)),
                  pl.BlockSpec((MEM_LEN, CROSS_W), lambda b, t: (b, 0)),
                  pl.BlockSpec((MEM_LEN, CROSS_W), lambda b, t: (b, 1))],
        out_specs=pl.BlockSpec((tq, CROSS_W), lambda b, t: (b * nq + t, 0)),
        out_shape=jax.ShapeDtypeStruct((batch * SEQ, CROSS_W), BF16),
        compiler_params=_cparams(("arbitrary", "arbitrary")),
        name="cross_prompt",
    )(proj, kvm, kvm)


ROWS = 16


def _pad_rows(x):
    return jnp.pad(x, [(0, 0)] * (x.ndim - 2) + [(0, ROWS - x.shape[-2]), (0, 0)])


def _group_rows(q):
    b = q.shape[0]
    q = q.reshape(b, DEC_SEQ, NSA_KV_HEADS, NSA_GROUP, HEAD_DIM).transpose(0, 2, 3, 1, 4)
    return _pad_rows(q.reshape(b, NSA_KV_HEADS, NSA_GROUP * DEC_SEQ, HEAD_DIM))


def _ungroup_rows(o):
    b = o.shape[0]
    o = o[:, :, :NSA_GROUP * DEC_SEQ].reshape(b, NSA_KV_HEADS, NSA_GROUP, DEC_SEQ, HEAD_DIM)
    return o.transpose(0, 3, 1, 2, 4).reshape(b * DEC_SEQ, SELF_W)


def _gate_rows(gl):
    b = gl.shape[0]
    gl = gl.reshape(b, DEC_SEQ, NSA_KV_HEADS, LANE).transpose(0, 2, 1, 3)
    return _pad_rows(jnp.tile(gl, (1, 1, NSA_GROUP, 1)))


def _head_rows(q, n_heads):
    b = q.shape[0]
    return _pad_rows(q.reshape(b, DEC_SEQ, n_heads, HEAD_DIM).transpose(0, 2, 1, 3))


def _unhead_rows(o):
    b, h = o.shape[:2]
    return o[:, :, :DEC_SEQ].transpose(0, 2, 1, 3).reshape(b * DEC_SEQ, h * HEAD_DIM)


def _row_ids():
    row = lax.broadcasted_iota(jnp.int32, (ROWS, LANE), 0)
    return row, row % DEC_SEQ, row // DEC_SEQ


def _per_head(r_idx, f):
    return jnp.where(r_idx == 0, f(0), jnp.where(r_idx == 1, f(1), f(2)))


def _safe_div(acc, l):
    return acc / jnp.where(l > 0.0, l, 1.0)


def _merge_new_keys(fl, q32, kn, vn, bias, masks):
    s_new = [jnp.where(masks[j], jnp.sum(q32 * kn[j:j + 1, :], axis=-1, keepdims=True) * ATTN_SCALE
                       + bias[j], NEG_INF) for j in range(DEC_SEQ)]
    m_old = fl.m[...]
    m_new = m_old
    for s in s_new:
        m_new = jnp.maximum(m_new, s)
    alpha = jnp.exp(m_old - m_new)
    l = alpha * fl.l[...]
    acc = alpha * fl.acc[...]
    for j in range(DEC_SEQ):
        p = jnp.where(masks[j], jnp.exp(s_new[j] - m_new), 0.0)
        l = l + p
        acc = acc + p * vn[j:j + 1, :]
    return _safe_div(acc, l)


PAGES_PER_STEP = 8
NSA_PAGES_PER_STEP = 16
SUB = 8


def _flash_update(m_ref, l_ref, acc_ref, s, pv_fn, mask=None, mxu_sum=False):
    if mask is not None:
        s = jnp.where(mask, s, NEG_INF)
    m_old = m_ref[...]
    m_new = jnp.maximum(m_old, jnp.max(s, axis=-1, keepdims=True))
    alpha = jnp.exp(m_old - m_new)
    p = jnp.exp(s - m_new)
    if mask is not None:
        p = jnp.where(mask, p, 0.0)
    if mxu_sum:
        p = p.astype(BF16)
        row_sum = _dot(p, jnp.ones((p.shape[1], LANE), BF16))[:, 0:1]
    else:
        row_sum = jnp.sum(p, axis=-1, keepdims=True)
        p = p.astype(BF16)
    l_ref[...] = alpha * l_ref[...] + row_sum
    acc_ref[...] = alpha * acc_ref[...] + pv_fn(p)
    m_ref[...] = m_new


def _cumsum_lanes_kernel(x_ref, o_ref):
    ii = lax.broadcasted_iota(jnp.int32, (LANE, LANE), 0)
    jj = lax.broadcasted_iota(jnp.int32, (LANE, LANE), 1)
    upper = jnp.where(ii <= jj, 1.0, 0.0).astype(F32)
    o_ref[...] = jnp.dot(x_ref[...], upper, precision=lax.Precision.HIGHEST, preferred_element_type=F32)


def cumsum_lanes(x, tr=512):
    h, rows, _ = x.shape
    spec = pl.BlockSpec((None, tr, LANE), lambda i, j: (i, j, 0))
    return pl.pallas_call(
        _cumsum_lanes_kernel, grid=(h, rows // tr), in_specs=[spec], out_specs=spec,
        out_shape=jax.ShapeDtypeStruct(x.shape, F32),
        compiler_params=_cparams(("arbitrary", "arbitrary")), name="cumsum_lanes",
    )(x)


def _fox_sample_kernel(pt_ref, q_ref, kn_ref, vn_ref, *refs):
    n = PAGES_PER_STEP
    k_refs, v_refs, c_refs = refs[:n], refs[n:2 * n], refs[2 * n:3 * n]
    lfn_ref, o_ref, m_ref, l_ref, acc_ref, run_ref = refs[3 * n:]
    b = pl.program_id(0)
    step = pl.program_id(1)

    @pl.when(step == 0)
    def _():
        m_ref[...] = jnp.full(m_ref.shape, NEG_INF, F32)
        l_ref[...] = jnp.zeros(l_ref.shape, F32)
        acc_ref[...] = jnp.zeros(acc_ref.shape, F32)
        run_ref[...] = jnp.zeros(run_ref.shape, F32)

    sub = [pt_ref[b, step * n + j] % SUB for j in range(n)]
    s_heads = []
    for h in range(FOX_HEADS):
        q = q_ref[h].astype(BF16)
        run = run_ref[h:h + 1, :]
        parts = []
        for j in range(n):
            ck = run + c_refs[j][h, pl.ds(sub[j], 1), :]
            parts.append(_dot_nt(q, k_refs[j][h].astype(BF16)) * ATTN_SCALE - ck)
            run = jnp.broadcast_to(ck[:, PAGE_SIZE - 1:PAGE_SIZE], (1, LANE))
        run_ref[h:h + 1, :] = run
        s_heads.append(jnp.concatenate(parts, axis=1))
    s = jnp.concatenate(s_heads, axis=0)

    def pv(p):
        return jnp.concatenate(
            [sum(_dot(p[h * ROWS:(h + 1) * ROWS, j * PAGE_SIZE:(j + 1) * PAGE_SIZE], v_refs[j][h].astype(BF16))
                 for j in range(n)) for h in range(FOX_HEADS)], axis=0)

    _flash_update(m_ref, l_ref, acc_ref, s, pv)

    @pl.when(step == N_PAGES // n - 1)
    def _():
        row = lax.broadcasted_iota(jnp.int32, (ROWS, 1), 0)
        lfn = lfn_ref[...]
        for h in range(FOX_HEADS):
            hs = slice(h * HEAD_DIM, (h + 1) * HEAD_DIM)
            rs = slice(h * ROWS, (h + 1) * ROWS)
            c, c_new = run_ref[h:h + 1, 0:1], []
            for j in range(DEC_SEQ):
                c = c + lfn[h:h + 1, j:j + 1]
                c_new.append(-c)
            fl = _Flash(m_ref.at[rs], l_ref.at[rs], acc_ref.at[rs])
            o_ref[h] = _merge_new_keys(fl, q_ref[h], kn_ref[:, hs], vn_ref[:, hs], c_new,
                                       [row >= j for j in range(DEC_SEQ)])


def fox_sample(q16, ps3, pool_k, pool_v, ct_pool, lfn, page_table):
    nb = q16.shape[0]
    n = PAGES_PER_STEP
    page = lambda j: pl.BlockSpec((None, FOX_HEADS, PAGE_SIZE, HEAD_DIM),
                                  lambda b, s, pt: (pt[b, s * n + j], 0, 0, 0))
    cpage = lambda j: pl.BlockSpec((FOX_HEADS, SUB, PAGE_SIZE), lambda b, s, pt: (0, pt[b, s * n + j] // SUB, 0))
    rows = FOX_HEADS * ROWS
    grid_spec = pltpu.PrefetchScalarGridSpec(
        num_scalar_prefetch=1,
        grid=(nb, N_PAGES // n),
        in_specs=[
            pl.BlockSpec((None, FOX_HEADS, ROWS, HEAD_DIM), lambda b, s, pt: (b, 0, 0, 0)),
            pl.BlockSpec((None, DEC_SEQ, SELF_W), lambda b, s, pt: (b, 0, 1)),
            pl.BlockSpec((None, DEC_SEQ, SELF_W), lambda b, s, pt: (b, 0, 2)),
        ] + [page(j) for j in range(n)] + [page(j) for j in range(n)] + [cpage(j) for j in range(n)] + [
            pl.BlockSpec((None, HEAD_PAD, DEC_SEQ), lambda b, s, pt: (b, 0, 0)),
        ],
        out_specs=pl.BlockSpec((None, FOX_HEADS, ROWS, HEAD_DIM), lambda b, s, pt: (b, 0, 0, 0)),
        scratch_shapes=[pltpu.VMEM((rows, 1), F32), pltpu.VMEM((rows, 1), F32),
                        pltpu.VMEM((rows, HEAD_DIM), F32), pltpu.VMEM((HEAD_PAD, LANE), F32)],
    )
    return pl.pallas_call(
        _fox_sample_kernel,
        grid_spec=grid_spec,
        out_shape=jax.ShapeDtypeStruct((nb, FOX_HEADS, ROWS, HEAD_DIM), F32),
        compiler_params=_cparams(("arbitrary", "arbitrary")),
        name="fox_sample",
    )(page_table, q16, ps3, ps3, *([pool_k] * n), *([pool_v] * n), *([ct_pool] * n), lfn)


CMP_SEQS_PER_STEP = 2


def _nsa_sample_cmp_kernel(pt_ref, rb_ref, q_ref, *refs):
    ns = CMP_SEQS_PER_STEP
    n_in = 2 * N_PAGES
    pages = [refs[i * n_in:(i + 1) * n_in] for i in range(ns)]
    ckn_ref, cvn_ref, oc_ref, sx_ref, ck_s, cv_s, bias_ref = refs[ns * n_in:]
    b = pl.program_id(0)
    blk_per_page = PAGE_SIZE // CMP_BLOCK
    row, tok, r_idx = _row_ids()
    lane = lax.broadcasted_iota(jnp.int32, (ROWS, LANE), 1)
    qpos = PAST_LEN + tok
    dist = qpos - (lane * CMP_BLOCK + (CMP_BLOCK - 1))
    valid = (dist >= 0) & (lane <= N_PAST_BLK)

    @pl.when(b == 0)
    def _():
        ck_s[...] = jnp.zeros(ck_s.shape, F32)
        cv_s[...] = jnp.zeros(cv_s.shape, F32)
        for g in range(NSA_KV_HEADS):
            bias_ref[g] = _per_head(r_idx, lambda r: _t5_bias(dist, rb_ref, NSA_GROUP * g + r))

    for i in range(ns):
        for p in range(N_PAGES):
            ck_s[i, p * blk_per_page:(p + 1) * blk_per_page, :] = pages[i][p][...]
            cv_s[i, p * blk_per_page:(p + 1) * blk_per_page, :] = pages[i][N_PAGES + p][...]
        ck_s[i, N_PAST_BLK:N_PAST_BLK + 1, :] = ckn_ref[i]
        cv_s[i, N_PAST_BLK:N_PAST_BLK + 1, :] = cvn_ref[i]
        for g in range(NSA_KV_HEADS):
            gs = slice(g * HEAD_DIM, (g + 1) * HEAD_DIM)
            s = _dot_nt(q_ref[i, g].astype(BF16), ck_s[i, :, gs].astype(BF16)) * ATTN_SCALE + bias_ref[g]
            p = _softmax_rows(s, valid)
            oc_ref[i, g] = _dot(p.astype(BF16), cv_s[i, :, gs].astype(BF16))
            p0 = jnp.where(row < NSA_GROUP * DEC_SEQ, p, 0.0)
            imp = p0 + pltpu.roll(p0, ROWS - DEC_SEQ, 0) + pltpu.roll(p0, ROWS - 2 * DEC_SEQ, 0)
            imp = jnp.where(row < DEC_SEQ, imp, 0.0)
            imp = imp + pltpu.roll(imp, DEC_SEQ, 0) + pltpu.roll(imp, 2 * DEC_SEQ, 0)
            sel = _select_blocks(imp, qpos // SLC_BLOCK, lane, N_PAST_BLK + 1)
            sx_ref[i, g] = jnp.where(sel, 1.0, 0.0)


def nsa_sample_cmp(qg16, ck_pool, cv_pool, ck_new, cv_new, rel_bias, page_table):
    nb = qg16.shape[0]
    ns = CMP_SEQS_PER_STEP
    page = lambda i, p: pl.BlockSpec((None, PAGE_SIZE // CMP_BLOCK, NSA_KV_W),
                                     lambda b, pt: (pt[b * ns + i, p], 0, 0))
    seq_pages = lambda i: [page(i, p) for p in range(N_PAGES)] * 2
    qspec = pl.BlockSpec((ns, NSA_KV_HEADS, ROWS, HEAD_DIM), lambda b, pt: (b, 0, 0, 0))
    grid_spec = pltpu.PrefetchScalarGridSpec(
        num_scalar_prefetch=1,
        grid=(nb // ns,),
        in_specs=[pl.BlockSpec(memory_space=pltpu.SMEM), qspec]
        + [spec for i in range(ns) for spec in seq_pages(i)]
        + [pl.BlockSpec((ns, 1, NSA_KV_W), lambda b, pt: (b, 0, 0))] * 2,
        out_specs=[qspec, pl.BlockSpec((ns, NSA_KV_HEADS, ROWS, LANE), lambda b, pt: (b, 0, 0, 0))],
        scratch_shapes=[pltpu.VMEM((ns, LANE, NSA_KV_W), F32), pltpu.VMEM((ns, LANE, NSA_KV_W), F32),
                        pltpu.VMEM((NSA_KV_HEADS, ROWS, LANE), F32)],
    )
    pools = [pool for _ in range(ns) for pool in [ck_pool] * N_PAGES + [cv_pool] * N_PAGES]
    return pl.pallas_call(
        _nsa_sample_cmp_kernel,
        grid_spec=grid_spec,
        out_shape=[jax.ShapeDtypeStruct((nb, NSA_KV_HEADS, ROWS, HEAD_DIM), F32),
                   jax.ShapeDtypeStruct((nb, NSA_KV_HEADS, ROWS, LANE), F32)],
        compiler_params=_cparams(("arbitrary",)),
        name="nsa_sample_cmp",
    )(page_table, rel_bias, qg16, *pools, ck_new, cv_new)


def _nsa_sample_kernel(pt_ref, rb_ref, q_ref, *refs):
    n = NSA_PAGES_PER_STEP
    n_steps = N_PAGES // n
    k_refs, v_refs = refs[:n], refs[n:2 * n]
    (sx_ref, kn_ref, vn_ref, wk_ref, wv_ref, wkn_ref, wvn_ref, wkr_ref, wvr_ref, oc_ref, gl_ref, bg_ref,
     o_ref, wko_ref, wvo_ref, m_ref, l_ref, acc_ref, ow_ref, b15_ref, bn_ref) = refs[2 * n:]
    b = pl.program_id(0)
    p = pl.program_id(1)
    ng = NSA_KV_HEADS
    rows = ng * ROWS
    page_w = PAGE_SIZE * ng
    row, tok, r_idx = _row_ids()
    lane = lax.broadcasted_iota(jnp.int32, (ROWS, LANE), 1)
    tok1, r1 = tok[:, 0:1], r_idx[:, 0:1]
    new_vis = [tok1 >= j for j in range(DEC_SEQ)]
    q_all = q_ref[...].reshape(rows, HEAD_DIM).astype(BF16)
    grp_of_row = lax.broadcasted_iota(jnp.int32, (rows, 1), 0) // ROWS
    tok_of_row = lax.broadcasted_iota(jnp.int32, (rows, 1), 0) % DEC_SEQ

    def stack(f):
        return jnp.concatenate([f(g) for g in range(ng)], axis=0)

    far_bias = stack(lambda g: _per_head(r1, lambda r: rb_ref[N_BUCKETS - 1, NSA_GROUP * g + r]))

    def own_group(width):
        return lax.broadcasted_iota(jnp.int32, (rows, width), 1) % ng == grp_of_row

    @pl.when((b == 0) & (p == 0))
    def _():
        c_tok = lax.broadcasted_iota(jnp.int32, (ROWS, page_w), 1) // ng
        t_row = lax.broadcasted_iota(jnp.int32, (ROWS, page_w), 0) % DEC_SEQ
        r_row = lax.broadcasted_iota(jnp.int32, (ROWS, page_w), 0) // DEC_SEQ
        for g in range(ng):
            b15_ref[g * ROWS:(g + 1) * ROWS, :] = _per_head(
                r_row, lambda r: _t5_bias(LANE + t_row - c_tok, rb_ref, NSA_GROUP * g + r))
            bn_ref[g] = _per_head(r_idx, lambda r: _t5_bias(tok - lane, rb_ref, NSA_GROUP * g + r))

    def new_scores(kn):
        return [stack(lambda g: jnp.sum(q_ref[g] * kn[j:j + 1, g * HEAD_DIM:(g + 1) * HEAD_DIM], axis=-1,
                                        keepdims=True) * ATTN_SCALE + bn_ref[g][:, j:j + 1])
                for j in range(DEC_SEQ)]

    def new_values(vn, j):
        return stack(lambda g: jnp.broadcast_to(vn[j:j + 1, g * HEAD_DIM:(g + 1) * HEAD_DIM], (ROWS, HEAD_DIM)))

    @pl.when(p == 0)
    def _():
        m_ref[...] = jnp.full(m_ref.shape, NEG_INF, F32)
        l_ref[...] = jnp.zeros(l_ref.shape, F32)
        acc_ref[...] = jnp.zeros(acc_ref.shape, F32)
        width = WINDOW * ng
        s = _dot_nt(q_all, wk_ref[...].astype(BF16)) * ATTN_SCALE
        bias = jnp.concatenate([jnp.broadcast_to(far_bias, (rows, width - page_w)), b15_ref[...]], axis=1)
        c_tok = lax.broadcasted_iota(jnp.int32, (rows, width), 1) // ng
        in_win = own_group(width) & (c_tok > tok_of_row)
        s = jnp.where(in_win, s + bias, NEG_INF)
        vis = [tok_of_row >= j for j in range(DEC_SEQ)]
        s_new = [jnp.where(vis[j], sn, NEG_INF) for j, sn in enumerate(new_scores(wkn_ref))]
        mx = jnp.max(s, axis=-1, keepdims=True)
        for sn in s_new:
            mx = jnp.maximum(mx, sn)
        e = jnp.where(in_win, jnp.exp(s - mx), 0.0)
        den = jnp.sum(e, axis=-1, keepdims=True)
        acc = _dot(e.astype(BF16), wv_ref[...].astype(BF16))
        for j in range(DEC_SEQ):
            pj = jnp.where(vis[j], jnp.exp(s_new[j] - mx), 0.0)
            den = den + pj
            acc = acc + pj * new_values(wvn_ref, j)
        ow_ref[...] = _safe_div(acc, den)
        keep = (WINDOW - DEC_SEQ) * ng
        wko_ref[0:keep, :] = wk_ref[DEC_SEQ * ng:, :]
        wvo_ref[0:keep, :] = wv_ref[DEC_SEQ * ng:, :]
        wko_ref[keep:, :] = wkr_ref[...]
        wvo_ref[keep:, :] = wvr_ref[...]

    blk_per_page = PAGE_SIZE // SLC_BLOCK
    blk = lax.broadcasted_iota(jnp.int32, (LANE, page_w), 0)
    off = lax.broadcasted_iota(jnp.int32, (LANE, page_w), 1) // (SLC_BLOCK * ng)
    sel = sx_ref[...].reshape(rows, LANE).astype(BF16)
    own = own_group(page_w)
    s_parts, m_parts = [], []
    for j in range(n):
        bias = far_bias
        if j == n - 1:
            bias = jnp.where(p == n_steps - 1, b15_ref[...], bias)
        s_parts.append(_dot_nt(q_all, k_refs[j][...].astype(BF16)) * ATTN_SCALE + bias)
        to_cols = jnp.where(blk - off == (p * n + j) * blk_per_page, 1.0, 0.0).astype(BF16)
        m_parts.append((_dot(sel, to_cols) > 0.5) & own)

    def pv(pr):
        return sum(_dot(pr[:, j * page_w:(j + 1) * page_w], v_refs[j][...].astype(BF16)) for j in range(n))

    _flash_update(m_ref, l_ref, acc_ref, jnp.concatenate(s_parts, axis=1), pv, jnp.concatenate(m_parts, axis=1))

    @pl.when(p == n_steps - 1)
    def _():
        gate = jax.nn.sigmoid(gl_ref[...] + bg_ref[...].reshape(ng, 1, LANE))
        picked = stack(lambda g: sx_ref[g][:, N_PAST_BLK:N_PAST_BLK + 1]) > 0.5
        vis = [(tok_of_row >= j) & picked for j in range(DEC_SEQ)]
        s_new = [jnp.where(vis[j], sn, NEG_INF) for j, sn in enumerate(new_scores(kn_ref))]
        m_old = m_ref[...]
        m_new = m_old
        for sn in s_new:
            m_new = jnp.maximum(m_new, sn)
        alpha = jnp.exp(m_old - m_new)
        den = alpha * l_ref[...]
        acc = alpha * acc_ref[...]
        for j in range(DEC_SEQ):
            pj = jnp.where(vis[j], jnp.exp(s_new[j] - m_new), 0.0)
            den = den + pj
            acc = acc + pj * new_values(vn_ref, j)
        o_s = _safe_div(acc, den)
        o_w = ow_ref[...]
        for g in range(ng):
            rs = slice(g * ROWS, (g + 1) * ROWS)
            gt = [_per_head(r1, lambda r: gate[g][:, 3 * r + c:3 * r + c + 1]) for c in range(3)]
            o_ref[g] = gt[0] * oc_ref[g] + gt[1] * o_s[rs] + gt[2] * o_w[rs]


def nsa_sample(qg16, ps3, pool_k, pool_v, selx, buf_k, buf_v, o_c, gl16, b_gate, rel_bias, page_table):
    nb = qg16.shape[0]
    win_cols = slice(SELF_W + 4 * NSA_KV_W, SELF_W + 6 * NSA_KV_W)
    new_rows = ps3[:, :, win_cols].reshape(nb, DEC_SEQ, 2, NSA_KV_HEADS, HEAD_DIM)
    wk_rows = new_rows[:, :, 0].reshape(nb, DEC_SEQ * NSA_KV_HEADS, HEAD_DIM)
    wv_rows = new_rows[:, :, 1].reshape(nb, DEC_SEQ * NSA_KV_HEADS, HEAD_DIM)
    rspec = pl.BlockSpec((None, DEC_SEQ * NSA_KV_HEADS, HEAD_DIM), lambda b, p, pt: (b, 0, 0))
    n = NSA_PAGES_PER_STEP
    cb = lambda c: c // NSA_KV_W
    qspec = pl.BlockSpec((None, NSA_KV_HEADS, ROWS, HEAD_DIM), lambda b, p, pt: (b, 0, 0, 0))
    new = lambda c: pl.BlockSpec((None, DEC_SEQ, NSA_KV_W), lambda b, p, pt: (b, 0, cb(c)))
    page_w = PAGE_SIZE * NSA_KV_HEADS
    pool_k = pool_k.reshape(-1, page_w, HEAD_DIM)
    pool_v = pool_v.reshape(-1, page_w, HEAD_DIM)
    buf_k = buf_k.reshape(nb, WINDOW * NSA_KV_HEADS, HEAD_DIM)
    buf_v = buf_v.reshape(nb, WINDOW * NSA_KV_HEADS, HEAD_DIM)
    page = lambda j: pl.BlockSpec((None, page_w, HEAD_DIM), lambda b, p, pt: (pt[b, p * n + j], 0, 0))
    wspec = pl.BlockSpec((None, WINDOW * NSA_KV_HEADS, HEAD_DIM), lambda b, p, pt: (b, 0, 0))
    rows = NSA_KV_HEADS * ROWS
    grid_spec = pltpu.PrefetchScalarGridSpec(
        num_scalar_prefetch=1,
        grid=(nb, N_PAGES // n),
        in_specs=[pl.BlockSpec(memory_space=pltpu.SMEM), qspec]
        + [page(j) for j in range(n)] + [page(j) for j in range(n)] + [
            qspec,
            new(SELF_W + 2 * NSA_KV_W), new(SELF_W + 3 * NSA_KV_W),
            wspec, wspec,
            new(SELF_W + 4 * NSA_KV_W), new(SELF_W + 5 * NSA_KV_W),
            rspec, rspec,
            qspec, qspec,
            pl.BlockSpec((1, CROSS_W), lambda b, p, pt: (0, 0)),
        ],
        out_specs=[qspec, wspec, wspec],
        scratch_shapes=[pltpu.VMEM((rows, 1), F32), pltpu.VMEM((rows, 1), F32),
                        pltpu.VMEM((rows, HEAD_DIM), F32),
                        pltpu.VMEM((rows, HEAD_DIM), F32),
                        pltpu.VMEM((rows, page_w), F32),
                        pltpu.VMEM((NSA_KV_HEADS, ROWS, LANE), F32)],
    )
    o, win_k, win_v = pl.pallas_call(
        _nsa_sample_kernel,
        grid_spec=grid_spec,
        out_shape=[jax.ShapeDtypeStruct((nb, NSA_KV_HEADS, ROWS, HEAD_DIM), F32),
                   jax.ShapeDtypeStruct(buf_k.shape, F32), jax.ShapeDtypeStruct(buf_v.shape, F32)],
        compiler_params=_cparams(("arbitrary", "arbitrary")),
        name="nsa_sample",
    )(page_table, rel_bias, qg16, *([pool_k] * n), *([pool_v] * n), selx, ps3, ps3, buf_k, buf_v,
      ps3, ps3, wk_rows, wv_rows, o_c, gl16, b_gate)
    heads = (nb, WINDOW, NSA_KV_HEADS, HEAD_DIM)
    return o, win_k.reshape(heads), win_v.reshape(heads)


def _cross_sample_kernel(q_ref, mk_ref, mv_ref, o_ref):
    nh = N_CROSS_HEADS
    rows, width = nh * ROWS, MEM_LEN * nh
    own = (lax.broadcasted_iota(jnp.int32, (rows, width), 1) % nh
           == lax.broadcasted_iota(jnp.int32, (rows, width), 0) // ROWS)
    for i in range(q_ref.shape[0]):
        q = q_ref[i].reshape(rows, HEAD_DIM).astype(BF16)
        s = _dot_nt(q, mk_ref[i].astype(BF16)) * ATTN_SCALE
        p = _softmax_rows(s, own)
        o_ref[i] = _dot(p.astype(BF16), mv_ref[i].astype(BF16)).reshape(nh, ROWS, HEAD_DIM)


CROSS_SEQS_PER_STEP = 4


def cross_sample(qx16, mem_k, mem_v, layer):
    nb = qx16.shape[0]
    ns = CROSS_SEQS_PER_STEP
    n_layers = mem_k.shape[0]
    mem_k = mem_k.reshape(n_layers, nb, MEM_LEN * N_CROSS_HEADS, HEAD_DIM)
    mem_v = mem_v.reshape(n_layers, nb, MEM_LEN * N_CROSS_HEADS, HEAD_DIM)
    qspec = pl.BlockSpec((ns, N_CROSS_HEADS, ROWS, HEAD_DIM), lambda b: (b, 0, 0, 0))
    mspec = pl.BlockSpec((None, ns, MEM_LEN * N_CROSS_HEADS, HEAD_DIM), lambda b: (layer, b, 0, 0))
    return pl.pallas_call(
        _cross_sample_kernel,
        grid=(nb // ns,),
        in_specs=[qspec, mspec, mspec],
        out_specs=qspec,
        out_shape=jax.ShapeDtypeStruct((nb, N_CROSS_HEADS, ROWS, HEAD_DIM), F32),
        compiler_params=_cparams(("arbitrary",)),
        name="cross_sample",
    )(qx16, mem_k, mem_v)


def _arrange_in_weights(w, n_gate, per_group):
    gate = w[:, COL_QX:COL_QX + n_gate]
    if per_group:
        gate = jnp.pad(gate.reshape(-1, NSA_KV_HEADS, n_gate // NSA_KV_HEADS),
                       ((0, 0), (0, 0), (0, LANE - n_gate // NSA_KV_HEADS))).reshape(-1, CROSS_W)
    else:
        gate = jnp.pad(gate, ((0, 0), (0, CROSS_W - n_gate)))
    return jnp.concatenate([w[:, COL_QX + n_gate:], gate], axis=1)


def kernel(x_prompt, x_sample, mem_prompt, cache_nsa_cmp_k, cache_nsa_cmp_v, cache_nsa_slc_k, cache_nsa_slc_v, cache_nsa_win_k, cache_nsa_win_v, cache_fox_k, cache_fox_v, cache_fox_logf, cache_mem_k, cache_mem_v, page_table, rel_bias, norm_g, mem_norm_g, w_mem_kv, w_ff_gu, w_ff_down, w_in_nsa, b_gate_nsa, w_cmp1, w_cmp2, cmp_pe, w_out_nsa, w_in_fox, b_forget, w_out_fox, final_norm_g):
    n_p = BATCH * SEQ
    x = jnp.concatenate([x_prompt.reshape(n_p, D_MODEL), x_sample.reshape(-1, D_MODEL)], axis=0)
    mem = mem_prompt.reshape(BATCH * MEM_LEN, D_MODEL)
    n_pool = cache_nsa_cmp_k.shape[1]
    kv_heads = (BATCH, SEQ, NSA_KV_HEADS, HEAD_DIM)
    w_ff_gu = w_ff_gu.astype(BF16)
    w_ff_down = w_ff_down.astype(BF16)
    out = {}
    for i in range(DEPTH):
        x = ffn(x, norm_g[i, 0], w_ff_gu, w_ff_down, (i, 0))
        kvm = linear(mem, w_mem_kv[i], g=mem_norm_g[i], tm=512)
        out[f"mem_k{i}"] = kvm[:, :CROSS_W].reshape(BATCH, MEM_LEN, N_CROSS_HEADS, HEAD_DIM)
        out[f"mem_v{i}"] = kvm[:, CROSS_W:].reshape(BATCH, MEM_LEN, N_CROSS_HEADS, HEAD_DIM)
        a = i // 2
        if i % 2 == 0:
            w_tail = _arrange_in_weights(w_in_nsa[a], 3 * N_SELF_HEADS, True).astype(BF16)
            proj = linear(x, w_in_nsa[a].astype(BF16), w_tail, COL_QX // 512, g=norm_g[i, 1], tm=BIG_TOK_TILE)
            ps3 = proj[n_p:].reshape(DEC_BATCH, DEC_SEQ, PROJ_W)
            bg = jnp.pad(b_gate_nsa[a].reshape(NSA_KV_HEADS, 3 * NSA_GROUP),
                         ((0, 0), (0, LANE - 3 * NSA_GROUP))).reshape(1, CROSS_W)
            cmp_w = [(w_cmp1[a, c], w_cmp2[a, c], cmp_pe[a, c]) for c in range(2)]
            kvc = proj[:, SELF_W:SELF_W + 2 * NSA_KV_W]
            x_blk = kvc.reshape(N_TOK // CMP_BLOCK, CMP_BLOCK * 2 * NSA_KV_W)
            x_new = kvc.reshape(N_TOK // DEC_SEQ, DEC_SEQ * 2 * NSA_KV_W)
            pools = [cache_nsa_cmp_k[a], cache_nsa_cmp_v[a]]
            c_prompt, c_pool, c_new = [], [], []
            for c in range(2):
                col = lambda l, c=c: 2 * l + c
                c_prompt.append(compress(x_blk, col, CMP_BLOCK, *cmp_w[c], tr=BATCH * SEQ // CMP_BLOCK,
                                         rows=BATCH * SEQ // CMP_BLOCK))
                c_pool.append(compress_pool(pools[c], *cmp_w[c])
                              .reshape(n_pool, PAGE_SIZE // CMP_BLOCK, NSA_KV_W))
                c_new.append(compress(x_new, col, DEC_SEQ, *cmp_w[c], tr=DEC_BATCH, rows=DEC_BATCH,
                                      row_block0=n_p // DEC_SEQ // DEC_BATCH).reshape(DEC_BATCH, 1, NSA_KV_W))
            o_p, ox_p = nsa_prompt(proj, c_prompt[0], c_prompt[1], kvm, rel_bias, bg, BATCH)
            qg16 = _group_rows(ps3[:, :, :SELF_W])
            o_c, selx = nsa_sample_cmp(qg16, c_pool[0], c_pool[1], c_new[0], c_new[1], rel_bias, page_table)
            o_s, win_k, win_v = nsa_sample(
                qg16, ps3, cache_nsa_slc_k[a], cache_nsa_slc_v[a], selx,
                cache_nsa_win_k[a], cache_nsa_win_v[a],
                o_c, _gate_rows(ps3[:, :, COL_GATE:]), bg, rel_bias, page_table)
            o_s = _ungroup_rows(o_s)
            new_win = {"win_k": win_k, "win_v": win_v}
            for j, name in enumerate(("cmp_k", "cmp_v", "slc_k", "slc_v", "win_k", "win_v")):
                cols = slice(SELF_W + j * NSA_KV_W, SELF_W + (j + 1) * NSA_KV_W)
                st_p = proj[:n_p, cols].reshape(kv_heads)
                st_s = ps3[:, :, cols].reshape(DEC_BATCH, DEC_SEQ, NSA_KV_HEADS, HEAD_DIM)
                if name.startswith("win"):
                    st_p = st_p[:, -WINDOW:]
                    st_s = new_win[name]
                out.setdefault("p_" + name, []).append(st_p)
                out.setdefault("s_" + name, []).append(st_s)
            w_out = w_out_nsa[a]
        else:
            w_tail = _arrange_in_weights(w_in_fox[a], FOX_HEADS, False).astype(BF16)
            proj = linear(x, w_in_fox[a].astype(BF16), w_tail, COL_QX // 512, g=norm_g[i, 1], tm=BIG_TOK_TILE)
            ps3 = proj[n_p:].reshape(DEC_BATCH, DEC_SEQ, PROJ_W)
            lf_p, ct = fox_prep(proj, b_forget[a], BATCH)
            n_s = DEC_BATCH * DEC_SEQ
            lf_s, _ = fox_prep(proj, b_forget[a], 1, seq_len=n_s, row_block0=n_p // n_s)
            o_p = fox_prompt(proj, ct, BATCH)
            ox_p = cross_prompt(proj, kvm, BATCH)
            ct_pool = cumsum_lanes(jnp.transpose(cache_fox_logf[a], (2, 0, 1)))
            lfn = lf_s[:, :HEAD_PAD].reshape(DEC_BATCH, DEC_SEQ, HEAD_PAD).transpose(0, 2, 1)
            o_s = _unhead_rows(fox_sample(
                _head_rows(ps3[:, :, :SELF_W], FOX_HEADS), ps3,
                jnp.transpose(cache_fox_k[a], (0, 2, 1, 3)), jnp.transpose(cache_fox_v[a], (0, 2, 1, 3)),
                ct_pool, lfn, page_table))
            heads = (FOX_HEADS, HEAD_DIM)
            for j, name in enumerate(("fox_k", "fox_v")):
                cols = slice((j + 1) * SELF_W, (j + 2) * SELF_W)
                out.setdefault("p_" + name, []).append(proj[:n_p, cols].reshape(BATCH, SEQ, *heads))
                out.setdefault("s_" + name, []).append(ps3[:, :, cols].reshape(DEC_BATCH, DEC_SEQ, *heads))
            out.setdefault("p_fox_logf", []).append(lf_p[:, :FOX_HEADS].reshape(BATCH, SEQ, FOX_HEADS))
            out.setdefault("s_fox_logf", []).append(lf_s[:, :FOX_HEADS].reshape(DEC_BATCH, DEC_SEQ, FOX_HEADS))
            w_out = w_out_fox[a]
        ox_s = _unhead_rows(cross_sample(
            _head_rows(ps3[:, :, COL_QX:COL_GATE], N_CROSS_HEADS), cache_mem_k, cache_mem_v, i))
        mix_self = jnp.concatenate([o_p, o_s.astype(BF16)], axis=0)
        mix_cross = jnp.concatenate([ox_p, ox_s.astype(BF16)], axis=0)
        x = linear([mix_self, mix_cross], w_out.astype(BF16), res=x, tm=BIG_TOK_TILE)
        x = ffn(x, norm_g[i, 2], w_ff_gu, w_ff_down, (i, 1),
                final_g=final_norm_g if i == DEPTH - 1 else None)
    y_prompt = x[:n_p].reshape(BATCH, SEQ, D_MODEL)
    y_sample = x[n_p:].reshape(DEC_BATCH, DEC_SEQ, D_MODEL)
    st = lambda name: jnp.stack(out[name])
    p_mem_k = jnp.stack([out[f"mem_k{i}"] for i in range(DEPTH)])
    p_mem_v = jnp.stack([out[f"mem_v{i}"] for i in range(DEPTH)])
    return (y_prompt, y_sample,
            st("p_cmp_k"), st("p_cmp_v"), st("p_slc_k"), st("p_slc_v"), st("p_win_k"), st("p_win_v"),
            st("p_fox_k"), st("p_fox_v"), st("p_fox_logf"), p_mem_k, p_mem_v,
            st("s_cmp_k"), st("s_cmp_v"), st("s_slc_k"), st("s_slc_v"), st("s_win_k"), st("s_win_v"),
            st("s_fox_k"), st("s_fox_v"), st("s_fox_logf"))
```
